```python
import math
import jax, jax.numpy as jnp
from jax import lax
import numpy as np

D_MODEL = 1024
BATCH = 8
SEQ = 2048
DEPTH = 2
DEC_BATCH = 128
DEC_SEQ = 4
PAST_LEN = 16384
PAGE_SIZE = 128

D_MIX = D_MODEL
W_A = D_MIX // 4
NH_A = 4
HD_A = W_A // NH_A
CONV_A = 4
LRU_C = 8.0
W_B = D_MIX // 4
CONV_B = 31
W_C = D_MIX // 2
NH_C = 8
HD_C = W_C // NH_C
RET_CHUNK = 128
ROPE_BASE = 10000.0
N_GROUPS = 4
EXPERTS_PER_GROUP = 8
N_EXPERTS = N_GROUPS * EXPERTS_PER_GROUP
TOP_K = 2
D_EXPERT = D_MODEL // 4
D_IN = 2 * W_A + 2 * W_B + 4 * W_C
EPS = 1e-6

kernel_name = "hymba_style_lru_conv_retention_hmoe_step"


def rmsnorm(x, g):
    xf = x.astype(jnp.float32)
    y = xf * lax.rsqrt(jnp.mean(xf * xf, axis=-1, keepdims=True) + EPS)
    return (y * g.astype(jnp.float32)).astype(x.dtype)


def layernorm_f32(x, g, b):
    mu = jnp.mean(x, axis=-1, keepdims=True)
    xc = x - mu
    var = jnp.mean(xc * xc, axis=-1, keepdims=True)
    return xc * lax.rsqrt(var + EPS) * g.astype(jnp.float32) + b.astype(jnp.float32)


def causal_dwconv(x, buf, w, b):
    xp = jnp.concatenate([buf.astype(x.dtype), x], axis=1)
    y = lax.conv_general_dilated(xp, w[:, None, :].astype(x.dtype), window_strides=(1,), padding='VALID',
                                 dimension_numbers=('NWC', 'WIO', 'NWC'), feature_group_count=x.shape[-1])
    return y + b.astype(x.dtype), xp[:, -(w.shape[0] - 1):]


def rglru(xc, h0, wa, ba, wx, bx, lam):
    B, T, W = xc.shape
    xf = xc.astype(jnp.float32)
    xh = xf.reshape(B, T, NH_A, HD_A)
    r = jax.nn.sigmoid(jnp.einsum('bthi,hij->bthj', xh, wa.astype(jnp.float32)).reshape(B, T, W) + ba.astype(jnp.float32))
    i = jax.nn.sigmoid(jnp.einsum('bthi,hij->bthj', xh, wx.astype(jnp.float32)).reshape(B, T, W) + bx.astype(jnp.float32))
    log_a = -LRU_C * r * jax.nn.softplus(-lam.astype(jnp.float32))
    a = jnp.exp(log_a)
    bterm = jnp.sqrt(-jnp.expm1(2.0 * log_a)) * (i * xf)
    bterm = bterm.at[:, 0].add(a[:, 0] * h0.astype(jnp.float32))

    def combine(c1, c2):
        a1, b1 = c1
        a2, b2 = c2
        return a1 * a2, a2 * b1 + b2

    _, h = lax.associative_scan(combine, (a, bterm), axis=1)
    return h, h[:, -1]


def rope(x, pos):
    half = x.shape[-1] // 2
    inv = ROPE_BASE ** (-jnp.arange(half, dtype=jnp.float32) / half)
    ang = pos.astype(jnp.float32)[:, None] * inv[None, :]
    cos = jnp.cos(ang)[None, :, None, :]
    sin = jnp.sin(ang)[None, :, None, :]
    x1, x2 = x[..., :half], x[..., half:]
    return jnp.concatenate([x1 * cos - x2 * sin, x1 * sin + x2 * cos], axis=-1)


def retention(q, k, v, S0, chunk):
    B, T, H, D = q.shape
    N = T // chunk
    log_g = jnp.log1p(-jnp.exp2(-5.0 - jnp.arange(H, dtype=jnp.float32)))
    idx = jnp.arange(chunk, dtype=jnp.float32)
    diff = idx[:, None] - idx[None, :]
    causal = diff >= 0
    dmat = jnp.where(causal[None], jnp.exp(jnp.where(causal, diff, 0.0)[None] * log_g[:, None, None]), 0.0)
    xi = jnp.exp((idx + 1.0)[None, :] * log_g[:, None])
    zeta = jnp.exp((chunk - 1.0 - idx)[None, :] * log_g[:, None])
    g_chunk = jnp.exp(chunk * log_g)

    def to_chunks(t):
        return t.reshape(B, N, chunk, H, D).transpose(1, 0, 3, 2, 4)

    def step(S, qkv):
        qc, kc, vc = qkv
        inner = jnp.einsum('bhid,bhjd->bhij', qc, kc) * dmat
        o = jnp.einsum('bhij,bhjd->bhid', inner, vc) + jnp.einsum('bhid,bhde->bhie', qc, S) * xi[None, :, :, None]
        S_new = g_chunk[None, :, None, None] * S + jnp.einsum('bhjd,bhje->bhde', kc * zeta[None, :, :, None], vc)
        return S_new, o

    S_last, o = lax.scan(step, S0.astype(jnp.float32), (to_chunks(q), to_chunks(k), to_chunks(v)))
    o = o.transpose(1, 0, 3, 2, 4).reshape(B, T, H, D)
    return o, S_last


def hier_moe(x, wg, bg, we, be, w1, w3, w2):
    B, T, D = x.shape
    xt = x.reshape(B * T, D)
    xf = xt.astype(jnp.float32)
    pg = jax.nn.softmax(xf @ wg.astype(jnp.float32) + bg.astype(jnp.float32), axis=-1)
    g_sel = jnp.argmax(pg, axis=-1)
    p_top = jnp.take_along_axis(pg, g_sel[:, None], axis=-1)
    le = (xf @ we.astype(jnp.float32) + be.astype(jnp.float32)).reshape(-1, N_GROUPS, EXPERTS_PER_GROUP)
    le_sel = jnp.take_along_axis(le, g_sel[:, None, None], axis=1)[:, 0]
    top_v, top_i = lax.top_k(le_sel, TOP_K)
    pe = jax.nn.softmax(top_v, axis=-1)
    eid = g_sel[:, None] * EXPERTS_PER_GROUP + top_i
    gates = jnp.sum(jax.nn.one_hot(eid, N_EXPERTS, dtype=jnp.float32) * (p_top * pe)[..., None], axis=1)

    def expert(acc, params):
        w1e, w3e, w2e, ge = params
        h = jax.nn.silu(xt @ w1e) * (xt @ w3e)
        return acc + ge[:, None] * (h @ w2e).astype(jnp.float32), None

    y, _ = lax.scan(expert, jnp.zeros((B * T, D), jnp.float32), (w1, w3, w2, gates.T))
    return y.astype(x.dtype).reshape(B, T, D)


def setup_inputs(seed: int = 0) -> dict:
    key = jax.random.key(seed)
    ks = jax.random.split(key, 40)
    nrm = lambda k, shape, s: jax.random.normal(k, shape, jnp.float32) * s
    a0 = jax.random.uniform(ks[14], (DEPTH, W_A), jnp.float32, 0.9, 0.999)
    return {
        "x_prompt": nrm(ks[0], (BATCH, SEQ, D_MODEL), 1.0),
        "x_sample": nrm(ks[1], (DEC_BATCH, DEC_SEQ, D_MODEL), 1.0),
        "state_lru_h": nrm(ks[2], (DEPTH, DEC_BATCH, W_A), 0.5),
        "state_lru_conv": nrm(ks[3], (DEPTH, DEC_BATCH, CONV_A - 1, W_A), 1.0),
        "state_conv": nrm(ks[4], (DEPTH, DEC_BATCH, CONV_B - 1, W_B), 0.5),
        "state_ret": nrm(ks[5], (DEPTH, DEC_BATCH, NH_C, HD_C, HD_C), 2.0),
        "norm_mix_g": 1.0 + nrm(ks[6], (DEPTH, D_MODEL), 0.01),
        "norm_ffn_g": 1.0 + nrm(ks[7], (DEPTH, D_MODEL), 0.01),
        "final_norm_g": 1.0 + nrm(ks[8], (D_MODEL,), 0.01),
        "w_in": nrm(ks[9], (DEPTH, D_MODEL, D_IN), D_MODEL ** -0.5),
        "lru_conv_w": nrm(ks[10], (DEPTH, CONV_A, W_A), CONV_A ** -0.5),
        "lru_conv_b": nrm(ks[11], (DEPTH, W_A), 0.02),
        "lru_wa": nrm(ks[12], (DEPTH, NH_A, HD_A, HD_A), HD_A ** -0.5),
        "lru_ba": nrm(ks[13], (DEPTH, W_A), 0.1),
        "lru_wx": nrm(ks[15], (DEPTH, NH_A, HD_A, HD_A), HD_A ** -0.5),
        "lru_bx": nrm(ks[16], (DEPTH, W_A), 0.1),
        "lru_lambda": jnp.log(a0) - jnp.log1p(-a0),
        "cm_dw_w": nrm(ks[17], (DEPTH, CONV_B, W_B), CONV_B ** -0.5),
        "cm_dw_b": nrm(ks[18], (DEPTH, W_B), 0.02),
        "cm_ln_g": 1.0 + nrm(ks[19], (DEPTH, W_B), 0.01),
        "cm_ln_b": nrm(ks[20], (DEPTH, W_B), 0.02),
        "ret_norm_g": 1.0 + nrm(ks[21], (DEPTH, W_C), 0.01),
        "w_out": nrm(ks[22], (DEPTH, D_MIX, D_MODEL), D_MIX ** -0.5),
        "moe_w_group": nrm(ks[23], (DEPTH, D_MODEL, N_GROUPS), D_MODEL ** -0.5),
        "moe_b_group": nrm(ks[24], (DEPTH, N_GROUPS), 0.01),
        "moe_w_expert": nrm(ks[25], (DEPTH, D_MODEL, N_EXPERTS), D_MODEL ** -0.5),
        "moe_b_expert": nrm(ks[26], (DEPTH, N_EXPERTS), 0.01),
        "moe_w1": nrm(ks[27], (DEPTH, N_EXPERTS, D_MODEL, D_EXPERT), D_MODEL ** -0.5),
        "moe_w3": nrm(ks[28], (DEPTH, N_EXPERTS, D_MODEL, D_EXPERT), D_MODEL ** -0.5),
        "moe_w2": nrm(ks[29], (DEPTH, N_EXPERTS, D_EXPERT, D_MODEL), D_EXPERT ** -0.5),
    }


def reference(x_prompt, x_sample, state_lru_h, state_lru_conv, state_conv, state_ret,
              norm_mix_g, norm_ffn_g, final_norm_g, w_in, lru_conv_w, lru_conv_b,
              lru_wa, lru_ba, lru_wx, lru_bx, lru_lambda, cm_dw_w, cm_dw_b, cm_ln_g, cm_ln_b,
              ret_norm_g, w_out, moe_w_group, moe_b_group, moe_w_expert, moe_b_expert,
              moe_w1, moe_w3, moe_w2):
    splits = [W_A, 2 * W_A, 2 * W_A + 2 * W_B, 2 * W_A + 2 * W_B + W_C,
              2 * W_A + 2 * W_B + 2 * W_C, 2 * W_A + 2 * W_B + 3 * W_C]

    def layer(x, h0, buf_a, buf_b, S0, pos, l):
        Bn, T, _ = x.shape
        u = rmsnorm(x, norm_mix_g[l])
        proj = u @ w_in[l]
        a_gate, a_x, b_in, c_q, c_k, c_v, c_g = jnp.split(proj, splits, axis=-1)
        xa, nbuf_a = causal_dwconv(a_x, buf_a, lru_conv_w[l], lru_conv_b[l])
        ha, h_last = rglru(xa, h0, lru_wa[l], lru_ba[l], lru_wx[l], lru_bx[l], lru_lambda[l])
        out_a = jax.nn.gelu(a_gate.astype(jnp.float32)) * ha
        glu = b_in[..., :W_B] * jax.nn.sigmoid(b_in[..., W_B:])
        cb, nbuf_b = causal_dwconv(glu, buf_b, cm_dw_w[l], cm_dw_b[l])
        out_b = jax.nn.silu(layernorm_f32(cb.astype(jnp.float32), cm_ln_g[l], cm_ln_b[l]))
        q = rope(c_q.reshape(Bn, T, NH_C, HD_C).astype(jnp.float32), pos)
        k = rope(c_k.reshape(Bn, T, NH_C, HD_C).astype(jnp.float32), pos) * (HD_C ** -0.5)
        v = c_v.reshape(Bn, T, NH_C, HD_C).astype(jnp.float32)
        o, S_last = retention(q, k, v, S0, math.gcd(T, RET_CHUNK))
        mu = jnp.mean(o, axis=-1, keepdims=True)
        oc = o - mu
        o = oc * lax.rsqrt(jnp.mean(oc * oc, axis=-1, keepdims=True) + EPS)
        o = o.reshape(Bn, T, W_C) * ret_norm_g[l].astype(jnp.float32)
        out_c = jax.nn.silu(c_g.astype(jnp.float32)) * o
        mix = jnp.concatenate([out_a, out_b, out_c], axis=-1).astype(x.dtype) @ w_out[l]
        x = x + mix
        x = x + hier_moe(rmsnorm(x, norm_ffn_g[l]), moe_w_group[l], moe_b_group[l], moe_w_expert[l],
                         moe_b_expert[l], moe_w1[l], moe_w3[l], moe_w2[l])
        return x, h_last, nbuf_a, nbuf_b, S_last

    Bp, Tp, _ = x_prompt.shape
    Bs, Ts, _ = x_sample.shape
    pos_p = jnp.arange(Tp, dtype=jnp.int32)
    pos_s = PAST_LEN + jnp.arange(Ts, dtype=jnp.int32)
    xp, xs = x_prompt, x_sample
    p_h, p_ca, p_cb, p_S = [], [], [], []
    s_h, s_ca, s_cb, s_S = [], [], [], []
    for l in range(DEPTH):
        xp, h, ca, cb, S = layer(xp, jnp.zeros((Bp, W_A), jnp.float32),
                                 jnp.zeros((Bp, CONV_A - 1, W_A), xp.dtype),
                                 jnp.zeros((Bp, CONV_B - 1, W_B), xp.dtype),
                                 jnp.zeros((Bp, NH_C, HD_C, HD_C), jnp.float32), pos_p, l)
        p_h.append(h); p_ca.append(ca); p_cb.append(cb); p_S.append(S)
        xs, h, ca, cb, S = layer(xs, state_lru_h[l], state_lru_conv[l], state_conv[l], state_ret[l], pos_s, l)
        s_h.append(h); s_ca.append(ca); s_cb.append(cb); s_S.append(S)
    y_prompt = rmsnorm(xp, final_norm_g)
    y_sample = rmsnorm(xs, final_norm_g)
    return (y_prompt, y_sample,
            jnp.stack(p_h), jnp.stack(p_ca), jnp.stack(p_cb), jnp.stack(p_S),
            jnp.stack(s_h), jnp.stack(s_ca), jnp.stack(s_cb), jnp.stack(s_S))
```

```python
import functools
import math

import jax
import jax.numpy as jnp
import numpy as np
from jax import lax
from jax.experimental import pallas as pl
from jax.experimental.pallas import tpu as pltpu

PAST_LEN = 16384
NH_A = 4
LRU_C = 8.0
CONV_A = 4
CONV_B = 31
NH_C = 8
RET_CHUNK = 128
ROPE_BASE = 10000.0
N_GROUPS = 4
EXPERTS_PER_GROUP = 8
N_EXPERTS = N_GROUPS * EXPERTS_PER_GROUP
EPS = 1e-6

LANES = 128
SUBLANES = 8
VMEM_LIMIT_BYTES = 56 * 1024 * 1024

F32 = jnp.float32
BF16 = jnp.bfloat16
HIGHEST = lax.Precision.HIGHEST


def _cparams(*sem):
    return pltpu.CompilerParams(dimension_semantics=sem, vmem_limit_bytes=VMEM_LIMIT_BYTES)


def _row_tile(n):
    for t in (512, 256, 128, 64, 32, 16, 8):
        if n % t == 0:
            return t
    raise ValueError(f"token count {n} must be a multiple of 8")


def _rms(x, g):
    ms = jnp.mean(x * x, axis=-1, keepdims=True)
    return x * lax.rsqrt(ms + EPS) * g


def _sigmoid(x):
    return jax.nn.sigmoid(x)


def _silu(x):
    return x * jax.nn.sigmoid(x)


def _in_proj_kernel(x_ref, g_ref, w_ref, o_ref):
    u = _rms(x_ref[...], g_ref[...]).astype(BF16)
    o_ref[...] = jnp.dot(u, w_ref[...], preferred_element_type=F32)


def _in_proj(x, g, w):
    n, d = x.shape
    d_in = w.shape[1]
    tm = _row_tile(n)
    return pl.pallas_call(
        _in_proj_kernel,
        grid=(n // tm,),
        in_specs=[
            pl.BlockSpec((tm, d), lambda i: (i, 0)),
            pl.BlockSpec((1, d), lambda i: (0, 0)),
            pl.BlockSpec((d, d_in), lambda i: (0, 0)),
        ],
        out_specs=pl.BlockSpec((tm, d_in), lambda i: (i, 0)),
        out_shape=jax.ShapeDtypeStruct((n, d_in), F32),
        compiler_params=_cparams("arbitrary"),
        name="in_proj",
    )(x, g, w)


def _rope_pair(x, cos, sin_signed, first_half):
    partner = jnp.where(first_half, pltpu.roll(x, 96, 1), pltpu.roll(x, 32, 1))
    return x * cos + partner * sin_signed


def _head_norm_pair(o, head0, inv_d):
    zero = jnp.zeros_like(o)
    s0 = jnp.sum(jnp.where(head0, o, zero), axis=-1, keepdims=True)
    s1 = jnp.sum(jnp.where(head0, zero, o), axis=-1, keepdims=True)
    oc = o - jnp.where(head0, s0, s1) * inv_d
    q = oc * oc
    v0 = jnp.sum(jnp.where(head0, q, zero), axis=-1, keepdims=True)
    v1 = jnp.sum(jnp.where(head0, zero, q), axis=-1, keepdims=True)
    return oc * lax.rsqrt(jnp.where(head0, v0, v1) * inv_d + EPS)


def _retention_intra_pair(qb, kb, vb, dmat0, dmat1, head0):
    outs = []
    for j, dm in ((0, dmat0), (1, dmat1)):
        keep = head0 if j == 0 else jnp.logical_not(head0)
        qm = jnp.where(keep, qb, jnp.zeros_like(qb))
        s = lax.dot_general(qm, kb, (((1,), (1,)), ((), ())), preferred_element_type=F32)
        inner = (s * dm).astype(BF16)
        outs.append(jnp.dot(inner, vb, preferred_element_type=F32))
    return jnp.where(head0, outs[0], outs[1])


def _lru_gates(xa, a_gate_unused, wg, bg, lam):
    w_a = xa.shape[-1]
    gates = jnp.dot(xa.astype(BF16), wg, preferred_element_type=F32) + bg
    r = _sigmoid(gates[:, :w_a])
    i = _sigmoid(gates[:, w_a:])
    z = -lam
    softplus = jnp.maximum(z, 0.0) + jnp.log1p(jnp.exp(-jnp.abs(z)))
    log_a = -LRU_C * r * softplus
    a = jnp.exp(log_a)
    bt = jnp.sqrt(-jnp.tanh(log_a) * (a * a + 1.0)) * (i * xa)
    return a, bt


def _layernorm(x, g, b):
    mu = jnp.mean(x, axis=-1, keepdims=True)
    xc = x - mu
    var = jnp.mean(xc * xc, axis=-1, keepdims=True)
    return xc * lax.rsqrt(var + EPS) * g + b


CONV_B_ROWS = 64


def _mix_prompt_kernel(proj_ref, cos_ref, sin_ref, h0_ref, bufa0_ref, bufb0_ref, s0_ref,
                       caw_ref, cab_ref, wg_ref, bg_ref, lam_ref,
                       cbw_ref, cbb_ref, lng_ref, lnb_ref, rng_ref,
                       dmat_ref, xi_ref, zeta_ref, gmat_ref,
                       mix_ref, hout_ref, bufa_out_ref, bufb_out_ref, sout_ref,
                       xpa_ref, xpb_ref, a_ref, b_ref, hseq_ref, hcar_ref, s_ref,
                       *, tc, w_a, w_b, w_c, chunk):
    t = pl.program_id(1)
    n_t = pl.num_programs(1)
    pa = SUBLANES
    pb = 4 * SUBLANES

    @pl.when(t == 0)
    def _():
        xpa_ref[0:pa, :] = bufa0_ref[0]
        xpb_ref[0:pb, :] = bufb0_ref[0]
        hcar_ref[...] = h0_ref[0]
        s_ref[...] = s0_ref[0]

    a_gate = proj_ref[:, 0:w_a]
    xpa_ref[pa:pa + tc, :] = proj_ref[:, w_a:2 * w_a]
    xa = cab_ref[...] + caw_ref[0:1, :] * xpa_ref[pa - 3:pa - 3 + tc, :]
    for j in range(1, CONV_A):
        xa = xa + caw_ref[j:j + 1, :] * xpa_ref[pa - 3 + j:pa - 3 + j + tc, :]
    a, bt = _lru_gates(xa, None, wg_ref[...], bg_ref[...], lam_ref[...])
    a_ref[...] = a
    b_ref[...] = bt
    row8 = lax.broadcasted_iota(jnp.int32, (SUBLANES, w_a), 0)

    def scan_block(i, hprev):
        r0 = pl.multiple_of(i * SUBLANES, SUBLANES)
        ab = a_ref[pl.ds(r0, SUBLANES), :]
        bb = b_ref[pl.ds(r0, SUBLANES), :]
        for k in (1, 2, 4):
            a_sh = jnp.where(row8 >= k, pltpu.roll(ab, k, 0), 1.0)
            b_sh = jnp.where(row8 >= k, pltpu.roll(bb, k, 0), 0.0)
            bb = ab * b_sh + bb
            ab = ab * a_sh
        h = ab * hprev + bb
        hseq_ref[pl.ds(r0, SUBLANES), :] = h
        return jnp.broadcast_to(h[SUBLANES - 1:SUBLANES, :], (SUBLANES, w_a))

    hcar_ref[...] = lax.fori_loop(0, tc // SUBLANES, scan_block, hcar_ref[...])
    out_a = jax.nn.gelu(a_gate) * hseq_ref[...]
    mix_ref[:, 0:w_a] = out_a.astype(BF16)
    xpa_ref[0:pa, :] = xpa_ref[tc:tc + pa, :]

    b0 = 2 * w_a
    xpb_ref[pb:pb + tc, :] = proj_ref[:, b0:b0 + w_b] * _sigmoid(proj_ref[:, b0 + w_b:b0 + 2 * w_b])
    first = pb - (CONV_B - 1)
    for blk in range(tc // CONV_B_ROWS):
        r0 = blk * CONV_B_ROWS
        acc = cbb_ref[...] + cbw_ref[0:1, :] * xpb_ref[first + r0:first + r0 + CONV_B_ROWS, :]
        for j in range(1, CONV_B):
            acc = acc + cbw_ref[j:j + 1, :] * xpb_ref[first + r0 + j:first + r0 + j + CONV_B_ROWS, :]
        out_b = _silu(_layernorm(acc, lng_ref[...], lnb_ref[...]))
        mix_ref[r0:r0 + CONV_B_ROWS, w_a:w_a + w_b] = out_b.astype(BF16)
    xpb_ref[0:pb, :] = xpb_ref[tc:tc + pb, :]

    c0 = 2 * w_a + 2 * w_b
    n_pair = w_c // LANES
    lane = lax.broadcasted_iota(jnp.int32, (chunk, LANES), 1)
    head0 = lane < (LANES // 2)
    first_half = (lane % (LANES // 2)) < (LANES // 4)
    rl = lax.broadcasted_iota(jnp.int32, (LANES, LANES), 0) < (LANES // 2)
    cl = lax.broadcasted_iota(jnp.int32, (LANES, LANES), 1) < (LANES // 2)
    same_head = rl == cl
    inv_d = 2.0 / LANES
    for c in range(tc // chunk):
        rows = slice(c * chunk, (c + 1) * chunk)
        cos = cos_ref[rows, :]
        sin = sin_ref[rows, :]
        for p in range(n_pair):
            cq = c0 + p * LANES
            q = _rope_pair(proj_ref[rows, cq:cq + LANES], cos, sin, first_half)
            k = _rope_pair(proj_ref[rows, cq + w_c:cq + w_c + LANES], cos, sin, first_half) * (2.0 / LANES) ** 0.5
            v = proj_ref[rows, cq + 2 * w_c:cq + 2 * w_c + LANES]
            g = proj_ref[rows, cq + 3 * w_c:cq + 3 * w_c + LANES]
            qb, kb, vb = q.astype(BF16), k.astype(BF16), v.astype(BF16)
            o = _retention_intra_pair(qb, kb, vb, dmat_ref[2 * p], dmat_ref[2 * p + 1], head0)
            s_old = s_ref[p]
            o = o + jnp.dot(qb, s_old.astype(BF16), preferred_element_type=F32) * xi_ref[p]
            kz = (k * zeta_ref[p]).astype(BF16)
            upd = lax.dot_general(kz, vb, (((0,), (0,)), ((), ())), preferred_element_type=F32)
            s_ref[p] = gmat_ref[p] * s_old + jnp.where(same_head, upd, 0.0)
            on = _head_norm_pair(o, head0, inv_d) * rng_ref[:, p * LANES:(p + 1) * LANES]
            mix_ref[rows, w_a + w_b + p * LANES:w_a + w_b + (p + 1) * LANES] = (_silu(g) * on).astype(BF16)

    @pl.when(t == n_t - 1)
    def _():
        hout_ref[0] = hcar_ref[...]
        bufa_out_ref[0] = xpa_ref[0:pa, :]
        bufb_out_ref[0] = xpb_ref[0:pb, :]
        sout_ref[0] = s_ref[...]


def _retention_tables(chunk):
    log_g = jnp.log1p(-jnp.exp2(-5.0 - jnp.arange(NH_C, dtype=F32)))
    idx = jnp.arange(chunk, dtype=F32)
    diff = idx[:, None] - idx[None, :]
    causal = diff >= 0
    dmat = jnp.where(causal[None], jnp.exp(jnp.where(causal, diff, 0.0)[None] * log_g[:, None, None]), 0.0)
    xi = jnp.exp((idx + 1.0)[None, :] * log_g[:, None])
    zeta = jnp.exp((chunk - 1.0 - idx)[None, :] * log_g[:, None])
    g_chunk = jnp.exp(chunk * log_g)
    return dmat, xi, zeta, g_chunk


def _pair_lanes(per_head):
    h, rows = per_head.shape
    half = LANES // 2
    x = jnp.broadcast_to(per_head.reshape(h // 2, 2, rows, 1), (h // 2, 2, rows, half))
    return x.transpose(0, 2, 1, 3).reshape(h // 2, rows, LANES)


def _rope_tables(pos):
    half = LANES // 4
    inv = ROPE_BASE ** (-jnp.arange(half, dtype=F32) / half)
    ang = pos.astype(F32)[:, None] * inv[None, :]
    cos, sin = jnp.cos(ang), jnp.sin(ang)
    return jnp.concatenate([cos] * 4, axis=1), jnp.concatenate([-sin, sin, -sin, sin], axis=1)


def _blockdiag_gate_weights(wa, wx):
    nh, hd, _ = wa.shape
    eye = jnp.eye(nh, dtype=wa.dtype)
    da = (eye[:, None, :, None] * wa[:, :, None, :]).reshape(nh * hd, nh * hd)
    dx = (eye[:, None, :, None] * wx[:, :, None, :]).reshape(nh * hd, nh * hd)
    return jnp.concatenate([da, dx], axis=1)


def _mix_prompt(proj, n_seq, seq_len, lw):
    w_a, w_b, w_c = lw["w_a"], lw["w_b"], lw["w_c"]
    d_in = proj.shape[1]
    d_mix = w_a + w_b + w_c
    chunk = math.gcd(seq_len, RET_CHUNK)
    tc = 256 if seq_len % 256 == 0 else chunk
    n_t = seq_len // tc
    n_pair = w_c // LANES
    dmat, xi, zeta, g_chunk = _retention_tables(chunk)
    xi_p = _pair_lanes(xi)
    zeta_p = _pair_lanes(zeta)
    half = LANES // 2
    gm = jnp.zeros((n_pair, 2, half, 2, half), F32)
    gm = gm.at[:, 0, :, 0, :].set(g_chunk[0::2, None, None]).at[:, 1, :, 1, :].set(g_chunk[1::2, None, None])
    gmat = gm.reshape(n_pair, LANES, LANES)
    cos, sin = _rope_tables(jnp.arange(seq_len, dtype=jnp.int32))
    zeros = lambda *s: jnp.zeros(s, F32)
    full = lambda shape: pl.BlockSpec(shape, lambda b, t: (0,) * len(shape))
    per_seq = lambda shape: pl.BlockSpec((1,) + shape, lambda b, t: (b,) + (0,) * len(shape))
    kern = functools.partial(_mix_prompt_kernel, tc=tc, w_a=w_a, w_b=w_b, w_c=w_c, chunk=chunk)
    return pl.pallas_call(
        kern,
        grid=(n_seq, n_t),
        in_specs=[
            pl.BlockSpec((tc, d_in), lambda b, t: (b * n_t + t, 0)),
            pl.BlockSpec((tc, LANES), lambda b, t: (t, 0)),
            pl.BlockSpec((tc, LANES), lambda b, t: (t, 0)),
            per_seq((SUBLANES, w_a)), per_seq((SUBLANES, w_a)), per_seq((4 * SUBLANES, w_b)),
            per_seq((n_pair, LANES, LANES)),
            full((CONV_A, w_a)), full((1, w_a)), full((w_a, 2 * w_a)), full((1, 2 * w_a)), full((1, w_a)),
            full((CONV_B, w_b)), full((1, w_b)), full((1, w_b)), full((1, w_b)), full((1, w_c)),
            full((NH_C, chunk, chunk)), full((n_pair, chunk, LANES)), full((n_pair, chunk, LANES)),
            full((n_pair, LANES, LANES)),
        ],
        out_specs=[
            pl.BlockSpec((tc, d_mix), lambda b, t: (b * n_t + t, 0)),
            per_seq((SUBLANES, w_a)), per_seq((SUBLANES, w_a)), per_seq((4 * SUBLANES, w_b)),
            per_seq((n_pair, LANES, LANES)),
        ],
        out_shape=[
            jax.ShapeDtypeStruct((n_seq * seq_len, d_mix), BF16),
            jax.ShapeDtypeStruct((n_seq, SUBLANES, w_a), F32),
            jax.ShapeDtypeStruct((n_seq, SUBLANES, w_a), F32),
            jax.ShapeDtypeStruct((n_seq, 4 * SUBLANES, w_b), F32),
            jax.ShapeDtypeStruct((n_seq, n_pair, LANES, LANES), F32),
        ],
        scratch_shapes=[
            pltpu.VMEM((SUBLANES + tc, w_a), F32),
            pltpu.VMEM((4 * SUBLANES + tc, w_b), F32),
            pltpu.VMEM((tc, w_a), F32),
            pltpu.VMEM((tc, w_a), F32),
            pltpu.VMEM((tc, w_a), F32),
            pltpu.VMEM((SUBLANES, w_a), F32),
            pltpu.VMEM((n_pair, LANES, LANES), F32),
        ],
        compiler_params=_cparams("arbitrary", "arbitrary"),
        name="mix_prompt",
    )(proj, cos, sin,
      zeros(n_seq, SUBLANES, w_a), zeros(n_seq, SUBLANES, w_a), zeros(n_seq, 4 * SUBLANES, w_b),
      zeros(n_seq, n_pair, LANES, LANES),
      lw["conv_a_w"], lw["conv_a_b"], lw["wg"], lw["bg"], lw["lam"],
      lw["conv_b_w"], lw["conv_b_b"], lw["ln_g"], lw["ln_b"], lw["ret_g"],
      dmat, xi_p, zeta_p, gmat)


def _unpair_state(s_pairs):
    b, n_pair = s_pairs.shape[:2]
    half = LANES // 2
    s = s_pairs.reshape(b, n_pair, 2, half, 2, half)
    return jnp.stack([s[:, :, 0, :, 0, :], s[:, :, 1, :, 1, :]], axis=2).reshape(b, 2 * n_pair, half, half)


def _mix_sample_ab_kernel(p_ref, h0_ref, bufa_ref, bufb_ref,
                          caw_ref, cab_ref, wg_ref, bg_ref, lam_ref,
                          cbw_ref, cbb_ref, lng_ref, lnb_ref,
                          o_ref, h_ref, bufa_out_ref, bufb_out_ref, *, ts, w_a, w_b):
    xpa = [bufa_ref[j] for j in range(CONV_A - 1)] + [p_ref[t, :, w_a:2 * w_a] for t in range(ts)]
    xa = []
    for t in range(ts):
        acc = cab_ref[...] + caw_ref[0:1, :] * xpa[t]
        for j in range(1, CONV_A):
            acc = acc + caw_ref[j:j + 1, :] * xpa[t + j]
        xa.append(acc)
    nb = xa[0].shape[0]
    a, bt = _lru_gates(jnp.concatenate(xa, axis=0), None, wg_ref[...], bg_ref[...], lam_ref[...])
    h = h0_ref[...]
    for t in range(ts):
        h = a[t * nb:(t + 1) * nb, :] * h + bt[t * nb:(t + 1) * nb, :]
        o_ref[t, :, 0:w_a] = (jax.nn.gelu(p_ref[t, :, 0:w_a]) * h).astype(BF16)
    h_ref[...] = h
    for j in range(CONV_A - 1):
        bufa_out_ref[j] = xpa[ts + j]
    b0 = 2 * w_a
    hist = CONV_B - 1
    glu = [p_ref[t, :, b0:b0 + w_b] * _sigmoid(p_ref[t, :, b0 + w_b:b0 + 2 * w_b]) for t in range(ts)]
    xpb = lambda i: bufb_ref[i] if i < hist else glu[i - hist]
    for t in range(ts):
        acc = cbb_ref[...] + cbw_ref[0:1, :] * xpb(t)
        for j in range(1, CONV_B):
            acc = acc + cbw_ref[j:j + 1, :] * xpb(t + j)
        o_ref[t, :, w_a:w_a + w_b] = _silu(_layernorm(acc, lng_ref[...], lnb_ref[...])).astype(BF16)
    for i in range(hist):
        bufb_out_ref[i] = xpb(ts + i)


def _mix_sample_ab(p_t, h0, bufa_t, bufb_t, lw):
    ts, n_seq, width = p_t.shape
    w_a, w_b = lw["w_a"], lw["w_b"]
    nb = 32 if n_seq % 32 == 0 else n_seq
    full = lambda shape: pl.BlockSpec(shape, lambda i: (0,) * len(shape))
    kern = functools.partial(_mix_sample_ab_kernel, ts=ts, w_a=w_a, w_b=w_b)
    return pl.pallas_call(
        kern,
        grid=(n_seq // nb,),
        in_specs=[
            pl.BlockSpec((ts, nb, width), lambda i: (0, i, 0)),
            pl.BlockSpec((nb, w_a), lambda i: (i, 0)),
            pl.BlockSpec((CONV_A - 1, nb, w_a), lambda i: (0, i, 0)),
            pl.BlockSpec((CONV_B - 1, nb, w_b), lambda i: (0, i, 0)),
            full((CONV_A, w_a)), full((1, w_a)), full((w_a, 2 * w_a)), full((1, 2 * w_a)), full((1, w_a)),
            full((CONV_B, w_b)), full((1, w_b)), full((1, w_b)), full((1, w_b)),
        ],
        out_specs=[
            pl.BlockSpec((ts, nb, w_a + w_b), lambda i: (0, i, 0)),
            pl.BlockSpec((nb, w_a), lambda i: (i, 0)),
            pl.BlockSpec((CONV_A - 1, nb, w_a), lambda i: (0, i, 0)),
            pl.BlockSpec((CONV_B - 1, nb, w_b), lambda i: (0, i, 0)),
        ],
        out_shape=[
            jax.ShapeDtypeStruct((ts, n_seq, w_a + w_b), BF16),
            jax.ShapeDtypeStruct((n_seq, w_a), F32),
            jax.ShapeDtypeStruct((CONV_A - 1, n_seq, w_a), F32),
            jax.ShapeDtypeStruct((CONV_B - 1, n_seq, w_b), F32),
        ],
        compiler_params=_cparams("arbitrary"),
        name="mix_sample_ab",
    )(p_t, h0, bufa_t, bufb_t,
      lw["conv_a_w"], lw["conv_a_b"], lw["wg"], lw["bg"], lw["lam"],
      lw["conv_b_w"], lw["conv_b_b"], lw["ln_g"], lw["ln_b"])


def _mix_sample_c_kernel(q_ref, k_ref, v_ref, g_ref, oab_ref, s_ref, cos_ref, sin_ref,
                         dmat_ref, xi_ref, zeta_ref, gcol_ref, rng_ref, tile_ref, fold_ref,
                         mix_ref, sout_ref, qs_ref, kzs_ref, vs_ref, oi_ref,
                         *, rows, ts, w_c, w_ab):
    n_pair = w_c // LANES
    hd = LANES // 2
    lane = lax.broadcasted_iota(jnp.int32, (rows, LANES), 1)
    head0 = lane < hd
    first_half = (lane % hd) < (hd // 2)
    cos = cos_ref[...]
    sin = sin_ref[...]
    for p in range(n_pair):
        cs = slice(p * LANES, (p + 1) * LANES)
        q = _rope_pair(q_ref[:, cs], cos, sin, first_half)
        k = _rope_pair(k_ref[:, cs], cos, sin, first_half) * (1.0 / hd) ** 0.5
        v = v_ref[:, cs]
        qb, kb, vb = q.astype(BF16), k.astype(BF16), v.astype(BF16)
        oi_ref[:, cs] = _retention_intra_pair(qb, kb, vb, dmat_ref[2 * p], dmat_ref[2 * p + 1], head0)
        qs_ref[:, cs] = q
        kzs_ref[:, cs] = k * zeta_ref[:, cs]
        vs_ref[:, cs] = v
    grp = SUBLANES
    per_grp = grp // ts
    stack = NH_C * grp
    srow = lax.broadcasted_iota(jnp.int32, (stack, w_c), 0)
    slane = lax.broadcasted_iota(jnp.int32, (stack, w_c), 1)
    own_head = (srow // grp) == (slane // hd)
    tile_m = tile_ref[...]
    fold_m = fold_ref[...]
    for gi in range(rows // grp):
        rs = slice(gi * grp, (gi + 1) * grp)
        q8 = jnp.concatenate([qs_ref[rs, :]] * NH_C, axis=0)
        kz8 = jnp.concatenate([kzs_ref[rs, :]] * NH_C, axis=0)
        v8 = jnp.concatenate([vs_ref[rs, :]] * NH_C, axis=0)
        vr = jnp.dot(jnp.where(own_head, v8, 0.0).astype(BF16), fold_m, preferred_element_type=F32).astype(BF16)
        o_stack = jnp.zeros((stack, hd), F32)
        for s2 in range(per_grp):
            sq = gi * per_grp + s2
            own_seq = ((srow % grp) // ts) == s2
            sel = jnp.logical_and(own_head, own_seq)
            s_old = s_ref[sq].reshape(w_c, hd)
            o_stack = o_stack + jnp.dot(jnp.where(sel, q8, 0.0).astype(BF16), s_old.astype(BF16),
                                        preferred_element_type=F32)
            upd = lax.dot_general(jnp.where(sel, kz8, 0.0).astype(BF16), vr, (((0,), (0,)), ((), ())),
                                  preferred_element_type=F32)
            sout_ref[sq] = (gcol_ref[...] * s_old + upd).reshape(NH_C, hd, hd)
        spread = jnp.dot(o_stack, tile_m, preferred_element_type=F32, precision=HIGHEST)
        spread = jnp.where(own_head, spread, 0.0).reshape(NH_C, grp, w_c)
        oi_ref[rs, :] = oi_ref[rs, :] + jnp.sum(spread, axis=0) * xi_ref[rs, :]
    mix_ref[:, 0:w_ab] = oab_ref[...]
    for p in range(n_pair):
        cs = slice(p * LANES, (p + 1) * LANES)
        on = _head_norm_pair(oi_ref[:, cs], head0, 1.0 / hd) * rng_ref[:, cs]
        mix_ref[:, w_ab + p * LANES:w_ab + (p + 1) * LANES] = (_silu(g_ref[:, cs]) * on).astype(BF16)


def _mix_sample_c(proj, row0, n_seq, ts, oab, state, lw):
    w_a, w_b, w_c = lw["w_a"], lw["w_b"], lw["w_c"]
    hd = LANES // 2
    n_rows = n_seq * ts
    rows = 64 if n_rows % 64 == 0 else n_rows
    assert SUBLANES % ts == 0 and rows % SUBLANES == 0 and row0 % rows == 0
    seq_per = rows // ts
    c0 = 2 * w_a + 2 * w_b
    assert c0 % w_c == 0
    cb = c0 // w_c
    rb = row0 // rows
    log_g = jnp.log1p(-jnp.exp2(-5.0 - jnp.arange(NH_C, dtype=F32)))
    r = jnp.arange(rows)
    step = (r % ts).astype(F32)
    diff = step[:, None] - step[None, :]
    live = jnp.logical_and((r[:, None] // ts) == (r[None, :] // ts), diff >= 0)
    dmat = jnp.where(live[None], jnp.exp(jnp.where(live, diff, 0.0)[None] * log_g[:, None, None]), 0.0)
    xi = jnp.repeat(jnp.exp((step + 1.0)[None, :] * log_g[:, None]).T, hd, axis=1)
    zeta = jnp.repeat(jnp.exp((ts - 1.0 - step)[None, :] * log_g[:, None]).T, hd, axis=1)
    gcol = jnp.broadcast_to(jnp.repeat(jnp.exp(ts * log_g), hd)[:, None], (w_c, hd))
    pos = PAST_LEN + (r % ts).astype(jnp.int32)
    cos, sin = _rope_tables(pos)
    tile_m = jnp.tile(jnp.eye(hd, dtype=F32), (1, NH_C))
    fold_m = jnp.tile(jnp.eye(hd, dtype=BF16), (NH_C, 1))
    full = lambda shape: pl.BlockSpec(shape, lambda i: (0,) * len(shape))
    col = lambda j: pl.BlockSpec((rows, w_c), lambda i: (rb + i, cb + j))
    kern = functools.partial(_mix_sample_c_kernel, rows=rows, ts=ts, w_c=w_c, w_ab=w_a + w_b)
    return pl.pallas_call(
        kern,
        grid=(n_rows // rows,),
        in_specs=[
            col(0), col(1), col(2), col(3),
            pl.BlockSpec((rows, w_a + w_b), lambda i: (i, 0)),
            pl.BlockSpec((seq_per, NH_C, hd, hd), lambda i: (i, 0, 0, 0)),
            full((rows, LANES)), full((rows, LANES)),
            full((NH_C, rows, rows)), full((rows, w_c)), full((rows, w_c)), full((w_c, hd)), full((1, w_c)),
            full((hd, w_c)), full((w_c, hd)),
        ],
        out_specs=[
            pl.BlockSpec((rows, w_a + w_b + w_c), lambda i: (i, 0)),
            pl.BlockSpec((seq_per, NH_C, hd, hd), lambda i: (i, 0, 0, 0)),
        ],
        out_shape=[
            jax.ShapeDtypeStruct((n_rows, w_a + w_b + w_c), BF16),
            jax.ShapeDtypeStruct(state.shape, F32),
        ],
        scratch_shapes=[pltpu.VMEM((rows, w_c), F32)] * 4,
        compiler_params=_cparams("arbitrary"),
        name="mix_sample_c",
    )(proj, proj, proj, proj, oab, state, cos, sin, dmat, xi, zeta, gcol, lw["ret_g"], tile_m, fold_m)


def _route(logits):
    shape = logits.shape
    lane = lax.broadcasted_iota(jnp.int32, shape, 1)
    neg = jnp.full(shape, -jnp.inf, F32)
    big = jnp.full(shape, LANES, jnp.int32)
    is_g = lane < N_GROUPS
    gl = jnp.where(is_g, logits, neg)
    gmax = jnp.max(gl, axis=-1, keepdims=True)
    p_top = 1.0 / jnp.sum(jnp.exp(gl - gmax), axis=-1, keepdims=True)
    g_sel = jnp.min(jnp.where(gl == gmax, lane, big), axis=-1, keepdims=True)
    lo = N_GROUPS + g_sel * EXPERTS_PER_GROUP
    in_grp = jnp.logical_and(lane >= lo, lane < lo + EXPERTS_PER_GROUP)
    el = jnp.where(in_grp, logits, neg)
    v1 = jnp.max(el, axis=-1, keepdims=True)
    i1 = jnp.min(jnp.where(el == v1, lane, big), axis=-1, keepdims=True)
    el2 = jnp.where(lane == i1, neg, el)
    v2 = jnp.max(el2, axis=-1, keepdims=True)
    i2 = jnp.min(jnp.where(el2 == v2, lane, big), axis=-1, keepdims=True)
    e2 = jnp.exp(v2 - v1)
    den = 1.0 + e2
    g1 = p_top * (1.0 / den)
    g2 = p_top * (e2 / den)
    return i1 - N_GROUPS, i2 - N_GROUPS, g1, g2


def _out_proj_kernel(x_ref, mixp_ref, mixs_ref, w_ref, g_ref, wr_ref, br_ref,
                     xmid_ref, xn_ref, gates_ref, *, n_prompt_tiles):
    i = pl.program_id(0)

    def finish(mix):
        xm = x_ref[...] + jnp.dot(mix, w_ref[...], preferred_element_type=F32)
        xmid_ref[...] = xm
        xn = _rms(xm, g_ref[...])
        xn_ref[...] = xn.astype(BF16)
        logits = jnp.dot(xn, wr_ref[...], preferred_element_type=F32, precision=HIGHEST) + br_ref[...]
        e1, e2, g1, g2 = _route(logits)
        lane = lax.broadcasted_iota(jnp.int32, logits.shape, 1)
        gates_ref[...] = jnp.where(lane == e1, g1, 0.0) + jnp.where(lane == e2, g2, 0.0)

    @pl.when(i < n_prompt_tiles)
    def _():
        finish(mixp_ref[...])

    @pl.when(i >= n_prompt_tiles)
    def _():
        finish(mixs_ref[...])


def _out_proj(x, mix_p, mix_s, w_out, g_ffn, w_route, b_route):
    n, d = x.shape
    n_p, d_mix = mix_p.shape
    n_s = mix_s.shape[0]
    tm = math.gcd(_row_tile(n_p), _row_tile(n_s))
    tp = n_p // tm
    ts_last = n_s // tm - 1
    kern = functools.partial(_out_proj_kernel, n_prompt_tiles=tp)
    return pl.pallas_call(
        kern,
        grid=(n // tm,),
        in_specs=[
            pl.BlockSpec((tm, d), lambda i: (i, 0)),
            pl.BlockSpec((tm, d_mix), lambda i: (jnp.minimum(i, tp - 1), 0)),
            pl.BlockSpec((tm, d_mix), lambda i: (jnp.clip(i - tp, 0, ts_last), 0)),
            pl.BlockSpec((d_mix, d), lambda i: (0, 0)),
            pl.BlockSpec((1, d), lambda i: (0, 0)),
            pl.BlockSpec((d, LANES), lambda i: (0, 0)),
            pl.BlockSpec((1, LANES), lambda i: (0, 0)),
        ],
        out_specs=[
            pl.BlockSpec((tm, d), lambda i: (i, 0)),
            pl.BlockSpec((tm, d), lambda i: (i, 0)),
            pl.BlockSpec((tm, LANES), lambda i: (i, 0)),
        ],
        out_shape=[
            jax.ShapeDtypeStruct((n, d), F32),
            jax.ShapeDtypeStruct((n, d), BF16),
            jax.ShapeDtypeStruct((n, LANES), F32),
        ],
        compiler_params=_cparams("arbitrary"),
        name="out_proj_route",
    )(x, mix_p, mix_s, w_out, g_ffn, w_route, b_route)


def _moe_dense_kernel(xmid_ref, xn_ref, gates_ref, w1_ref, w3_ref, w2_ref, gfin_ref, o_ref, acc_ref,
                      *, final_norm):
    e = pl.program_id(1)

    @pl.when(e == 0)
    def _():
        acc_ref[...] = jnp.zeros_like(acc_ref)

    xn = xn_ref[...]
    h1 = jnp.dot(xn, w1_ref[0], preferred_element_type=F32)
    h3 = jnp.dot(xn, w3_ref[0], preferred_element_type=F32)
    h = (_silu(h1) * h3).astype(BF16)
    y = jnp.dot(h, w2_ref[0], preferred_element_type=F32)
    gates = gates_ref[...]
    lane = lax.broadcasted_iota(jnp.int32, gates.shape, 1)
    ge = jnp.sum(jnp.where(lane == e, gates, 0.0), axis=-1, keepdims=True)
    acc_ref[...] += ge * y

    @pl.when(e == pl.num_programs(1) - 1)
    def _():
        out = xmid_ref[...] + acc_ref[...]
        o_ref[...] = _rms(out, gfin_ref[...]) if final_norm else out


def _moe_dense(xmid, xn, gates, w1, w3, w2, g_final, final_norm):
    n, d = xmid.shape
    n_e, _, d_e = w1.shape
    tm = _row_tile(n)
    kern = functools.partial(_moe_dense_kernel, final_norm=final_norm)
    return pl.pallas_call(
        kern,
        grid=(n // tm, n_e),
        in_specs=[
            pl.BlockSpec((tm, d), lambda i, e: (i, 0)),
            pl.BlockSpec((tm, d), lambda i, e: (i, 0)),
            pl.BlockSpec((tm, LANES), lambda i, e: (i, 0)),
            pl.BlockSpec((1, d, d_e), lambda i, e: (e, 0, 0)),
            pl.BlockSpec((1, d, d_e), lambda i, e: (e, 0, 0)),
            pl.BlockSpec((1, d_e, d), lambda i, e: (e, 0, 0)),
            pl.BlockSpec((1, d), lambda i, e: (0, 0)),
        ],
        out_specs=pl.BlockSpec((tm, d), lambda i, e: (i, 0)),
        out_shape=jax.ShapeDtypeStruct((n, d), F32),
        scratch_shapes=[pltpu.VMEM((tm, d), F32)],
        compiler_params=_cparams("arbitrary", "arbitrary"),
        name="moe_dense",
    )(xmid, xn, gates, w1, w3, w2, g_final)


def kernel(x_prompt, x_sample, state_lru_h, state_lru_conv, state_conv, state_ret, norm_mix_g, norm_ffn_g, final_norm_g, w_in, lru_conv_w, lru_conv_b, lru_wa, lru_ba, lru_wx, lru_bx, lru_lambda, cm_dw_w, cm_dw_b, cm_ln_g, cm_ln_b, ret_norm_g, w_out, moe_w_group, moe_b_group, moe_w_expert, moe_b_expert, moe_w1, moe_w3, moe_w2):
    bp, tp, d = x_prompt.shape
    bs, ts, _ = x_sample.shape
    depth = w_in.shape[0]
    w_a = lru_lambda.shape[1]
    w_b = cm_dw_b.shape[1]
    w_c = ret_norm_g.shape[1]
    n_p, n_s = bp * tp, bs * ts
    row = lambda v: v.reshape(1, -1)

    x = jnp.concatenate([x_prompt.reshape(n_p, d), x_sample.reshape(n_s, d)], axis=0)
    outs = {k: [] for k in ("p_h", "p_ca", "p_cb", "p_s", "s_h", "s_ca", "s_cb", "s_s")}
    for l in range(depth):
        lw = dict(
            w_a=w_a, w_b=w_b, w_c=w_c,
            conv_a_w=lru_conv_w[l], conv_a_b=row(lru_conv_b[l]),
            wg=_blockdiag_gate_weights(lru_wa[l], lru_wx[l]).astype(BF16),
            bg=row(jnp.concatenate([lru_ba[l], lru_bx[l]])), lam=row(lru_lambda[l]),
            conv_b_w=cm_dw_w[l], conv_b_b=row(cm_dw_b[l]), ln_g=row(cm_ln_g[l]), ln_b=row(cm_ln_b[l]),
            ret_g=row(ret_norm_g[l]),
        )
        proj = _in_proj(x, row(norm_mix_g[l]), w_in[l].astype(BF16))
        mix_p, h_p, ca_p, cb_p, s_p = _mix_prompt(proj, bp, tp, lw)
        outs["p_h"].append(h_p[:, 0, :])
        outs["p_ca"].append(ca_p[:, SUBLANES - (CONV_A - 1):, :])
        outs["p_cb"].append(cb_p[:, 4 * SUBLANES - (CONV_B - 1):, :])
        outs["p_s"].append(_unpair_state(s_p))
        p_t = proj[n_p:, :2 * w_a + 2 * w_b].reshape(bs, ts, -1).transpose(1, 0, 2)
        oab_t, h_s, ca_t, cb_t = _mix_sample_ab(
            p_t, state_lru_h[l], state_lru_conv[l].transpose(1, 0, 2), state_conv[l].transpose(1, 0, 2), lw)
        oab = oab_t.transpose(1, 0, 2).reshape(n_s, w_a + w_b)
        mix_s, s_s = _mix_sample_c(proj, n_p, bs, ts, oab, state_ret[l], lw)
        outs["s_h"].append(h_s)
        outs["s_ca"].append(ca_t.transpose(1, 0, 2))
        outs["s_cb"].append(cb_t.transpose(1, 0, 2))
        outs["s_s"].append(s_s)
        w_route = jnp.zeros((d, LANES), F32)
        w_route = w_route.at[:, :N_GROUPS].set(moe_w_group[l]).at[:, N_GROUPS:N_GROUPS + N_EXPERTS].set(moe_w_expert[l])
        b_route = jnp.zeros((1, LANES), F32)
        b_route = b_route.at[0, :N_GROUPS].set(moe_b_group[l]).at[0, N_GROUPS:N_GROUPS + N_EXPERTS].set(moe_b_expert[l])
        xmid, xn, gates = _out_proj(x, mix_p, mix_s, w_out[l].astype(BF16), row(norm_ffn_g[l]), w_route, b_route)
        x = _moe_dense(xmid, xn, gates, moe_w1[l].astype(BF16), moe_w3[l].astype(BF16), moe_w2[l].astype(BF16),
                       row(final_norm_g), final_norm=(l == depth - 1))
    y_prompt = x[:n_p].reshape(bp, tp, d)
    y_sample = x[n_p:].reshape(bs, ts, d)
    st = lambda k: jnp.stack(outs[k])
    return (y_prompt, y_sample, st("p_h"), st("p_ca"), st("p_cb"), st("p_s"),
            st("s_h"), st("s_ca"), st("s_cb"), st("s_s"))
```

```python
import functools
import math

import jax
import jax.numpy as jnp
import numpy as np
from jax import lax
from jax.experimental import pallas as pl
from jax.experimental.pallas import tpu as pltpu

PAST_LEN = 16384
NH_A = 4
LRU_C = 8.0
CONV_A = 4
CONV_B = 31
NH_C = 8
RET_CHUNK = 128
ROPE_BASE = 10000.0
N_GROUPS = 4
EXPERTS_PER_GROUP = 8
N_EXPERTS = N_GROUPS * EXPERTS_PER_GROUP
EPS = 1e-6

LANES = 128
SUBLANES = 8
VMEM_LIMIT_BYTES = 56 * 1024 * 1024

F32 = jnp.float32
BF16 = jnp.bfloat16
HIGHEST = lax.Precision.HIGHEST


def _cparams(*sem):
    return pltpu.CompilerParams(dimension_semantics=sem, vmem_limit_bytes=VMEM_LIMIT_BYTES)


def _row_tile(n):
    for t in (512, 256, 128, 64, 32, 16, 8):
        if n % t == 0:
            return t
    raise ValueError(f"token count {n} must be a multiple of 8")


def _rms(x, g):
    ms = jnp.mean(x * x, axis=-1, keepdims=True)
    return x * lax.rsqrt(ms + EPS) * g


def _sigmoid(x):
    return jax.nn.sigmoid(x)


def _silu(x):
    return x * jax.nn.sigmoid(x)


def _in_proj_kernel(x_ref, g_ref, w_ref, o_ref):
    u = _rms(x_ref[...], g_ref[...]).astype(BF16)
    o_ref[...] = jnp.dot(u, w_ref[...], preferred_element_type=F32)


def _in_proj(x, g, w):
    n, d = x.shape
    d_in = w.shape[1]
    tm = _row_tile(n)
    return pl.pallas_call(
        _in_proj_kernel,
        grid=(n // tm,),
        in_specs=[
            pl.BlockSpec((tm, d), lambda i: (i, 0)),
            pl.BlockSpec((1, d), lambda i: (0, 0)),
            pl.BlockSpec((d, d_in), lambda i: (0, 0)),
        ],
        out_specs=pl.BlockSpec((tm, d_in), lambda i: (i, 0)),
        out_shape=jax.ShapeDtypeStruct((n, d_in), F32),
        compiler_params=_cparams("arbitrary"),
        name="in_proj",
    )(x, g, w)


def _rope_pair(x, cos, sin_signed, first_half):
    partner = jnp.where(first_half, pltpu.roll(x, 96, 1), pltpu.roll(x, 32, 1))
    return x * cos + partner * sin_signed


def _head_norm_pair(o, head0, inv_d):
    zero = jnp.zeros_like(o)
    s0 = jnp.sum(jnp.where(head0, o, zero), axis=-1, keepdims=True)
    s1 = jnp.sum(jnp.where(head0, zero, o), axis=-1, keepdims=True)
    oc = o - jnp.where(head0, s0, s1) * inv_d
    q = oc * oc
    v0 = jnp.sum(jnp.where(head0, q, zero), axis=-1, keepdims=True)
    v1 = jnp.sum(jnp.where(head0, zero, q), axis=-1, keepdims=True)
    return oc * lax.rsqrt(jnp.where(head0, v0, v1) * inv_d + EPS)


def _retention_intra_pair(qb, kb, vb, dmat0, dmat1, head0):
    outs = []
    for j, dm in ((0, dmat0), (1, dmat1)):
        keep = head0 if j == 0 else jnp.logical_not(head0)
        qm = jnp.where(keep, qb, jnp.zeros_like(qb))
        s = lax.dot_general(qm, kb, (((1,), (1,)), ((), ())), preferred_element_type=F32)
        inner = (s * dm).astype(BF16)
        outs.append(jnp.dot(inner, vb, preferred_element_type=F32))
    return jnp.where(head0, outs[0], outs[1])


def _lru_gates(xa, a_gate_unused, wg, bg, lam):
    w_a = xa.shape[-1]
    gates = jnp.dot(xa.astype(BF16), wg, preferred_element_type=F32) + bg
    r = _sigmoid(gates[:, :w_a])
    i = _sigmoid(gates[:, w_a:])
    z = -lam
    softplus = jnp.maximum(z, 0.0) + jnp.log1p(jnp.exp(-jnp.abs(z)))
    log_a = -LRU_C * r * softplus
    a = jnp.exp(log_a)
    bt = jnp.sqrt(-jnp.tanh(log_a) * (a * a + 1.0)) * (i * xa)
    return a, bt


def _layernorm(x, g, b):
    mu = jnp.mean(x, axis=-1, keepdims=True)
    xc = x - mu
    var = jnp.mean(xc * xc, axis=-1, keepdims=True)
    return xc * lax.rsqrt(var + EPS) * g + b


CONV_B_ROWS = 64


def _mix_prompt_kernel(proj_ref, cos_ref, sin_ref, h0_ref, bufa0_ref, bufb0_ref, s0_ref,
                       caw_ref, cab_ref, wg_ref, bg_ref, lam_ref,
                       cbw_ref, cbb_ref, lng_ref, lnb_ref, rng_ref,
                       dmat_ref, xi_ref, zeta_ref, gmat_ref,
                       mix_ref, hout_ref, bufa_out_ref, bufb_out_ref, sout_ref,
                       xpa_ref, xpb_ref, a_ref, b_ref, hseq_ref, hcar_ref, s_ref,
                       *, tc, w_a, w_b, w_c, chunk):
    t = pl.program_id(1)
    n_t = pl.num_programs(1)
    pa = SUBLANES
    pb = 4 * SUBLANES

    @pl.when(t == 0)
    def _():
        xpa_ref[0:pa, :] = bufa0_ref[0]
        xpb_ref[0:pb, :] = bufb0_ref[0]
        hcar_ref[...] = h0_ref[0]
        s_ref[...] = s0_ref[0]

    a_gate = proj_ref[:, 0:w_a]
    xpa_ref[pa:pa + tc, :] = proj_ref[:, w_a:2 * w_a]
    xa = cab_ref[...] + caw_ref[0:1, :] * xpa_ref[pa - 3:pa - 3 + tc, :]
    for j in range(1, CONV_A):
        xa = xa + caw_ref[j:j + 1, :] * xpa_ref[pa - 3 + j:pa - 3 + j + tc, :]
    a, bt = _lru_gates(xa, None, wg_ref[...], bg_ref[...], lam_ref[...])
    a_ref[...] = a
    b_ref[...] = bt
    row8 = lax.broadcasted_iota(jnp.int32, (SUBLANES, w_a), 0)

    def scan_block(i, hprev):
        r0 = pl.multiple_of(i * SUBLANES, SUBLANES)
        ab = a_ref[pl.ds(r0, SUBLANES), :]
        bb = b_ref[pl.ds(r0, SUBLANES), :]
        for k in (1, 2, 4):
            a_sh = jnp.where(row8 >= k, pltpu.roll(ab, k, 0), 1.0)
            b_sh = jnp.where(row8 >= k, pltpu.roll(bb, k, 0), 0.0)
            bb = ab * b_sh + bb
            ab = ab * a_sh
        h = ab * hprev + bb
        hseq_ref[pl.ds(r0, SUBLANES), :] = h
        return jnp.broadcast_to(h[SUBLANES - 1:SUBLANES, :], (SUBLANES, w_a))

    hcar_ref[...] = lax.fori_loop(0, tc // SUBLANES, scan_block, hcar_ref[...])
    out_a = jax.nn.gelu(a_gate) * hseq_ref[...]
    mix_ref[:, 0:w_a] = out_a.astype(BF16)
    xpa_ref[0:pa, :] = xpa_ref[tc:tc + pa, :]

    b0 = 2 * w_a
    xpb_ref[pb:pb + tc, :] = proj_ref[:, b0:b0 + w_b] * _sigmoid(proj_ref[:, b0 + w_b:b0 + 2 * w_b])
    first = pb - (CONV_B - 1)
    for blk in range(tc // CONV_B_ROWS):
        r0 = blk * CONV_B_ROWS
        acc = cbb_ref[...] + cbw_ref[0:1, :] * xpb_ref[first + r0:first + r0 + CONV_B_ROWS, :]
        for j in range(1, CONV_B):
            acc = acc + cbw_ref[j:j + 1, :] * xpb_ref[first + r0 + j:first + r0 + j + CONV_B_ROWS, :]
        out_b = _silu(_layernorm(acc, lng_ref[...], lnb_ref[...]))
        mix_ref[r0:r0 + CONV_B_ROWS, w_a:w_a + w_b] = out_b.astype(BF16)
    xpb_ref[0:pb, :] = xpb_ref[tc:tc + pb, :]

    c0 = 2 * w_a + 2 * w_b
    n_pair = w_c // LANES
    lane = lax.broadcasted_iota(jnp.int32, (chunk, LANES), 1)
    head0 = lane < (LANES // 2)
    first_half = (lane % (LANES // 2)) < (LANES // 4)
    rl = lax.broadcasted_iota(jnp.int32, (LANES, LANES), 0) < (LANES // 2)
    cl = lax.broadcasted_iota(jnp.int32, (LANES, LANES), 1) < (LANES // 2)
    same_head = rl == cl
    inv_d = 2.0 / LANES
    for c in range(tc // chunk):
        rows = slice(c * chunk, (c + 1) * chunk)
        cos = cos_ref[rows, :]
        sin = sin_ref[rows, :]
        for p in range(n_pair):
            cq = c0 + p * LANES
            q = _rope_pair(proj_ref[rows, cq:cq + LANES], cos, sin, first_half)
            k = _rope_pair(proj_ref[rows, cq + w_c:cq + w_c + LANES], cos, sin, first_half) * (2.0 / LANES) ** 0.5
            v = proj_ref[rows, cq + 2 * w_c:cq + 2 * w_c + LANES]
            g = proj_ref[rows, cq + 3 * w_c:cq + 3 * w_c + LANES]
            qb, kb, vb = q.astype(BF16), k.astype(BF16), v.astype(BF16)
            o = _retention_intra_pair(qb, kb, vb, dmat_ref[2 * p], dmat_ref[2 * p + 1], head0)
            s_old = s_ref[p]
            o = o + jnp.dot(qb, s_old.astype(BF16), preferred_element_type=F32) * xi_ref[p]
            kz = (k * zeta_ref[p]).astype(BF16)
            upd = lax.dot_general(kz, vb, (((0,), (0,)), ((), ())), preferred_element_type=F32)
            s_ref[p] = gmat_ref[p] * s_old + jnp.where(same_head, upd, 0.0)
            on = _head_norm_pair(o, head0, inv_d) * rng_ref[:, p * LANES:(p + 1) * LANES]
            mix_ref[rows, w_a + w_b + p * LANES:w_a + w_b + (p + 1) * LANES] = (_silu(g) * on).astype(BF16)

    @pl.when(t == n_t - 1)
    def _():
        hout_ref[0] = hcar_ref[...]
        bufa_out_ref[0] = xpa_ref[0:pa, :]
        bufb_out_ref[0] = xpb_ref[0:pb, :]
        sout_ref[0] = s_ref[...]


def _retention_tables(chunk):
    log_g = jnp.log1p(-jnp.exp2(-5.0 - jnp.arange(NH_C, dtype=F32)))
    idx = jnp.arange(chunk, dtype=F32)
    diff = idx[:, None] - idx[None, :]
    causal = diff >= 0
    dmat = jnp.where(causal[None], jnp.exp(jnp.where(causal, diff, 0.0)[None] * log_g[:, None, None]), 0.0)
    xi = jnp.exp((idx + 1.0)[None, :] * log_g[:, None])
    zeta = jnp.exp((chunk - 1.0 - idx)[None, :] * log_g[:, None])
    g_chunk = jnp.exp(chunk * log_g)
    return dmat, xi, zeta, g_chunk


def _pair_lanes(per_head):
    h, rows = per_head.shape
    half = LANES // 2
    x = jnp.broadcast_to(per_head.reshape(h // 2, 2, rows, 1), (h // 2, 2, rows, half))
    return x.transpose(0, 2, 1, 3).reshape(h // 2, rows, LANES)


def _rope_tables(pos):
    half = LANES // 4
    inv = ROPE_BASE ** (-jnp.arange(half, dtype=F32) / half)
    ang = pos.astype(F32)[:, None] * inv[None, :]
    cos, sin = jnp.cos(ang), jnp.sin(ang)
    return jnp.concatenate([cos] * 4, axis=1), jnp.concatenate([-sin, sin, -sin, sin], axis=1)


def _blockdiag_gate_weights(wa, wx):
    nh, hd, _ = wa.shape
    eye = jnp.eye(nh, dtype=wa.dtype)
    da = (eye[:, None, :, None] * wa[:, :, None, :]).reshape(nh * hd, nh * hd)
    dx = (eye[:, None, :, None] * wx[:, :, None, :]).reshape(nh * hd, nh * hd)
    return jnp.concatenate([da, dx], axis=1)


def _mix_prompt(proj, n_seq, seq_len, lw):
    w_a, w_b, w_c = lw["w_a"], lw["w_b"], lw["w_c"]
    d_in = proj.shape[1]
    d_mix = w_a + w_b + w_c
    chunk = math.gcd(seq_len, RET_CHUNK)
    tc = 256 if seq_len % 256 == 0 else chunk
    n_t = seq_len // tc
    n_pair = w_c // LANES
    dmat, xi, zeta, g_chunk = _retention_tables(chunk)
    xi_p = _pair_lanes(xi)
    zeta_p = _pair_lanes(zeta)
    half = LANES // 2
    gm = jnp.zeros((n_pair, 2, half, 2, half), F32)
    gm = gm.at[:, 0, :, 0, :].set(g_chunk[0::2, None, None]).at[:, 1, :, 1, :].set(g_chunk[1::2, None, None])
    gmat = gm.reshape(n_pair, LANES, LANES)
    cos, sin = _rope_tables(jnp.arange(seq_len, dtype=jnp.int32))
    zeros = lambda *s: jnp.zeros(s, F32)
    full = lambda shape: pl.BlockSpec(shape, lambda b, t: (0,) * len(shape))
    per_seq = lambda shape: pl.BlockSpec((1,) + shape, lambda b, t: (b,) + (0,) * len(shape))
    kern = functools.partial(_mix_prompt_kernel, tc=tc, w_a=w_a, w_b=w_b, w_c=w_c, chunk=chunk)
    return pl.pallas_call(
        kern,
        grid=(n_seq, n_t),
        in_specs=[
            pl.BlockSpec((tc, d_in), lambda b, t: (b * n_t + t, 0)),
            pl.BlockSpec((tc, LANES), lambda b, t: (t, 0)),
            pl.BlockSpec((tc, LANES), lambda b, t: (t, 0)),
            per_seq((SUBLANES, w_a)), per_seq((SUBLANES, w_a)), per_seq((4 * SUBLANES, w_b)),
            per_seq((n_pair, LANES, LANES)),
            full((CONV_A, w_a)), full((1, w_a)), full((w_a, 2 * w_a)), full((1, 2 * w_a)), full((1, w_a)),
            full((CONV_B, w_b)), full((1, w_b)), full((1, w_b)), full((1, w_b)), full((1, w_c)),
            full((NH_C, chunk, chunk)), full((n_pair, chunk, LANES)), full((n_pair, chunk, LANES)),
            full((n_pair, LANES, LANES)),
        ],
        out_specs=[
            pl.BlockSpec((tc, d_mix), lambda b, t: (b * n_t + t, 0)),
            per_seq((SUBLANES, w_a)), per_seq((SUBLANES, w_a)), per_seq((4 * SUBLANES, w_b)),
            per_seq((n_pair, LANES, LANES)),
        ],
        out_shape=[
            jax.ShapeDtypeStruct((n_seq * seq_len, d_mix), BF16),
            jax.ShapeDtypeStruct((n_seq, SUBLANES, w_a), F32),
            jax.ShapeDtypeStruct((n_seq, SUBLANES, w_a), F32),
            jax.ShapeDtypeStruct((n_seq, 4 * SUBLANES, w_b), F32),
            jax.ShapeDtypeStruct((n_seq, n_pair, LANES, LANES), F32),
        ],
        scratch_shapes=[
            pltpu.VMEM((SUBLANES + tc, w_a), F32),
            pltpu.VMEM((4 * SUBLANES + tc, w_b), F32),
            pltpu.VMEM((tc, w_a), F32),
            pltpu.VMEM((tc, w_a), F32),
            pltpu.VMEM((tc, w_a), F32),
            pltpu.VMEM((SUBLANES, w_a), F32),
            pltpu.VMEM((n_pair, LANES, LANES), F32),
        ],
        compiler_params=_cparams("arbitrary", "arbitrary"),
        name="mix_prompt",
    )(proj, cos, sin,
      zeros(n_seq, SUBLANES, w_a), zeros(n_seq, SUBLANES, w_a), zeros(n_seq, 4 * SUBLANES, w_b),
      zeros(n_seq, n_pair, LANES, LANES),
      lw["conv_a_w"], lw["conv_a_b"], lw["wg"], lw["bg"], lw["lam"],
      lw["conv_b_w"], lw["conv_b_b"], lw["ln_g"], lw["ln_b"], lw["ret_g"],
      dmat, xi_p, zeta_p, gmat)


def _unpair_state(s_pairs):
    b, n_pair = s_pairs.shape[:2]
    half = LANES // 2
    s = s_pairs.reshape(b, n_pair, 2, half, 2, half)
    return jnp.stack([s[:, :, 0, :, 0, :], s[:, :, 1, :, 1, :]], axis=2).reshape(b, 2 * n_pair, half, half)


def _mix_sample_ab_kernel(p_ref, h0_ref, bufa_ref, bufb_ref,
                          caw_ref, cab_ref, wg_ref, bg_ref, lam_ref,
                          cbw_ref, cbb_ref, lng_ref, lnb_ref,
                          o_ref, h_ref, bufa_out_ref, bufb_out_ref, *, ts, w_a, w_b):
    xpa = [bufa_ref[j] for j in range(CONV_A - 1)] + [p_ref[t, :, w_a:2 * w_a] for t in range(ts)]
    xa = []
    for t in range(ts):
        acc = cab_ref[...] + caw_ref[0:1, :] * xpa[t]
        for j in range(1, CONV_A):
            acc = acc + caw_ref[j:j + 1, :] * xpa[t + j]
        xa.append(acc)
    nb = xa[0].shape[0]
    a, bt = _lru_gates(jnp.concatenate(xa, axis=0), None, wg_ref[...], bg_ref[...], lam_ref[...])
    h = h0_ref[...]
    for t in range(ts):
        h = a[t * nb:(t + 1) * nb, :] * h + bt[t * nb:(t + 1) * nb, :]
        o_ref[t, :, 0:w_a] = (jax.nn.gelu(p_ref[t, :, 0:w_a]) * h).astype(BF16)
    h_ref[...] = h
    for j in range(CONV_A - 1):
        bufa_out_ref[j] = xpa[ts + j]
    b0 = 2 * w_a
    hist = CONV_B - 1
    glu = [p_ref[t, :, b0:b0 + w_b] * _sigmoid(p_ref[t, :, b0 + w_b:b0 + 2 * w_b]) for t in range(ts)]
    xpb = lambda i: bufb_ref[i] if i < hist else glu[i - hist]
    for t in range(ts):
        acc = cbb_ref[...] + cbw_ref[0:1, :] * xpb(t)
        for j in range(1, CONV_B):
            acc = acc + cbw_ref[j:j + 1, :] * xpb(t + j)
        o_ref[t, :, w_a:w_a + w_b] = _silu(_layernorm(acc, lng_ref[...], lnb_ref[...])).astype(BF16)
    for i in range(hist):
        bufb_out_ref[i] = xpb(ts + i)


def _mix_sample_ab(p_t, h0, bufa_t, bufb_t, lw):
    ts, n_seq, width = p_t.shape
    w_a, w_b = lw["w_a"], lw["w_b"]
    nb = 32 if n_seq % 32 == 0 else n_seq
    full = lambda shape: pl.BlockSpec(shape, lambda i: (0,) * len(shape))
    kern = functools.partial(_mix_sample_ab_kernel, ts=ts, w_a=w_a, w_b=w_b)
    return pl.pallas_call(
        kern,
        grid=(n_seq // nb,),
        in_specs=[
            pl.BlockSpec((ts, nb, width), lambda i: (0, i, 0)),
            pl.BlockSpec((nb, w_a), lambda i: (i, 0)),
            pl.BlockSpec((CONV_A - 1, nb, w_a), lambda i: (0, i, 0)),
            pl.BlockSpec((CONV_B - 1, nb, w_b), lambda i: (0, i, 0)),
            full((CONV_A, w_a)), full((1, w_a)), full((w_a, 2 * w_a)), full((1, 2 * w_a)), full((1, w_a)),
            full((CONV_B, w_b)), full((1, w_b)), full((1, w_b)), full((1, w_b)),
        ],
        out_specs=[
            pl.BlockSpec((ts, nb, w_a + w_b), lambda i: (0, i, 0)),
            pl.BlockSpec((nb, w_a), lambda i: (i, 0)),
            pl.BlockSpec((CONV_A - 1, nb, w_a), lambda i: (0, i, 0)),
            pl.BlockSpec((CONV_B - 1, nb, w_b), lambda i: (0, i, 0)),
        ],
        out_shape=[
            jax.ShapeDtypeStruct((ts, n_seq, w_a + w_b), BF16),
            jax.ShapeDtypeStruct((n_seq, w_a), F32),
            jax.ShapeDtypeStruct((CONV_A - 1, n_seq, w_a), F32),
            jax.ShapeDtypeStruct((CONV_B - 1, n_seq, w_b), F32),
        ],
        compiler_params=_cparams("arbitrary"),
        name="mix_sample_ab",
    )(p_t, h0, bufa_t, bufb_t,
      lw["conv_a_w"], lw["conv_a_b"], lw["wg"], lw["bg"], lw["lam"],
      lw["conv_b_w"], lw["conv_b_b"], lw["ln_g"], lw["ln_b"])


def _mix_sample_c_kernel(q_ref, k_ref, v_ref, g_ref, oab_ref, s_ref, cos_ref, sin_ref,
                         dmat_ref, xi_ref, zeta_ref, gcol_ref, rng_ref, tile_ref, fold_ref,
                         mix_ref, sout_ref, qs_ref, kzs_ref, vs_ref, oi_ref,
                         *, rows, ts, w_c, w_ab):
    n_pair = w_c // LANES
    hd = LANES // 2
    lane = lax.broadcasted_iota(jnp.int32, (rows, LANES), 1)
    head0 = lane < hd
    first_half = (lane % hd) < (hd // 2)
    cos = cos_ref[...]
    sin = sin_ref[...]
    for p in range(n_pair):
        cs = slice(p * LANES, (p + 1) * LANES)
        q = _rope_pair(q_ref[:, cs], cos, sin, first_half)
        k = _rope_pair(k_ref[:, cs], cos, sin, first_half) * (1.0 / hd) ** 0.5
        v = v_ref[:, cs]
        qb, kb, vb = q.astype(BF16), k.astype(BF16), v.astype(BF16)
        oi_ref[:, cs] = _retention_intra_pair(qb, kb, vb, dmat_ref[2 * p], dmat_ref[2 * p + 1], head0)
        qs_ref[:, cs] = q
        kzs_ref[:, cs] = k * zeta_ref[:, cs]
        vs_ref[:, cs] = v
    grp = SUBLANES
    per_grp = grp // ts
    stack = NH_C * grp
    srow = lax.broadcasted_iota(jnp.int32, (stack, w_c), 0)
    slane = lax.broadcasted_iota(jnp.int32, (stack, w_c), 1)
    own_head = (srow // grp) == (slane // hd)
    tile_m = tile_ref[...]
    fold_m = fold_ref[...]
    for gi in range(rows // grp):
        rs = slice(gi * grp, (gi + 1) * grp)
        q8 = jnp.concatenate([qs_ref[rs, :]] * NH_C, axis=0)
        kz8 = jnp.concatenate([kzs_ref[rs, :]] * NH_C, axis=0)
        v8 = jnp.concatenate([vs_ref[rs, :]] * NH_C, axis=0)
        vr = jnp.dot(jnp.where(own_head, v8, 0.0).astype(BF16), fold_m, preferred_element_type=F32).astype(BF16)
        o_stack = jnp.zeros((stack, hd), F32)
        for s2 in range(per_grp):
            sq = gi * per_grp + s2
            own_seq = ((srow % grp) // ts) == s2
            sel = jnp.logical_and(own_head, own_seq)
            s_old = s_ref[sq].reshape(w_c, hd)
            o_stack = o_stack + jnp.dot(jnp.where(sel, q8, 0.0).astype(BF16), s_old.astype(BF16),
                                        preferred_element_type=F32)
            upd = lax.dot_general(jnp.where(sel, kz8, 0.0).astype(BF16), vr, (((0,), (0,)), ((), ())),
                                  preferred_element_type=F32)
            sout_ref[sq] = (gcol_ref[...] * s_old + upd).reshape(NH_C, hd, hd)
        spread = jnp.dot(o_stack, tile_m, preferred_element_type=F32, precision=HIGHEST)
        spread = jnp.where(own_head, spread, 0.0).reshape(NH_C, grp, w_c)
        oi_ref[rs, :] = oi_ref[rs, :] + jnp.sum(spread, axis=0) * xi_ref[rs, :]
    mix_ref[:, 0:w_ab] = oab_ref[...]
    for p in range(n_pair):
        cs = slice(p * LANES, (p + 1) * LANES)
        on = _head_norm_pair(oi_ref[:, cs], head0, 1.0 / hd) * rng_ref[:, cs]
        mix_ref[:, w_ab + p * LANES:w_ab + (p + 1) * LANES] = (_silu(g_ref[:, cs]) * on).astype(BF16)


def _mix_sample_c(proj, row0, n_seq, ts, oab, state, lw):
    w_a, w_b, w_c = lw["w_a"], lw["w_b"], lw["w_c"]
    hd = LANES // 2
    n_rows = n_seq * ts
    rows = 64 if n_rows % 64 == 0 else n_rows
    assert SUBLANES % ts == 0 and rows % SUBLANES == 0 and row0 % rows == 0
    seq_per = rows // ts
    c0 = 2 * w_a + 2 * w_b
    assert c0 % w_c == 0
    cb = c0 // w_c
    rb = row0 // rows
    log_g = jnp.log1p(-jnp.exp2(-5.0 - jnp.arange(NH_C, dtype=F32)))
    r = jnp.arange(rows)
    step = (r % ts).astype(F32)
    diff = step[:, None] - step[None, :]
    live = jnp.logical_and((r[:, None] // ts) == (r[None, :] // ts), diff >= 0)
    dmat = jnp.where(live[None], jnp.exp(jnp.where(live, diff, 0.0)[None] * log_g[:, None, None]), 0.0)
    xi = jnp.repeat(jnp.exp((step + 1.0)[None, :] * log_g[:, None]).T, hd, axis=1)
    zeta = jnp.repeat(jnp.exp((ts - 1.0 - step)[None, :] * log_g[:, None]).T, hd, axis=1)
    gcol = jnp.broadcast_to(jnp.repeat(jnp.exp(ts * log_g), hd)[:, None], (w_c, hd))
    pos = PAST_LEN + (r % ts).astype(jnp.int32)
    cos, sin = _rope_tables(pos)
    tile_m = jnp.tile(jnp.eye(hd, dtype=F32), (1, NH_C))
    fold_m = jnp.tile(jnp.eye(hd, dtype=BF16), (NH_C, 1))
    full = lambda shape: pl.BlockSpec(shape, lambda i: (0,) * len(shape))
    col = lambda j: pl.BlockSpec((rows, w_c), lambda i: (rb + i, cb + j))
    kern = functools.partial(_mix_sample_c_kernel, rows=rows, ts=ts, w_c=w_c, w_ab=w_a + w_b)
    return pl.pallas_call(
        kern,
        grid=(n_rows // rows,),
        in_specs=[
            col(0), col(1), col(2), col(3),
            pl.BlockSpec((rows, w_a + w_b), lambda i: (i, 0)),
            pl.BlockSpec((seq_per, NH_C, hd, hd), lambda i: (i, 0, 0, 0)),
            full((rows, LANES)), full((rows, LANES)),
            full((NH_C, rows, rows)), full((rows, w_c)), full((rows, w_c)), full((w_c, hd)), full((1, w_c)),
            full((hd, w_c)), full((w_c, hd)),
        ],
        out_specs=[
            pl.BlockSpec((rows, w_a + w_b + w_c), lambda i: (i, 0)),
            pl.BlockSpec((seq_per, NH_C, hd, hd), lambda i: (i, 0, 0, 0)),
        ],
        out_shape=[
            jax.ShapeDtypeStruct((n_rows, w_a + w_b + w_c), BF16),
            jax.ShapeDtypeStruct(state.shape, F32),
        ],
        scratch_shapes=[pltpu.VMEM((rows, w_c), F32)] * 4,
        compiler_params=_cparams("arbitrary"),
        name="mix_sample_c",
    )(proj, proj, proj, proj, oab, state, cos, sin, dmat, xi, zeta, gcol, lw["ret_g"], tile_m, fold_m)


def _route(logits):
    shape = logits.shape
    lane = lax.broadcasted_iota(jnp.int32, shape, 1)
    neg = jnp.full(shape, -jnp.inf, F32)
    big = jnp.full(shape, LANES, jnp.int32)
    is_g = lane < N_GROUPS
    gl = jnp.where(is_g, logits, neg)
    gmax = jnp.max(gl, axis=-1, keepdims=True)
    p_top = 1.0 / jnp.sum(jnp.exp(gl - gmax), axis=-1, keepdims=True)
    g_sel = jnp.min(jnp.where(gl == gmax, lane, big), axis=-1, keepdims=True)
    lo = N_GROUPS + g_sel * EXPERTS_PER_GROUP
    in_grp = jnp.logical_and(lane >= lo, lane < lo + EXPERTS_PER_GROUP)
    el = jnp.where(in_grp, logits, neg)
    v1 = jnp.max(el, axis=-1, keepdims=True)
    i1 = jnp.min(jnp.where(el == v1, lane, big), axis=-1, keepdims=True)
    el2 = jnp.where(lane == i1, neg, el)
    v2 = jnp.max(el2, axis=-1, keepdims=True)
    i2 = jnp.min(jnp.where(el2 == v2, lane, big), axis=-1, keepdims=True)
    e2 = jnp.exp(v2 - v1)
    den = 1.0 + e2
    g1 = p_top * (1.0 / den)
    g2 = p_top * (e2 / den)
    return i1 - N_GROUPS, i2 - N_GROUPS, g1, g2


def _out_proj_kernel(x_ref, mixp_ref, mixs_ref, w_ref, g_ref, wr_ref, br_ref,
                     xmid_ref, xn_ref, ri_ref, rg_ref, counts_ref, carry_ref, *, n_prompt_tiles):
    i = pl.program_id(0)

    @pl.when(i == 0)
    def _():
        carry_ref[...] = jnp.zeros_like(carry_ref)

    def finish(mix):
        xm = x_ref[...] + jnp.dot(mix, w_ref[...], preferred_element_type=F32)
        xmid_ref[...] = xm
        xn = _rms(xm, g_ref[...])
        xn_ref[...] = xn
        logits = jnp.dot(xn, wr_ref[...], preferred_element_type=F32, precision=HIGHEST) + br_ref[...]
        e1, e2, g1, g2 = _route(logits)
        tm = logits.shape[0]
        lane = lax.broadcasted_iota(jnp.int32, logits.shape, 1)
        oh1 = lane == e1
        oh2 = lane == e2
        onehot = jnp.logical_or(oh1, oh2).astype(BF16)
        earlier = (lax.broadcasted_iota(jnp.int32, (tm, tm), 1)
                   < lax.broadcasted_iota(jnp.int32, (tm, tm), 0)).astype(BF16)
        seen = carry_ref[0:1, :] + jnp.dot(earlier, onehot, preferred_element_type=F32)
        rank1 = jnp.sum(jnp.where(oh1, seen, 0.0), axis=-1, keepdims=True).astype(jnp.int32)
        rank2 = jnp.sum(jnp.where(oh2, seen, 0.0), axis=-1, keepdims=True).astype(jnp.int32)
        total = carry_ref[0:1, :] + jnp.sum(onehot.astype(F32), axis=0, keepdims=True)
        carry_ref[...] = jnp.broadcast_to(total, carry_ref.shape)
        counts_ref[...] = jnp.broadcast_to(total, counts_ref.shape)
        zi = jnp.zeros(logits.shape, jnp.int32)
        ri_ref[...] = jnp.where(lane == 0, e1, jnp.where(lane == 1, e2, jnp.where(
            lane == 2, rank1, jnp.where(lane == 3, rank2, zi))))
        rg_ref[...] = jnp.where(lane == 0, g1, jnp.where(lane == 1, g2, 0.0))

    @pl.when(i < n_prompt_tiles)
    def _():
        finish(mixp_ref[...])

    @pl.when(i >= n_prompt_tiles)
    def _():
        finish(mixs_ref[...])


def _out_proj(x, mix_p, mix_s, w_out, g_ffn, w_route, b_route):
    n, d = x.shape
    n_p, d_mix = mix_p.shape
    n_s = mix_s.shape[0]
    tm = math.gcd(_row_tile(n_p), _row_tile(n_s))
    tp = n_p // tm
    ts_last = n_s // tm - 1
    kern = functools.partial(_out_proj_kernel, n_prompt_tiles=tp)
    return pl.pallas_call(
        kern,
        grid=(n // tm,),
        in_specs=[
            pl.BlockSpec((tm, d), lambda i: (i, 0)),
            pl.BlockSpec((tm, d_mix), lambda i: (jnp.minimum(i, tp - 1), 0)),
            pl.BlockSpec((tm, d_mix), lambda i: (jnp.clip(i - tp, 0, ts_last), 0)),
            pl.BlockSpec((d_mix, d), lambda i: (0, 0)),
            pl.BlockSpec((1, d), lambda i: (0, 0)),
            pl.BlockSpec((d, LANES), lambda i: (0, 0)),
            pl.BlockSpec((1, LANES), lambda i: (0, 0)),
        ],
        out_specs=[
            pl.BlockSpec((tm, d), lambda i: (i, 0)),
            pl.BlockSpec((tm, d), lambda i: (i, 0)),
            pl.BlockSpec((tm, LANES), lambda i: (i, 0)),
            pl.BlockSpec((tm, LANES), lambda i: (i, 0)),
            pl.BlockSpec((SUBLANES, LANES), lambda i: (0, 0)),
        ],
        out_shape=[
            jax.ShapeDtypeStruct((n, d), F32),
            jax.ShapeDtypeStruct((n, d), F32),
            jax.ShapeDtypeStruct((n, LANES), jnp.int32),
            jax.ShapeDtypeStruct((n, LANES), F32),
            jax.ShapeDtypeStruct((SUBLANES, LANES), F32),
        ],
        scratch_shapes=[pltpu.VMEM((SUBLANES, LANES), F32)],
        compiler_params=_cparams("arbitrary"),
        name="out_proj_route",
    )(x, mix_p, mix_s, w_out, g_ffn, w_route, b_route)


SLOT_TILE = 256
DMA_UNROLL = 8


def _slot_plan(counts, n_tokens):
    n_slots = 2 * n_tokens + N_EXPERTS * SLOT_TILE
    n_slots = -(-n_slots // SLOT_TILE) * SLOT_TILE
    n_tiles = n_slots // SLOT_TILE
    padded = (counts + (SLOT_TILE - 1)) // SLOT_TILE * SLOT_TILE
    ends = jnp.cumsum(padded)
    offs = ends - padded
    tile_start = jnp.arange(n_tiles, dtype=jnp.int32) * SLOT_TILE
    valid = tile_start < ends[-1]
    probe = jnp.minimum(tile_start, ends[-1] - 1)
    tile_expert = jnp.minimum(jnp.searchsorted(ends, probe, side="right"), N_EXPERTS - 1).astype(jnp.int32)
    return n_slots, offs, tile_expert, valid.astype(jnp.int32)


def _row_copy(src_ref, src_row, dst_ref, dst_row, sem):
    return pltpu.make_async_copy(src_ref.at[pl.ds(src_row, 1), :], dst_ref.at[pl.ds(dst_row, 1), :], sem)


def _dispatch_kernel(pos1_ref, pos2_ref, xn_ref, xs_in_ref, xs_ref, sem):
    del xs_in_ref
    tm = xn_ref.shape[0]

    def issue(t, c):
        _row_copy(xn_ref, t, xs_ref, pos1_ref[t], sem).start()
        _row_copy(xn_ref, t, xs_ref, pos2_ref[t], sem).start()
        return c

    lax.fori_loop(0, tm, issue, 0, unroll=DMA_UNROLL)

    def drain(t, c):
        _row_copy(xn_ref, 0, xs_ref, 0, sem).wait()
        _row_copy(xn_ref, 0, xs_ref, 0, sem).wait()
        return c

    lax.fori_loop(0, tm, drain, 0, unroll=DMA_UNROLL)


def _dispatch(xn, pos1, pos2, n_slots):
    n, d = xn.shape
    tm = _row_tile(n)
    smem_rows = lambda: pl.BlockSpec((tm,), lambda i: (i,), memory_space=pltpu.SMEM)
    return pl.pallas_call(
        _dispatch_kernel,
        grid=(n // tm,),
        in_specs=[smem_rows(), smem_rows(),
                  pl.BlockSpec((tm, d), lambda i: (i, 0)),
                  pl.BlockSpec(memory_space=pl.ANY)],
        out_specs=pl.BlockSpec(memory_space=pl.ANY),
        out_shape=jax.ShapeDtypeStruct((n_slots, d), xn.dtype),
        scratch_shapes=[pltpu.SemaphoreType.DMA],
        input_output_aliases={3: 0},
        compiler_params=_cparams("arbitrary"),
        name="moe_dispatch",
    )(pos1, pos2, xn, jnp.zeros((n_slots, d), xn.dtype))


def _experts_kernel(te_ref, tv_ref, xs_ref, w1_ref, w3_ref, w2_ref, ys_ref, w1b_ref, w3b_ref, w2b_ref):
    j = pl.program_id(0)
    fresh = jnp.logical_or(j == 0, te_ref[j] != te_ref[jnp.maximum(j - 1, 0)])

    @pl.when(fresh)
    def _():
        w1b_ref[...] = w1_ref[0].astype(BF16)
        w3b_ref[...] = w3_ref[0].astype(BF16)
        w2b_ref[...] = w2_ref[0].astype(BF16)

    @pl.when(tv_ref[j] != 0)
    def _():
        x = xs_ref[...].astype(BF16)
        h1 = jnp.dot(x, w1b_ref[...], preferred_element_type=F32)
        h3 = jnp.dot(x, w3b_ref[...], preferred_element_type=F32)
        h = (_silu(h1) * h3).astype(BF16)
        ys_ref[...] = jnp.dot(h, w2b_ref[...], preferred_element_type=F32)

    @pl.when(tv_ref[j] == 0)
    def _():
        ys_ref[...] = jnp.zeros_like(ys_ref)


def _experts(xs, tile_expert, tile_valid, w1, w3, w2):
    n_slots, d = xs.shape
    _, _, d_e = w1.shape
    grid_spec = pltpu.PrefetchScalarGridSpec(
        num_scalar_prefetch=2,
        grid=(n_slots // SLOT_TILE,),
        in_specs=[
            pl.BlockSpec((SLOT_TILE, d), lambda j, te, tv: (j, 0)),
            pl.BlockSpec((1, d, d_e), lambda j, te, tv: (te[j], 0, 0)),
            pl.BlockSpec((1, d, d_e), lambda j, te, tv: (te[j], 0, 0)),
            pl.BlockSpec((1, d_e, d), lambda j, te, tv: (te[j], 0, 0)),
        ],
        out_specs=pl.BlockSpec((SLOT_TILE, d), lambda j, te, tv: (j, 0)),
        scratch_shapes=[pltpu.VMEM((d, d_e), BF16), pltpu.VMEM((d, d_e), BF16), pltpu.VMEM((d_e, d), BF16)],
    )
    return pl.pallas_call(
        _experts_kernel,
        grid_spec=grid_spec,
        out_shape=jax.ShapeDtypeStruct((n_slots, d), F32),
        compiler_params=_cparams("arbitrary"),
        name="moe_experts",
    )(tile_expert, tile_valid, xs, w1, w3, w2)


def _combine_kernel(pos1_ref, pos2_ref, xmid_ref, rg_ref, gfin_ref, ys_ref, o_ref, y1_ref, y2_ref, sem,
                    *, final_norm):
    tm = xmid_ref.shape[0]

    def issue(t, c):
        _row_copy(ys_ref, pos1_ref[t], y1_ref, t, sem).start()
        _row_copy(ys_ref, pos2_ref[t], y2_ref, t, sem).start()
        return c

    lax.fori_loop(0, tm, issue, 0, unroll=DMA_UNROLL)

    def drain(t, c):
        _row_copy(ys_ref, 0, y1_ref, 0, sem).wait()
        _row_copy(ys_ref, 0, y2_ref, 0, sem).wait()
        return c

    lax.fori_loop(0, tm, drain, 0, unroll=DMA_UNROLL)
    rg = rg_ref[...]
    out = xmid_ref[...] + (rg[:, 0:1] * y1_ref[...] + rg[:, 1:2] * y2_ref[...])
    o_ref[...] = _rms(out, gfin_ref[...]) if final_norm else out


def _combine(xmid, rg, pos1, pos2, ys, g_final, final_norm):
    n, d = xmid.shape
    tm = _row_tile(n)
    smem_rows = lambda: pl.BlockSpec((tm,), lambda i: (i,), memory_space=pltpu.SMEM)
    kern = functools.partial(_combine_kernel, final_norm=final_norm)
    return pl.pallas_call(
        kern,
        grid=(n // tm,),
        in_specs=[smem_rows(), smem_rows(),
                  pl.BlockSpec((tm, d), lambda i: (i, 0)),
                  pl.BlockSpec((tm, LANES), lambda i: (i, 0)),
                  pl.BlockSpec((1, d), lambda i: (0, 0)),
                  pl.BlockSpec(memory_space=pl.ANY)],
        out_specs=pl.BlockSpec((tm, d), lambda i: (i, 0)),
        out_shape=jax.ShapeDtypeStruct((n, d), F32),
        scratch_shapes=[pltpu.VMEM((tm, d), F32), pltpu.VMEM((tm, d), F32), pltpu.SemaphoreType.DMA],
        compiler_params=_cparams("arbitrary"),
        name="moe_combine",
    )(pos1, pos2, xmid, rg, g_final, ys)


def _moe_routed(xmid, xn, ri, rg, counts, w1, w3, w2, g_final, final_norm):
    n = xmid.shape[0]
    n_slots, offs, tile_expert, tile_valid = _slot_plan(counts[0, :N_EXPERTS].astype(jnp.int32), n)
    pos1 = offs[ri[:, 0]] + ri[:, 2]
    pos2 = offs[ri[:, 1]] + ri[:, 3]
    xs = _dispatch(xn, pos1, pos2, n_slots)
    ys = _experts(xs, tile_expert, tile_valid, w1, w3, w2)
    return _combine(xmid, rg, pos1, pos2, ys, g_final, final_norm)


def kernel(x_prompt, x_sample, state_lru_h, state_lru_conv, state_conv, state_ret, norm_mix_g, norm_ffn_g, final_norm_g, w_in, lru_conv_w, lru_conv_b, lru_wa, lru_ba, lru_wx, lru_bx, lru_lambda, cm_dw_w, cm_dw_b, cm_ln_g, cm_ln_b, ret_norm_g, w_out, moe_w_group, moe_b_group, moe_w_expert, moe_b_expert, moe_w1, moe_w3, moe_w2):
    bp, tp, d = x_prompt.shape
    bs, ts, _ = x_sample.shape
    depth = w_in.shape[0]
    w_a = lru_lambda.shape[1]
    w_b = cm_dw_b.shape[1]
    w_c = ret_norm_g.shape[1]
    n_p, n_s = bp * tp, bs * ts
    row = lambda v: v.reshape(1, -1)

    x = jnp.concatenate([x_prompt.reshape(n_p, d), x_sample.reshape(n_s, d)], axis=0)
    outs = {k: [] for k in ("p_h", "p_ca", "p_cb", "p_s", "s_h", "s_ca", "s_cb", "s_s")}
    for l in range(depth):
        lw = dict(
            w_a=w_a, w_b=w_b, w_c=w_c,
            conv_a_w=lru_conv_w[l], conv_a_b=row(lru_conv_b[l]),
            wg=_blockdiag_gate_weights(lru_wa[l], lru_wx[l]).astype(BF16),
            bg=row(jnp.concatenate([lru_ba[l], lru_bx[l]])), lam=row(lru_lambda[l]),
            conv_b_w=cm_dw_w[l], conv_b_b=row(cm_dw_b[l]), ln_g=row(cm_ln_g[l]), ln_b=row(cm_ln_b[l]),
            ret_g=row(ret_norm_g[l]),
        )
        proj = _in_proj(x, row(norm_mix_g[l]), w_in[l].astype(BF16))
        mix_p, h_p, ca_p, cb_p, s_p = _mix_prompt(proj, bp, tp, lw)
        outs["p_h"].append(h_p[:, 0, :])
        outs["p_ca"].append(ca_p[:, SUBLANES - (CONV_A - 1):, :])
        outs["p_cb"].append(cb_p[:, 4 * SUBLANES - (CONV_B - 1):, :])
        outs["p_s"].append(_unpair_state(s_p))
        p_t = proj[n_p:, :2 * w_a + 2 * w_b].reshape(bs, ts, -1).transpose(1, 0, 2)
        oab_t, h_s, ca_t, cb_t = _mix_sample_ab(
            p_t, state_lru_h[l], state_lru_conv[l].transpose(1, 0, 2), state_conv[l].transpose(1, 0, 2), lw)
        oab = oab_t.transpose(1, 0, 2).reshape(n_s, w_a + w_b)
        mix_s, s_s = _mix_sample_c(proj, n_p, bs, ts, oab, state_ret[l], lw)
        outs["s_h"].append(h_s)
        outs["s_ca"].append(ca_t.transpose(1, 0, 2))
        outs["s_cb"].append(cb_t.transpose(1, 0, 2))
        outs["s_s"].append(s_s)
        w_route = jnp.zeros((d, LANES), F32)
        w_route = w_route.at[:, :N_GROUPS].set(moe_w_group[l]).at[:, N_GROUPS:N_GROUPS + N_EXPERTS].set(moe_w_expert[l])
        b_route = jnp.zeros((1, LANES), F32)
        b_route = b_route.at[0, :N_GROUPS].set(moe_b_group[l]).at[0, N_GROUPS:N_GROUPS + N_EXPERTS].set(moe_b_expert[l])
        xmid, xn, ri, rg, counts = _out_proj(x, mix_p, mix_s, w_out[l].astype(BF16), row(norm_ffn_g[l]),
                                             w_route, b_route)
        x = _moe_routed(xmid, xn, ri, rg, counts, moe_w1[l], moe_w3[l], moe_w2[l],
                        row(final_norm_g), final_norm=(l == depth - 1))
    y_prompt = x[:n_p].reshape(bp, tp, d)
    y_sample = x[n_p:].reshape(bs, ts, d)
    st = lambda k: jnp.stack(outs[k])
    return (y_prompt, y_sample, st("p_h"), st("p_ca"), st("p_cb"), st("p_s"),
            st("s_h"), st("s_ca"), st("s_cb"), st("s_s"))
```

```python
import functools
import math

import jax
import jax.numpy as jnp
import numpy as np
from jax import lax
from jax.experimental import pallas as pl
from jax.experimental.pallas import tpu as pltpu

PAST_LEN = 16384
NH_A = 4
LRU_C = 8.0
CONV_A = 4
CONV_B = 31
NH_C = 8
RET_CHUNK = 128
ROPE_BASE = 10000.0
N_GROUPS = 4
EXPERTS_PER_GROUP = 8
N_EXPERTS = N_GROUPS * EXPERTS_PER_GROUP
EPS = 1e-6

LANES = 128
SUBLANES = 8
VMEM_LIMIT_BYTES = 56 * 1024 * 1024

F32 = jnp.float32
BF16 = jnp.bfloat16
HIGHEST = lax.Precision.HIGHEST


def _cparams(*sem):
    return pltpu.CompilerParams(dimension_semantics=sem, vmem_limit_bytes=VMEM_LIMIT_BYTES)


def _row_tile(n):
    for t in (512, 256, 128, 64, 32, 16, 8):
        if n % t == 0:
            return t
    raise ValueError(f"token count {n} must be a multiple of 8")


def _rms(x, g):
    ms = jnp.mean(x * x, axis=-1, keepdims=True)
    return x * lax.rsqrt(ms + EPS) * g


def _sigmoid(x):
    return jax.nn.sigmoid(x)


def _silu(x):
    return x * jax.nn.sigmoid(x)


def _in_proj_kernel(x_ref, g_ref, w_ref, o_ref):
    u = _rms(x_ref[...], g_ref[...]).astype(BF16)
    o_ref[...] = jnp.dot(u, w_ref[...], preferred_element_type=F32)


def _in_proj(x, g, w):
    n, d = x.shape
    d_in = w.shape[1]
    tm = _row_tile(n)
    return pl.pallas_call(
        _in_proj_kernel,
        grid=(n // tm,),
        in_specs=[
            pl.BlockSpec((tm, d), lambda i: (i, 0)),
            pl.BlockSpec((1, d), lambda i: (0, 0)),
            pl.BlockSpec((d, d_in), lambda i: (0, 0)),
        ],
        out_specs=pl.BlockSpec((tm, d_in), lambda i: (i, 0)),
        out_shape=jax.ShapeDtypeStruct((n, d_in), F32),
        compiler_params=_cparams("arbitrary"),
        name="in_proj",
    )(x, g, w)


def _rope_pair(x, cos, sin_signed, first_half):
    partner = jnp.where(first_half, pltpu.roll(x, 96, 1), pltpu.roll(x, 32, 1))
    return x * cos + partner * sin_signed


def _head_norm_pair(o, head0, inv_d):
    zero = jnp.zeros_like(o)
    s0 = jnp.sum(jnp.where(head0, o, zero), axis=-1, keepdims=True)
    s1 = jnp.sum(jnp.where(head0, zero, o), axis=-1, keepdims=True)
    oc = o - jnp.where(head0, s0, s1) * inv_d
    q = oc * oc
    v0 = jnp.sum(jnp.where(head0, q, zero), axis=-1, keepdims=True)
    v1 = jnp.sum(jnp.where(head0, zero, q), axis=-1, keepdims=True)
    return oc * lax.rsqrt(jnp.where(head0, v0, v1) * inv_d + EPS)


def _retention_intra_pair(qb, kb, vb, dmat0, dmat1, head0):
    outs = []
    for j, dm in ((0, dmat0), (1, dmat1)):
        keep = head0 if j == 0 else jnp.logical_not(head0)
        qm = jnp.where(keep, qb, jnp.zeros_like(qb))
        s = lax.dot_general(qm, kb, (((1,), (1,)), ((), ())), preferred_element_type=F32)
        inner = (s * dm).astype(BF16)
        outs.append(jnp.dot(inner, vb, preferred_element_type=F32))
    return jnp.where(head0, outs[0], outs[1])


def _lru_gates(xa, wg, bg, lam):
    w_a = xa.shape[-1]
    gates = jnp.dot(xa.astype(BF16), wg, preferred_element_type=F32) + bg
    r = _sigmoid(gates[:, :w_a])
    i = _sigmoid(gates[:, w_a:])
    z = -lam
    softplus = jnp.maximum(z, 0.0) + jnp.log1p(jnp.exp(-jnp.abs(z)))
    log_a = -LRU_C * r * softplus
    a = jnp.exp(log_a)
    bt = jnp.sqrt(-jnp.tanh(log_a) * (a * a + 1.0)) * (i * xa)
    return a, bt


def _layernorm(x, g, b):
    mu = jnp.mean(x, axis=-1, keepdims=True)
    xc = x - mu
    var = jnp.mean(xc * xc, axis=-1, keepdims=True)
    return xc * lax.rsqrt(var + EPS) * g + b


CONV_B_ROWS = 64


def _mix_prompt_kernel(proj_ref, cos_ref, sin_ref, h0_ref, bufa0_ref, bufb0_ref, s0_ref,
                       caw_ref, cab_ref, wg_ref, bg_ref, lam_ref,
                       cbw_ref, cbb_ref, lng_ref, lnb_ref, rng_ref,
                       dmat_ref, xi_ref, zeta_ref, gmat_ref,
                       mix_ref, hout_ref, bufa_out_ref, bufb_out_ref, sout_ref,
                       xpa_ref, xpb_ref, sh_ref, a_ref, b_ref, hseq_ref, hcar_ref, s_ref,
                       *, tc, w_a, w_b, w_c, chunk):
    t = pl.program_id(1)
    n_t = pl.num_programs(1)
    pa = SUBLANES
    pb = 4 * SUBLANES

    @pl.when(t == 0)
    def _():
        xpa_ref[0:pa, :] = bufa0_ref[0]
        xpb_ref[0:pb, :] = bufb0_ref[0]
        hcar_ref[...] = h0_ref[0]
        s_ref[...] = s0_ref[0]

    a_gate = proj_ref[:, 0:w_a]
    xpa_ref[pa:pa + tc, :] = proj_ref[:, w_a:2 * w_a]
    xa = cab_ref[...] + caw_ref[0:1, :] * xpa_ref[pa - 3:pa - 3 + tc, :]
    for j in range(1, CONV_A):
        xa = xa + caw_ref[j:j + 1, :] * xpa_ref[pa - 3 + j:pa - 3 + j + tc, :]
    a, bt = _lru_gates(xa, wg_ref[...], bg_ref[...], lam_ref[...])
    a_ref[...] = a
    b_ref[...] = bt
    row8 = lax.broadcasted_iota(jnp.int32, (SUBLANES, w_a), 0)

    def scan_block(i, hprev):
        r0 = pl.multiple_of(i * SUBLANES, SUBLANES)
        ab = a_ref[pl.ds(r0, SUBLANES), :]
        bb = b_ref[pl.ds(r0, SUBLANES), :]
        for k in (1, 2, 4):
            a_sh = jnp.where(row8 >= k, pltpu.roll(ab, k, 0), 1.0)
            b_sh = jnp.where(row8 >= k, pltpu.roll(bb, k, 0), 0.0)
            bb = ab * b_sh + bb
            ab = ab * a_sh
        h = ab * hprev + bb
        hseq_ref[pl.ds(r0, SUBLANES), :] = h
        return jnp.broadcast_to(h[SUBLANES - 1:SUBLANES, :], (SUBLANES, w_a))

    hcar_ref[...] = lax.fori_loop(0, tc // SUBLANES, scan_block, hcar_ref[...])
    out_a = jax.nn.gelu(a_gate) * hseq_ref[...]
    mix_ref[:, 0:w_a] = out_a.astype(BF16)
    xpa_ref[0:pa, :] = xpa_ref[tc:tc + pa, :]

    b0 = 2 * w_a
    xpb_ref[pb:pb + tc, :] = proj_ref[:, b0:b0 + w_b] * _sigmoid(proj_ref[:, b0 + w_b:b0 + 2 * w_b])
    first = pb - (CONV_B - 1)
    for r in range(SUBLANES):
        span = tc + (CONV_B - 1 - r) // SUBLANES * SUBLANES
        sh_ref[r, 0:span, :] = xpb_ref[first + r:first + r + span, :]
    for blk in range(tc // CONV_B_ROWS):
        r0 = blk * CONV_B_ROWS
        acc = jnp.broadcast_to(cbb_ref[...], (CONV_B_ROWS, w_b))
        for j in range(CONV_B):
            r = j % SUBLANES
            acc = acc + cbw_ref[j:j + 1, :] * sh_ref[r, r0 + j - r:r0 + j - r + CONV_B_ROWS, :]
        out_b = _silu(_layernorm(acc, lng_ref[...], lnb_ref[...]))
        mix_ref[r0:r0 + CONV_B_ROWS, w_a:w_a + w_b] = out_b.astype(BF16)
    xpb_ref[0:pb, :] = xpb_ref[tc:tc + pb, :]

    c0 = 2 * w_a + 2 * w_b
    n_pair = w_c // LANES
    lane = lax.broadcasted_iota(jnp.int32, (chunk, LANES), 1)
    head0 = lane < (LANES // 2)
    first_half = (lane % (LANES // 2)) < (LANES // 4)
    rl = lax.broadcasted_iota(jnp.int32, (LANES, LANES), 0) < (LANES // 2)
    cl = lax.broadcasted_iota(jnp.int32, (LANES, LANES), 1) < (LANES // 2)
    same_head = rl == cl
    inv_d = 2.0 / LANES
    for c in range(tc // chunk):
        rows = slice(c * chunk, (c + 1) * chunk)
        cos = cos_ref[rows, :]
        sin = sin_ref[rows, :]
        for p in range(n_pair):
            cq = c0 + p * LANES
            q = _rope_pair(proj_ref[rows, cq:cq + LANES], cos, sin, first_half)
            k = _rope_pair(proj_ref[rows, cq + w_c:cq + w_c + LANES], cos, sin, first_half) * (2.0 / LANES) ** 0.5
            v = proj_ref[rows, cq + 2 * w_c:cq + 2 * w_c + LANES]
            g = proj_ref[rows, cq + 3 * w_c:cq + 3 * w_c + LANES]
            qb, kb, vb = q.astype(BF16), k.astype(BF16), v.astype(BF16)
            o = _retention_intra_pair(qb, kb, vb, dmat_ref[2 * p], dmat_ref[2 * p + 1], head0)
            s_old = s_ref[p]
            o = o + jnp.dot(qb, s_old.astype(BF16), preferred_element_type=F32) * xi_ref[p]
            kz = (k * zeta_ref[p]).astype(BF16)
            upd = lax.dot_general(kz, vb, (((0,), (0,)), ((), ())), preferred_element_type=F32)
            s_ref[p] = gmat_ref[p] * s_old + jnp.where(same_head, upd, 0.0)
            on = _head_norm_pair(o, head0, inv_d) * rng_ref[:, p * LANES:(p + 1) * LANES]
            mix_ref[rows, w_a + w_b + p * LANES:w_a + w_b + (p + 1) * LANES] = (_silu(g) * on).astype(BF16)

    @pl.when(t == n_t - 1)
    def _():
        hout_ref[0] = hcar_ref[...]
        bufa_out_ref[0] = xpa_ref[0:pa, :]
        bufb_out_ref[0] = xpb_ref[0:pb, :]
        sout_ref[0] = s_ref[...]


def _retention_tables(chunk):
    log_g = jnp.log1p(-jnp.exp2(-5.0 - jnp.arange(NH_C, dtype=F32)))
    idx = jnp.arange(chunk, dtype=F32)
    diff = idx[:, None] - idx[None, :]
    causal = diff >= 0
    dmat = jnp.where(causal[None], jnp.exp(jnp.where(causal, diff, 0.0)[None] * log_g[:, None, None]), 0.0)
    xi = jnp.exp((idx + 1.0)[None, :] * log_g[:, None])
    zeta = jnp.exp((chunk - 1.0 - idx)[None, :] * log_g[:, None])
    g_chunk = jnp.exp(chunk * log_g)
    return dmat, xi, zeta, g_chunk


def _pair_lanes(per_head):
    h, rows = per_head.shape
    half = LANES // 2
    x = jnp.broadcast_to(per_head.reshape(h // 2, 2, rows, 1), (h // 2, 2, rows, half))
    return x.transpose(0, 2, 1, 3).reshape(h // 2, rows, LANES)


def _rope_tables(pos):
    half = LANES // 4
    inv = ROPE_BASE ** (-jnp.arange(half, dtype=F32) / half)
    ang = pos.astype(F32)[:, None] * inv[None, :]
    cos, sin = jnp.cos(ang), jnp.sin(ang)
    return jnp.concatenate([cos] * 4, axis=1), jnp.concatenate([-sin, sin, -sin, sin], axis=1)


def _blockdiag_gate_weights(wa, wx):
    nh, hd, _ = wa.shape
    eye = jnp.eye(nh, dtype=wa.dtype)
    da = (eye[:, None, :, None] * wa[:, :, None, :]).reshape(nh * hd, nh * hd)
    dx = (eye[:, None, :, None] * wx[:, :, None, :]).reshape(nh * hd, nh * hd)
    return jnp.concatenate([da, dx], axis=1)


def _mix_prompt(proj, n_seq, seq_len, lw):
    w_a, w_b, w_c = lw["w_a"], lw["w_b"], lw["w_c"]
    d_in = proj.shape[1]
    d_mix = w_a + w_b + w_c
    chunk = math.gcd(seq_len, RET_CHUNK)
    tc = 256 if seq_len % 256 == 0 else chunk
    n_t = seq_len // tc
    n_pair = w_c // LANES
    dmat, xi, zeta, g_chunk = _retention_tables(chunk)
    xi_p = _pair_lanes(xi)
    zeta_p = _pair_lanes(zeta)
    half = LANES // 2
    gm = jnp.zeros((n_pair, 2, half, 2, half), F32)
    gm = gm.at[:, 0, :, 0, :].set(g_chunk[0::2, None, None]).at[:, 1, :, 1, :].set(g_chunk[1::2, None, None])
    gmat = gm.reshape(n_pair, LANES, LANES)
    cos, sin = _rope_tables(jnp.arange(seq_len, dtype=jnp.int32))
    zeros = lambda *s: jnp.zeros(s, F32)
    full = lambda shape: pl.BlockSpec(shape, lambda b, t: (0,) * len(shape))
    per_seq = lambda shape: pl.BlockSpec((1,) + shape, lambda b, t: (b,) + (0,) * len(shape))
    kern = functools.partial(_mix_prompt_kernel, tc=tc, w_a=w_a, w_b=w_b, w_c=w_c, chunk=chunk)
    return pl.pallas_call(
        kern,
        grid=(n_seq, n_t),
        in_specs=[
            pl.BlockSpec((tc, d_in), lambda b, t: (b * n_t + t, 0)),
            pl.BlockSpec((tc, LANES), lambda b, t: (t, 0)),
            pl.BlockSpec((tc, LANES), lambda b, t: (t, 0)),
            per_seq((SUBLANES, w_a)), per_seq((SUBLANES, w_a)), per_seq((4 * SUBLANES, w_b)),
            per_seq((n_pair, LANES, LANES)),
            full((CONV_A, w_a)), full((1, w_a)), full((w_a, 2 * w_a)), full((1, 2 * w_a)), full((1, w_a)),
            full((CONV_B, w_b)), full((1, w_b)), full((1, w_b)), full((1, w_b)), full((1, w_c)),
            full((NH_C, chunk, chunk)), full((n_pair, chunk, LANES)), full((n_pair, chunk, LANES)),
            full((n_pair, LANES, LANES)),
        ],
        out_specs=[
            pl.BlockSpec((tc, d_mix), lambda b, t: (b * n_t + t, 0)),
            per_seq((SUBLANES, w_a)), per_seq((SUBLANES, w_a)), per_seq((4 * SUBLANES, w_b)),
            per_seq((n_pair, LANES, LANES)),
        ],
        out_shape=[
            jax.ShapeDtypeStruct((n_seq * seq_len, d_mix), BF16),
            jax.ShapeDtypeStruct((n_seq, SUBLANES, w_a), F32),
            jax.ShapeDtypeStruct((n_seq, SUBLANES, w_a), F32),
            jax.ShapeDtypeStruct((n_seq, 4 * SUBLANES, w_b), F32),
            jax.ShapeDtypeStruct((n_seq, n_pair, LANES, LANES), F32),
        ],
        scratch_shapes=[
            pltpu.VMEM((SUBLANES + tc, w_a), F32),
            pltpu.VMEM((4 * SUBLANES + tc, w_b), F32),
            pltpu.VMEM((SUBLANES, 3 * SUBLANES + tc, w_b), F32),
            pltpu.VMEM((tc, w_a), F32),
            pltpu.VMEM((tc, w_a), F32),
            pltpu.VMEM((tc, w_a), F32),
            pltpu.VMEM((SUBLANES, w_a), F32),
            pltpu.VMEM((n_pair, LANES, LANES), F32),
        ],
        compiler_params=_cparams("arbitrary", "arbitrary"),
        name="mix_prompt",
    )(proj, cos, sin,
      zeros(n_seq, SUBLANES, w_a), zeros(n_seq, SUBLANES, w_a), zeros(n_seq, 4 * SUBLANES, w_b),
      zeros(n_seq, n_pair, LANES, LANES),
      lw["conv_a_w"], lw["conv_a_b"], lw["wg"], lw["bg"], lw["lam"],
      lw["conv_b_w"], lw["conv_b_b"], lw["ln_g"], lw["ln_b"], lw["ret_g"],
      dmat, xi_p, zeta_p, gmat)


def _unpair_state(s_pairs):
    b, n_pair = s_pairs.shape[:2]
    half = LANES // 2
    s = s_pairs.reshape(b, n_pair, 2, half, 2, half)
    return jnp.stack([s[:, :, 0, :, 0, :], s[:, :, 1, :, 1, :]], axis=2).reshape(b, 2 * n_pair, half, half)


def _mix_sample_ab_kernel(p_ref, h0_ref, bufa_ref, bufb_ref,
                          caw_ref, cab_ref, wg_ref, bg_ref, lam_ref,
                          cbw_ref, cbb_ref, lng_ref, lnb_ref,
                          o_ref, h_ref, bufa_out_ref, bufb_out_ref, *, ts, w_a, w_b):
    xpa = [bufa_ref[j] for j in range(CONV_A - 1)] + [p_ref[t, :, w_a:2 * w_a] for t in range(ts)]
    xa = []
    for t in range(ts):
        acc = cab_ref[...] + caw_ref[0:1, :] * xpa[t]
        for j in range(1, CONV_A):
            acc = acc + caw_ref[j:j + 1, :] * xpa[t + j]
        xa.append(acc)
    nb = xa[0].shape[0]
    a, bt = _lru_gates(jnp.concatenate(xa, axis=0), wg_ref[...], bg_ref[...], lam_ref[...])
    h = h0_ref[...]
    for t in range(ts):
        h = a[t * nb:(t + 1) * nb, :] * h + bt[t * nb:(t + 1) * nb, :]
        o_ref[t, :, 0:w_a] = (jax.nn.gelu(p_ref[t, :, 0:w_a]) * h).astype(BF16)
    h_ref[...] = h
    for j in range(CONV_A - 1):
        bufa_out_ref[j] = xpa[ts + j]
    b0 = 2 * w_a
    hist = CONV_B - 1
    glu = [p_ref[t, :, b0:b0 + w_b] * _sigmoid(p_ref[t, :, b0 + w_b:b0 + 2 * w_b]) for t in range(ts)]
    xpb = lambda i: bufb_ref[i] if i < hist else glu[i - hist]
    for t in range(ts):
        acc = cbb_ref[...] + cbw_ref[0:1, :] * xpb(t)
        for j in range(1, CONV_B):
            acc = acc + cbw_ref[j:j + 1, :] * xpb(t + j)
        o_ref[t, :, w_a:w_a + w_b] = _silu(_layernorm(acc, lng_ref[...], lnb_ref[...])).astype(BF16)
    for i in range(hist):
        bufb_out_ref[i] = xpb(ts + i)


def _mix_sample_ab(p_t, h0, bufa_t, bufb_t, lw):
    ts, n_seq, width = p_t.shape
    w_a, w_b = lw["w_a"], lw["w_b"]
    nb = 32 if n_seq % 32 == 0 else n_seq
    full = lambda shape: pl.BlockSpec(shape, lambda i: (0,) * len(shape))
    kern = functools.partial(_mix_sample_ab_kernel, ts=ts, w_a=w_a, w_b=w_b)
    return pl.pallas_call(
        kern,
        grid=(n_seq // nb,),
        in_specs=[
            pl.BlockSpec((ts, nb, width), lambda i: (0, i, 0)),
            pl.BlockSpec((nb, w_a), lambda i: (i, 0)),
            pl.BlockSpec((CONV_A - 1, nb, w_a), lambda i: (0, i, 0)),
            pl.BlockSpec((CONV_B - 1, nb, w_b), lambda i: (0, i, 0)),
            full((CONV_A, w_a)), full((1, w_a)), full((w_a, 2 * w_a)), full((1, 2 * w_a)), full((1, w_a)),
            full((CONV_B, w_b)), full((1, w_b)), full((1, w_b)), full((1, w_b)),
        ],
        out_specs=[
            pl.BlockSpec((ts, nb, w_a + w_b), lambda i: (0, i, 0)),
            pl.BlockSpec((nb, w_a), lambda i: (i, 0)),
            pl.BlockSpec((CONV_A - 1, nb, w_a), lambda i: (0, i, 0)),
            pl.BlockSpec((CONV_B - 1, nb, w_b), lambda i: (0, i, 0)),
        ],
        out_shape=[
            jax.ShapeDtypeStruct((ts, n_seq, w_a + w_b), BF16),
            jax.ShapeDtypeStruct((n_seq, w_a), F32),
            jax.ShapeDtypeStruct((CONV_A - 1, n_seq, w_a), F32),
            jax.ShapeDtypeStruct((CONV_B - 1, n_seq, w_b), F32),
        ],
        compiler_params=_cparams("arbitrary"),
        name="mix_sample_ab",
    )(p_t, h0, bufa_t, bufb_t,
      lw["conv_a_w"], lw["conv_a_b"], lw["wg"], lw["bg"], lw["lam"],
      lw["conv_b_w"], lw["conv_b_b"], lw["ln_g"], lw["ln_b"])


def _mix_sample_c_kernel(q_ref, k_ref, v_ref, g_ref, oab_ref, s_ref, cos_ref, sin_ref,
                         dmat_ref, xi_ref, zeta_ref, gcol_ref, rng_ref, tile_ref, fold_ref,
                         mix_ref, sout_ref, qs_ref, kzs_ref, vs_ref, oi_ref,
                         *, rows, ts, w_c, w_ab):
    n_pair = w_c // LANES
    hd = LANES // 2
    lane = lax.broadcasted_iota(jnp.int32, (rows, LANES), 1)
    head0 = lane < hd
    first_half = (lane % hd) < (hd // 2)
    cos = cos_ref[...]
    sin = sin_ref[...]
    for p in range(n_pair):
        cs = slice(p * LANES, (p + 1) * LANES)
        q = _rope_pair(q_ref[:, cs], cos, sin, first_half)
        k = _rope_pair(k_ref[:, cs], cos, sin, first_half) * (1.0 / hd) ** 0.5
        v = v_ref[:, cs]
        qb, kb, vb = q.astype(BF16), k.astype(BF16), v.astype(BF16)
        oi_ref[:, cs] = _retention_intra_pair(qb, kb, vb, dmat_ref[2 * p], dmat_ref[2 * p + 1], head0)
        qs_ref[:, cs] = q
        kzs_ref[:, cs] = k * zeta_ref[:, cs]
        vs_ref[:, cs] = v
    grp = SUBLANES
    per_grp = grp // ts
    stack = NH_C * grp
    srow = lax.broadcasted_iota(jnp.int32, (stack, w_c), 0)
    slane = lax.broadcasted_iota(jnp.int32, (stack, w_c), 1)
    own_head = (srow // grp) == (slane // hd)
    tile_m = tile_ref[...]
    fold_m = fold_ref[...]
    for gi in range(rows // grp):
        rs = slice(gi * grp, (gi + 1) * grp)
        q8 = jnp.concatenate([qs_ref[rs, :]] * NH_C, axis=0)
        kz8 = jnp.concatenate([kzs_ref[rs, :]] * NH_C, axis=0)
        v8 = jnp.concatenate([vs_ref[rs, :]] * NH_C, axis=0)
        vr = jnp.dot(jnp.where(own_head, v8, 0.0).astype(BF16), fold_m, preferred_element_type=F32).astype(BF16)
        o_stack = jnp.zeros((stack, hd), F32)
        for s2 in range(per_grp):
            sq = gi * per_grp + s2
            own_seq = ((srow % grp) // ts) == s2
            sel = jnp.logical_and(own_head, own_seq)
            s_old = s_ref[sq].reshape(w_c, hd)
            o_stack = o_stack + jnp.dot(jnp.where(sel, q8, 0.0).astype(BF16), s_old.astype(BF16),
                                        preferred_element_type=F32)
            upd = lax.dot_general(jnp.where(sel, kz8, 0.0).astype(BF16), vr, (((0,), (0,)), ((), ())),
                                  preferred_element_type=F32)
            sout_ref[sq] = (gcol_ref[...] * s_old + upd).reshape(NH_C, hd, hd)
        spread = jnp.dot(o_stack, tile_m, preferred_element_type=F32, precision=HIGHEST)
        spread = jnp.where(own_head, spread, 0.0).reshape(NH_C, grp, w_c)
        oi_ref[rs, :] = oi_ref[rs, :] + jnp.sum(spread, axis=0) * xi_ref[rs, :]
    mix_ref[:, 0:w_ab] = oab_ref[...]
    for p in range(n_pair):
        cs = slice(p * LANES, (p + 1) * LANES)
        on = _head_norm_pair(oi_ref[:, cs], head0, 1.0 / hd) * rng_ref[:, cs]
        mix_ref[:, w_ab + p * LANES:w_ab + (p + 1) * LANES] = (_silu(g_ref[:, cs]) * on).astype(BF16)


def _mix_sample_c(proj, row0, n_seq, ts, oab, state, lw):
    w_a, w_b, w_c = lw["w_a"], lw["w_b"], lw["w_c"]
    hd = LANES // 2
    n_rows = n_seq * ts
    rows = 64 if n_rows % 64 == 0 else n_rows
    assert SUBLANES % ts == 0 and rows % SUBLANES == 0 and row0 % rows == 0
    seq_per = rows // ts
    c0 = 2 * w_a + 2 * w_b
    assert c0 % w_c == 0
    cb = c0 // w_c
    rb = row0 // rows
    log_g = jnp.log1p(-jnp.exp2(-5.0 - jnp.arange(NH_C, dtype=F32)))
    r = jnp.arange(rows)
    step = (r % ts).astype(F32)
    diff = step[:, None] - step[None, :]
    live = jnp.logical_and((r[:, None] // ts) == (r[None, :] // ts), diff >= 0)
    dmat = jnp.where(live[None], jnp.exp(jnp.where(live, diff, 0.0)[None] * log_g[:, None, None]), 0.0)
    xi = jnp.repeat(jnp.exp((step + 1.0)[None, :] * log_g[:, None]).T, hd, axis=1)
    zeta = jnp.repeat(jnp.exp((ts - 1.0 - step)[None, :] * log_g[:, None]).T, hd, axis=1)
    gcol = jnp.broadcast_to(jnp.repeat(jnp.exp(ts * log_g), hd)[:, None], (w_c, hd))
    pos = PAST_LEN + (r % ts).astype(jnp.int32)
    cos, sin = _rope_tables(pos)
    tile_m = jnp.tile(jnp.eye(hd, dtype=F32), (1, NH_C))
    fold_m = jnp.tile(jnp.eye(hd, dtype=BF16), (NH_C, 1))
    full = lambda shape: pl.BlockSpec(shape, lambda i: (0,) * len(shape))
    col = lambda j: pl.BlockSpec((rows, w_c), lambda i: (rb + i, cb + j))
    kern = functools.partial(_mix_sample_c_kernel, rows=rows, ts=ts, w_c=w_c, w_ab=w_a + w_b)
    return pl.pallas_call(
        kern,
        grid=(n_rows // rows,),
        in_specs=[
            col(0), col(1), col(2), col(3),
            pl.BlockSpec((rows, w_a + w_b), lambda i: (i, 0)),
            pl.BlockSpec((seq_per, NH_C, hd, hd), lambda i: (i, 0, 0, 0)),
            full((rows, LANES)), full((rows, LANES)),
            full((NH_C, rows, rows)), full((rows, w_c)), full((rows, w_c)), full((w_c, hd)), full((1, w_c)),
            full((hd, w_c)), full((w_c, hd)),
        ],
        out_specs=[
            pl.BlockSpec((rows, w_a + w_b + w_c), lambda i: (i, 0)),
            pl.BlockSpec((seq_per, NH_C, hd, hd), lambda i: (i, 0, 0, 0)),
        ],
        out_shape=[
            jax.ShapeDtypeStruct((n_rows, w_a + w_b + w_c), BF16),
            jax.ShapeDtypeStruct(state.shape, F32),
        ],
        scratch_shapes=[pltpu.VMEM((rows, w_c), F32)] * 4,
        compiler_params=_cparams("arbitrary"),
        name="mix_sample_c",
    )(proj, proj, proj, proj, oab, state, cos, sin, dmat, xi, zeta, gcol, lw["ret_g"], tile_m, fold_m)


def _route(logits):
    shape = logits.shape
    lane = lax.broadcasted_iota(jnp.int32, shape, 1)
    neg = jnp.full(shape, -jnp.inf, F32)
    big = jnp.full(shape, LANES, jnp.int32)
    is_g = lane < N_GROUPS
    gl = jnp.where(is_g, logits, neg)
    gmax = jnp.max(gl, axis=-1, keepdims=True)
    p_top = 1.0 / jnp.sum(jnp.exp(gl - gmax), axis=-1, keepdims=True)
    g_sel = jnp.min(jnp.where(gl == gmax, lane, big), axis=-1, keepdims=True)
    lo = N_GROUPS + g_sel * EXPERTS_PER_GROUP
    in_grp = jnp.logical_and(lane >= lo, lane < lo + EXPERTS_PER_GROUP)
    el = jnp.where(in_grp, logits, neg)
    v1 = jnp.max(el, axis=-1, keepdims=True)
    i1 = jnp.min(jnp.where(el == v1, lane, big), axis=-1, keepdims=True)
    el2 = jnp.where(lane == i1, neg, el)
    v2 = jnp.max(el2, axis=-1, keepdims=True)
    i2 = jnp.min(jnp.where(el2 == v2, lane, big), axis=-1, keepdims=True)
    e2 = jnp.exp(v2 - v1)
    den = 1.0 + e2
    g1 = p_top * (1.0 / den)
    g2 = p_top * (e2 / den)
    return i1 - N_GROUPS, i2 - N_GROUPS, g1, g2


def _out_proj_kernel(xp_ref, xs_ref, mixp_ref, mixs_ref, w_ref, g_ref, wr_ref, br_ref,
                     xmid_ref, xn_ref, rt_ref, rg_ref, counts_ref, carry_ref, *, n_prompt_tiles):
    i = pl.program_id(0)

    @pl.when(i == 0)
    def _():
        carry_ref[...] = jnp.zeros_like(carry_ref)

    is_prompt = i < n_prompt_tiles
    x = jnp.where(is_prompt, xp_ref[...], xs_ref[...])
    mix = jnp.where(is_prompt, mixp_ref[...], mixs_ref[...])
    xm = x + jnp.dot(mix, w_ref[...], preferred_element_type=F32)
    xmid_ref[...] = xm
    xn = _rms(xm, g_ref[...])
    xn_ref[...] = xn
    logits = jnp.dot(xn, wr_ref[...], preferred_element_type=F32, precision=HIGHEST) + br_ref[...]
    e1, e2, g1, g2 = _route(logits)
    tm = logits.shape[0]
    lane = lax.broadcasted_iota(jnp.int32, logits.shape, 1)
    oh1 = lane == e1
    oh2 = lane == e2
    onehot = jnp.logical_or(oh1, oh2).astype(BF16)
    earlier = (lax.broadcasted_iota(jnp.int32, (tm, tm), 1)
               < lax.broadcasted_iota(jnp.int32, (tm, tm), 0)).astype(BF16)
    seen = carry_ref[0:1, :] + jnp.dot(earlier, onehot, preferred_element_type=F32)
    rank1 = jnp.sum(jnp.where(oh1, seen, 0.0), axis=-1, keepdims=True).astype(jnp.int32)
    rank2 = jnp.sum(jnp.where(oh2, seen, 0.0), axis=-1, keepdims=True).astype(jnp.int32)
    total = carry_ref[0:1, :] + jnp.sum(onehot.astype(F32), axis=0, keepdims=True)
    carry_ref[...] = jnp.broadcast_to(total, carry_ref.shape)
    counts_ref[...] = jnp.broadcast_to(total, counts_ref.shape)
    zi = jnp.zeros(logits.shape, jnp.int32)
    table = jnp.where(lane == 0, e1, jnp.where(lane == 1, e2, jnp.where(
        lane == 2, rank1, jnp.where(lane == 3, rank2, zi))))
    rt_ref[...] = table.T[0:SUBLANES, :]
    rg_ref[...] = jnp.where(lane == 0, g1, jnp.where(lane == 1, g2, 0.0))


def _out_proj(x_p, x_s, xs_row0, mix_p, mix_s, w_out, g_ffn, w_route, b_route):
    n_p, d_mix = mix_p.shape
    n_s = mix_s.shape[0]
    d = x_p.shape[1]
    n = n_p + n_s
    tm = math.gcd(_row_tile(n_p), _row_tile(n_s))
    assert tm % LANES == 0 and xs_row0 % tm == 0
    tp = n_p // tm
    ts_last = n_s // tm - 1
    s0 = xs_row0 // tm
    kern = functools.partial(_out_proj_kernel, n_prompt_tiles=tp)
    return pl.pallas_call(
        kern,
        grid=(n // tm,),
        in_specs=[
            pl.BlockSpec((tm, d), lambda i: (jnp.minimum(i, tp - 1), 0)),
            pl.BlockSpec((tm, d), lambda i: (s0 + jnp.clip(i - tp, 0, ts_last), 0)),
            pl.BlockSpec((tm, d_mix), lambda i: (jnp.minimum(i, tp - 1), 0)),
            pl.BlockSpec((tm, d_mix), lambda i: (jnp.clip(i - tp, 0, ts_last), 0)),
            pl.BlockSpec((d_mix, d), lambda i: (0, 0)),
            pl.BlockSpec((1, d), lambda i: (0, 0)),
            pl.BlockSpec((d, LANES), lambda i: (0, 0)),
            pl.BlockSpec((1, LANES), lambda i: (0, 0)),
        ],
        out_specs=[
            pl.BlockSpec((tm, d), lambda i: (i, 0)),
            pl.BlockSpec((tm, d), lambda i: (i, 0)),
            pl.BlockSpec((SUBLANES, tm), lambda i: (0, i)),
            pl.BlockSpec((tm, LANES), lambda i: (i, 0)),
            pl.BlockSpec((SUBLANES, LANES), lambda i: (0, 0)),
        ],
        out_shape=[
            jax.ShapeDtypeStruct((n, d), F32),
            jax.ShapeDtypeStruct((n, d), F32),
            jax.ShapeDtypeStruct((SUBLANES, n), jnp.int32),
            jax.ShapeDtypeStruct((n, LANES), F32),
            jax.ShapeDtypeStruct((SUBLANES, LANES), F32),
        ],
        scratch_shapes=[pltpu.VMEM((SUBLANES, LANES), F32)],
        compiler_params=_cparams("arbitrary"),
        name="out_proj_route",
    )(x_p, x_s, mix_p, mix_s, w_out, g_ffn, w_route, b_route)


SLOT_TILE = 256
DMA_UNROLL = 8


def _slot_plan(counts, n_tokens):
    n_slots = 2 * n_tokens + N_EXPERTS * SLOT_TILE
    n_slots = -(-n_slots // SLOT_TILE) * SLOT_TILE
    n_tiles = n_slots // SLOT_TILE
    padded = (counts + (SLOT_TILE - 1)) // SLOT_TILE * SLOT_TILE
    ends = jnp.cumsum(padded)
    offs = ends - padded
    total = ends[-1]
    tile_start = jnp.arange(n_tiles, dtype=jnp.int32) * SLOT_TILE
    probe = jnp.minimum(tile_start, total - 1)
    tile_expert = jnp.sum((ends[None, :] <= probe[:, None]).astype(jnp.int32), axis=1)
    tile_expert = jnp.minimum(tile_expert, N_EXPERTS - 1)
    n_valid = (total // SLOT_TILE).reshape(1)
    last_tile = jnp.where(padded > 0, ends - SLOT_TILE, 0)
    return n_slots, offs, tile_expert, n_valid, last_tile, (padded > 0).astype(jnp.int32)


def _row_copy(src_ref, src_row, dst_ref, dst_row, sem):
    return pltpu.make_async_copy(src_ref.at[pl.ds(src_row, 1), :], dst_ref.at[pl.ds(dst_row, 1), :], sem)


def _dispatch_kernel(last_ref, used_ref, nv_ref, pos1_ref, pos2_ref, xn_ref, xs_ref, zero_ref, zsem, sem):
    tm = xn_ref.shape[0]
    n_tiles = xs_ref.shape[0] // SLOT_TILE

    @pl.when(pl.program_id(0) == 0)
    def _():
        zero_ref[...] = jnp.zeros_like(zero_ref)
        tile = lambda row: pltpu.make_async_copy(
            zero_ref, xs_ref.at[pl.ds(pl.multiple_of(row, SLOT_TILE), SLOT_TILE), :], zsem)
        for e in range(N_EXPERTS):
            @pl.when(used_ref[e] != 0)
            def _():
                tile(last_ref[e]).start()
        lax.fori_loop(nv_ref[0], n_tiles, lambda j, c: (tile(j * SLOT_TILE).start(), c)[1], 0)
        for e in range(N_EXPERTS):
            @pl.when(used_ref[e] != 0)
            def _():
                tile(0).wait()
        lax.fori_loop(nv_ref[0], n_tiles, lambda j, c: (tile(0).wait(), c)[1], 0)

    def issue(t, c):
        _row_copy(xn_ref, t, xs_ref, pos1_ref[t], sem).start()
        _row_copy(xn_ref, t, xs_ref, pos2_ref[t], sem).start()
        return c

    lax.fori_loop(0, tm, issue, 0, unroll=DMA_UNROLL)

    def drain(t, c):
        _row_copy(xn_ref, 0, xs_ref, 0, sem).wait()
        _row_copy(xn_ref, 0, xs_ref, 0, sem).wait()
        return c

    lax.fori_loop(0, tm, drain, 0, unroll=DMA_UNROLL)


def _dispatch(xn, pos1, pos2, n_slots, last_tile, used, n_valid):
    n, d = xn.shape
    tm = _row_tile(n)
    smem_rows = lambda: pl.BlockSpec((tm,), lambda i, *_: (i,), memory_space=pltpu.SMEM)
    grid_spec = pltpu.PrefetchScalarGridSpec(
        num_scalar_prefetch=3,
        grid=(n // tm,),
        in_specs=[smem_rows(), smem_rows(), pl.BlockSpec((tm, d), lambda i, *_: (i, 0))],
        out_specs=pl.BlockSpec(memory_space=pl.ANY),
        scratch_shapes=[pltpu.VMEM((SLOT_TILE, d), xn.dtype), pltpu.SemaphoreType.DMA, pltpu.SemaphoreType.DMA],
    )
    return pl.pallas_call(
        _dispatch_kernel,
        grid_spec=grid_spec,
        out_shape=jax.ShapeDtypeStruct((n_slots, d), xn.dtype),
        compiler_params=_cparams("arbitrary"),
        name="moe_dispatch",
    )(last_tile, used, n_valid, pos1, pos2, xn)


def _experts_kernel(te_ref, nv_ref, xs_ref, w1_ref, w3_ref, w2_ref, ys_ref, w1b_ref, w3b_ref, w2b_ref):
    j = pl.program_id(0)
    fresh = jnp.logical_or(j == 0, te_ref[j] != te_ref[jnp.maximum(j - 1, 0)])

    @pl.when(fresh)
    def _():
        w1b_ref[...] = w1_ref[0].astype(BF16)
        w3b_ref[...] = w3_ref[0].astype(BF16)
        w2b_ref[...] = w2_ref[0].astype(BF16)

    @pl.when(j < nv_ref[0])
    def _():
        x = xs_ref[...].astype(BF16)
        h1 = jnp.dot(x, w1b_ref[...], preferred_element_type=F32)
        h3 = jnp.dot(x, w3b_ref[...], preferred_element_type=F32)
        h = (_silu(h1) * h3).astype(BF16)
        ys_ref[...] = jnp.dot(h, w2b_ref[...], preferred_element_type=F32)

    @pl.when(j >= nv_ref[0])
    def _():
        ys_ref[...] = jnp.zeros_like(ys_ref)


def _experts(xs, tile_expert, n_valid, w1, w3, w2):
    n_slots, d = xs.shape
    _, _, d_e = w1.shape
    tile = lambda j, te, nv: (jnp.minimum(j, nv[0] - 1), 0)
    expert = lambda j, te, nv: (te[j], 0, 0)
    grid_spec = pltpu.PrefetchScalarGridSpec(
        num_scalar_prefetch=2,
        grid=(n_slots // SLOT_TILE,),
        in_specs=[
            pl.BlockSpec((SLOT_TILE, d), tile),
            pl.BlockSpec((1, d, d_e), expert),
            pl.BlockSpec((1, d, d_e), expert),
            pl.BlockSpec((1, d_e, d), expert),
        ],
        out_specs=pl.BlockSpec((SLOT_TILE, d), lambda j, te, nv: (j, 0)),
        scratch_shapes=[pltpu.VMEM((d, d_e), BF16), pltpu.VMEM((d, d_e), BF16), pltpu.VMEM((d_e, d), BF16)],
    )
    return pl.pallas_call(
        _experts_kernel,
        grid_spec=grid_spec,
        out_shape=jax.ShapeDtypeStruct((n_slots, d), F32),
        compiler_params=_cparams("arbitrary"),
        name="moe_experts",
    )(tile_expert, n_valid, xs, w1, w3, w2)


def _combine_kernel(pos1_ref, pos2_ref, xmid_ref, rg_ref, gfin_ref, ys_ref, o_ref, y1_ref, y2_ref, sem,
                    *, final_norm):
    tm = xmid_ref.shape[0]

    def issue(t, c):
        _row_copy(ys_ref, pos1_ref[t], y1_ref, t, sem).start()
        _row_copy(ys_ref, pos2_ref[t], y2_ref, t, sem).start()
        return c

    lax.fori_loop(0, tm, issue, 0, unroll=DMA_UNROLL)

    def drain(t, c):
        _row_copy(ys_ref, 0, y1_ref, 0, sem).wait()
        _row_copy(ys_ref, 0, y2_ref, 0, sem).wait()
        return c

    lax.fori_loop(0, tm, drain, 0, unroll=DMA_UNROLL)
    rg = rg_ref[...]
    out = xmid_ref[...] + (rg[:, 0:1] * y1_ref[...] + rg[:, 1:2] * y2_ref[...])
    o_ref[...] = _rms(out, gfin_ref[...]) if final_norm else out


def _combine(xmid, rg, pos1, pos2, ys, g_final, final_norm, row0, n_rows):
    d = xmid.shape[1]
    tm = math.gcd(_row_tile(n_rows), _row_tile(row0)) if row0 else _row_tile(n_rows)
    t0 = row0 // tm
    smem_rows = lambda: pl.BlockSpec((tm,), lambda i: (t0 + i,), memory_space=pltpu.SMEM)
    kern = functools.partial(_combine_kernel, final_norm=final_norm)
    return pl.pallas_call(
        kern,
        grid=(n_rows // tm,),
        in_specs=[smem_rows(), smem_rows(),
                  pl.BlockSpec((tm, d), lambda i: (t0 + i, 0)),
                  pl.BlockSpec((tm, LANES), lambda i: (t0 + i, 0)),
                  pl.BlockSpec((1, d), lambda i: (0, 0)),
                  pl.BlockSpec(memory_space=pl.ANY)],
        out_specs=pl.BlockSpec((tm, d), lambda i: (i, 0)),
        out_shape=jax.ShapeDtypeStruct((n_rows, d), F32),
        scratch_shapes=[pltpu.VMEM((tm, d), F32), pltpu.VMEM((tm, d), F32), pltpu.SemaphoreType.DMA],
        compiler_params=_cparams("arbitrary"),
        name="moe_combine",
    )(pos1, pos2, xmid, rg, g_final, ys)


def _moe_routed(xmid, xn, rt, rg, counts, w1, w3, w2, g_final, final_norm, out_splits):
    n = xmid.shape[0]
    n_slots, offs, tile_expert, n_valid, last_tile, used = _slot_plan(
        counts[0, :N_EXPERTS].astype(jnp.int32), n)
    experts = jnp.arange(N_EXPERTS, dtype=jnp.int32)[:, None]
    base = lambda e: jnp.sum(jnp.where(experts == e[None, :], offs[:, None], 0), axis=0)
    pos1 = base(rt[0]) + rt[2]
    pos2 = base(rt[1]) + rt[3]
    xs = _dispatch(xn, pos1, pos2, n_slots, last_tile, used, n_valid)
    ys = _experts(xs, tile_expert, n_valid, w1, w3, w2)
    return [_combine(xmid, rg, pos1, pos2, ys, g_final, final_norm, r0, nr) for r0, nr in out_splits]


def kernel(x_prompt, x_sample, state_lru_h, state_lru_conv, state_conv, state_ret, norm_mix_g, norm_ffn_g, final_norm_g, w_in, lru_conv_w, lru_conv_b, lru_wa, lru_ba, lru_wx, lru_bx, lru_lambda, cm_dw_w, cm_dw_b, cm_ln_g, cm_ln_b, ret_norm_g, w_out, moe_w_group, moe_b_group, moe_w_expert, moe_b_expert, moe_w1, moe_w3, moe_w2):
    bp, tp, d = x_prompt.shape
    bs, ts, _ = x_sample.shape
    depth = w_in.shape[0]
    w_a = lru_lambda.shape[1]
    w_b = cm_dw_b.shape[1]
    w_c = ret_norm_g.shape[1]
    n_p, n_s = bp * tp, bs * ts
    row = lambda v: v.reshape(1, -1)

    x_p, x_s, xs_row0 = x_prompt.reshape(n_p, d), x_sample.reshape(n_s, d), 0
    outs = {k: [] for k in ("p_h", "p_ca", "p_cb", "p_s", "s_h", "s_ca", "s_cb", "s_s")}
    for l in range(depth):
        lw = dict(
            w_a=w_a, w_b=w_b, w_c=w_c,
            conv_a_w=lru_conv_w[l], conv_a_b=row(lru_conv_b[l]),
            wg=_blockdiag_gate_weights(lru_wa[l], lru_wx[l]).astype(BF16),
            bg=row(jnp.concatenate([lru_ba[l], lru_bx[l]])), lam=row(lru_lambda[l]),
            conv_b_w=cm_dw_w[l], conv_b_b=row(cm_dw_b[l]), ln_g=row(cm_ln_g[l]), ln_b=row(cm_ln_b[l]),
            ret_g=row(ret_norm_g[l]),
        )
        w_in_l = w_in[l].astype(BF16)
        if x_p is x_s:
            proj_p = proj_s = _in_proj(x_p, row(norm_mix_g[l]), w_in_l)
        else:
            proj_p = _in_proj(x_p, row(norm_mix_g[l]), w_in_l)
            proj_s = _in_proj(x_s, row(norm_mix_g[l]), w_in_l)
        mix_p, h_p, ca_p, cb_p, s_p = _mix_prompt(proj_p, bp, tp, lw)
        outs["p_h"].append(h_p[:, 0, :])
        outs["p_ca"].append(ca_p[:, SUBLANES - (CONV_A - 1):, :])
        outs["p_cb"].append(cb_p[:, 4 * SUBLANES - (CONV_B - 1):, :])
        outs["p_s"].append(_unpair_state(s_p))
        p_t = proj_s[xs_row0:xs_row0 + n_s, :2 * w_a + 2 * w_b].reshape(bs, ts, -1).transpose(1, 0, 2)
        oab_t, h_s, ca_t, cb_t = _mix_sample_ab(
            p_t, state_lru_h[l], state_lru_conv[l].transpose(1, 0, 2), state_conv[l].transpose(1, 0, 2), lw)
        oab = oab_t.transpose(1, 0, 2).reshape(n_s, w_a + w_b)
        mix_s, s_s = _mix_sample_c(proj_s, xs_row0, bs, ts, oab, state_ret[l], lw)
        outs["s_h"].append(h_s)
        outs["s_ca"].append(ca_t.transpose(1, 0, 2))
        outs["s_cb"].append(cb_t.transpose(1, 0, 2))
        outs["s_s"].append(s_s)
        w_route = jnp.zeros((d, LANES), F32)
        w_route = w_route.at[:, :N_GROUPS].set(moe_w_group[l]).at[:, N_GROUPS:N_GROUPS + N_EXPERTS].set(moe_w_expert[l])
        b_route = jnp.zeros((1, LANES), F32)
        b_route = b_route.at[0, :N_GROUPS].set(moe_b_group[l]).at[0, N_GROUPS:N_GROUPS + N_EXPERTS].set(moe_b_expert[l])
        xmid, xn, rt, rg, counts = _out_proj(x_p, x_s, xs_row0, mix_p, mix_s, w_out[l].astype(BF16),
                                             row(norm_ffn_g[l]), w_route, b_route)
        last = l == depth - 1
        splits = [(0, n_p), (n_p, n_s)] if last else [(0, n_p + n_s)]
        res = _moe_routed(xmid, xn, rt, rg, counts, moe_w1[l], moe_w3[l], moe_w2[l],
                          row(final_norm_g), last, splits)
        if not last:
            x_p = x_s = res[0]
            xs_row0 = n_p
    y_prompt = res[0].reshape(bp, tp, d)
    y_sample = res[1].reshape(bs, ts, d)
    st = lambda k: jnp.stack(outs[k])
    return (y_prompt, y_sample, st("p_h"), st("p_ca"), st("p_cb"), st("p_s"),
            st("s_h"), st("s_ca"), st("s_cb"), st("s_s"))
```

```python
import functools
import math

import jax
import jax.numpy as jnp
from jax import lax
from jax.experimental import pallas as pl
from jax.experimental.pallas import tpu as pltpu

PAST_LEN = 16384
LRU_C = 8.0
CONV_A = 4
CONV_B = 31
NH_C = 8
RET_CHUNK = 128
ROPE_BASE = 10000.0
N_GROUPS = 4
EXPERTS_PER_GROUP = 8
N_EXPERTS = N_GROUPS * EXPERTS_PER_GROUP
EPS = 1e-6

LANES = 128
SUBLANES = 8
VMEM_LIMIT_BYTES = 56 * 1024 * 1024

F32 = jnp.float32
BF16 = jnp.bfloat16


def _cparams(*sem):
    return pltpu.CompilerParams(dimension_semantics=sem, vmem_limit_bytes=VMEM_LIMIT_BYTES)


def _row_tile(n):
    for t in (512, 256, 128, 64, 32, 16, 8):
        if n % t == 0:
            return t
    raise ValueError(f"row count {n} must be a multiple of 8")


def _resident(shape, index):
    return pl.BlockSpec(shape, lambda *_: index, pipeline_mode=pl.Buffered(1))


def _rms(x, g):
    ms = jnp.mean(x * x, axis=-1, keepdims=True)
    return x * lax.rsqrt(ms + EPS) * g


def _sigmoid(x):
    return jax.nn.sigmoid(x)


def _silu(x):
    return x * jax.nn.sigmoid(x)


def _layernorm(x, g, b):
    mu = jnp.mean(x, axis=-1, keepdims=True)
    xc = x - mu
    var = jnp.mean(xc * xc, axis=-1, keepdims=True)
    return xc * lax.rsqrt(var + EPS) * g + b


def _rope_pair(x, cos, sin_signed, first_half):
    partner = jnp.where(first_half, pltpu.roll(x, 96, 1), pltpu.roll(x, 32, 1))
    return x * cos + partner * sin_signed


def _head_norm_pair(o, head0, inv_d):
    zero = jnp.zeros_like(o)
    s0 = jnp.sum(jnp.where(head0, o, zero), axis=-1, keepdims=True)
    s1 = jnp.sum(jnp.where(head0, zero, o), axis=-1, keepdims=True)
    oc = o - jnp.where(head0, s0, s1) * inv_d
    q = oc * oc
    v0 = jnp.sum(jnp.where(head0, q, zero), axis=-1, keepdims=True)
    v1 = jnp.sum(jnp.where(head0, zero, q), axis=-1, keepdims=True)
    return oc * lax.rsqrt(jnp.where(head0, v0, v1) * inv_d + EPS)


def _retention_intra_pair(qb, kb, vb, dmat0, dmat1, head0):
    outs = []
    for j, dm in ((0, dmat0), (1, dmat1)):
        keep = head0 if j == 0 else jnp.logical_not(head0)
        qm = jnp.where(keep, qb, jnp.zeros_like(qb))
        s = lax.dot_general(qm, kb, (((1,), (1,)), ((), ())), preferred_element_type=F32)
        inner = (s * dm).astype(BF16)
        outs.append(jnp.dot(inner, vb, preferred_element_type=F32))
    return jnp.where(head0, outs[0], outs[1])


def _lru_gates(xa, wg, bg, lam):
    w_a = xa.shape[-1]
    gates = jnp.dot(xa.astype(BF16), wg, preferred_element_type=F32) + bg
    r = _sigmoid(gates[:, :w_a])
    i = _sigmoid(gates[:, w_a:])
    z = -lam
    softplus = jnp.maximum(z, 0.0) + jnp.log1p(jnp.exp(-jnp.abs(z)))
    log_a = -LRU_C * r * softplus
    a = jnp.exp(log_a)
    bt = jnp.sqrt(-jnp.tanh(log_a) * (a * a + 1.0)) * (i * xa)
    return a, bt


EXPERT_ROW0 = SUBLANES
assert EXPERTS_PER_GROUP == SUBLANES and N_GROUPS <= SUBLANES and EXPERT_ROW0 + N_EXPERTS <= LANES


def _split_bf16(x):
    hi = x.astype(BF16)
    return hi, (x - hi.astype(F32)).astype(BF16)


def _lane_tile(x, width):
    return x if width == x.shape[1] else jnp.concatenate([x] * (width // x.shape[1]), axis=1)


def _route_weights(w_group, b_group, w_expert, b_expert):
    d = w_group.shape[0]
    w = jnp.zeros((LANES, d), F32).at[:N_GROUPS].set(w_group.T).at[EXPERT_ROW0:EXPERT_ROW0 + N_EXPERTS].set(w_expert.T)
    b = jnp.zeros((LANES,), F32).at[:N_GROUPS].set(b_group).at[EXPERT_ROW0:EXPERT_ROW0 + N_EXPERTS].set(b_expert)
    hi, lo = _split_bf16(w)
    return hi, lo, jnp.broadcast_to(b[:, None], (LANES, LANES))


def _ffn_norm_route(xm, g_ffn, wr_hi, wr_lo, b_route, carry_ref, xn_ref, rt_ref, rg_ref, counts_ref):
    xn = _rms(xm, g_ffn)
    xn_ref[...] = xn
    tm = xn.shape[0]
    xh, xl = _split_bf16(xn)
    dot_t = lambda w, x: lax.dot_general(w, x, (((1,), (1,)), ((), ())), preferred_element_type=F32)
    lt = dot_t(wr_hi, xh) + (dot_t(wr_hi, xl) + dot_t(wr_lo, xh)) + _lane_tile(b_route, tm)
    row8 = lax.broadcasted_iota(jnp.int32, (SUBLANES, tm), 0)
    neg = jnp.full((SUBLANES, tm), -jnp.inf, F32)
    big = jnp.full((SUBLANES, tm), LANES, jnp.int32)
    first = lambda hit: jnp.min(jnp.where(hit, row8, big), axis=0, keepdims=True)
    gl = jnp.where(row8 < N_GROUPS, lt[0:SUBLANES, :], neg)
    gmax = jnp.max(gl, axis=0, keepdims=True)
    p_top = 1.0 / jnp.sum(jnp.exp(gl - gmax), axis=0, keepdims=True)
    g_sel = first(gl == gmax)
    slab = lambda g: lt[EXPERT_ROW0 + g * SUBLANES:EXPERT_ROW0 + (g + 1) * SUBLANES, :]
    el = slab(N_GROUPS - 1)
    for g in reversed(range(N_GROUPS - 1)):
        el = jnp.where(g_sel == g, slab(g), el)
    v1 = jnp.max(el, axis=0, keepdims=True)
    i1 = first(el == v1)
    el2 = jnp.where(row8 == i1, neg, el)
    v2 = jnp.max(el2, axis=0, keepdims=True)
    i2 = first(el2 == v2)
    ex = jnp.exp(v2 - v1)
    den = 1.0 + ex
    g1 = p_top * (1.0 / den)
    g2 = p_top * (ex / den)
    e1 = g_sel * EXPERTS_PER_GROUP + i1
    e2 = g_sel * EXPERTS_PER_GROUP + i2
    rowe = lax.broadcasted_iota(jnp.int32, (LANES, tm), 0)
    oh1 = rowe == e1
    oh2 = rowe == e2
    onehot = jnp.logical_or(oh1, oh2).astype(BF16)
    earlier = (lax.broadcasted_iota(jnp.int32, (tm, tm), 0)
               < lax.broadcasted_iota(jnp.int32, (tm, tm), 1)).astype(BF16)
    seen = _lane_tile(carry_ref[...], tm) + jnp.dot(onehot, earlier, preferred_element_type=F32)
    rank1 = jnp.sum(jnp.where(oh1, seen, 0.0), axis=0, keepdims=True).astype(jnp.int32)
    rank2 = jnp.sum(jnp.where(oh2, seen, 0.0), axis=0, keepdims=True).astype(jnp.int32)
    total = carry_ref[...] + jnp.sum(onehot.astype(F32), axis=1, keepdims=True)
    carry_ref[...] = total
    counts_ref[...] = total
    zi = jnp.zeros((SUBLANES, tm), jnp.int32)
    rt_ref[...] = jnp.where(row8 == 0, e1, jnp.where(row8 == 1, e2, jnp.where(
        row8 == 2, rank1, jnp.where(row8 == 3, rank2, zi))))
    rg_ref[...] = jnp.where(rowe == 0, g1, jnp.where(rowe == 1, g2, 0.0)).T


CONV_B_ROWS = 64


def _prompt_layer_kernel(x_ref, gmix_ref, win_ref, cos_ref, sin_ref, h0_ref, bufa0_ref, bufb0_ref, s0_ref,
                         caw_ref, cab_ref, wg_ref, bg_ref, lam_ref,
                         cbw_ref, cbb_ref, lng_ref, lnb_ref, rng_ref,
                         dmat_ref, xi_ref, zeta_ref, gmat_ref,
                         wout_ref, gffn_ref, wrh_ref, wrl_ref, br_ref,
                         xmid_ref, xn_ref, rt_ref, rg_ref, counts_ref,
                         hout_ref, bufa_out_ref, bufb_out_ref, sout_ref,
                         winb_ref, woutb_ref, u_ref, proj_ref, mix_ref,
                         xpa_ref, xpb_ref, sh_ref, a_ref, b_ref, hseq_ref, hcar_ref, s_ref, carry_ref,
                         *, tc, w_a, w_b, w_c, chunk):
    t = pl.program_id(1)
    n_t = pl.num_programs(1)
    pa = SUBLANES
    pb = 4 * SUBLANES

    @pl.when(jnp.logical_and(pl.program_id(0) == 0, t == 0))
    def _():
        winb_ref[...] = win_ref[0].astype(BF16)
        woutb_ref[...] = wout_ref[0].astype(BF16)
        carry_ref[...] = jnp.zeros_like(carry_ref)

    @pl.when(t == 0)
    def _():
        xpa_ref[0:pa, :] = bufa0_ref[0]
        xpb_ref[0:pb, :] = bufb0_ref[0]
        hcar_ref[...] = h0_ref[0]
        s_ref[...] = s0_ref[0]

    u = _rms(x_ref[...], gmix_ref[...]).astype(BF16)
    u_ref[...] = u

    def in_proj(c0, c1):
        for c in range(c0, c1, 4 * LANES):
            ce = min(c + 4 * LANES, c1)
            proj_ref[:, c:ce] = jnp.dot(u_ref[...], winb_ref[:, c:ce], preferred_element_type=F32)

    in_proj(0, 2 * w_a)

    xpa_ref[pa:pa + tc, :] = proj_ref[:, w_a:2 * w_a]
    xa = cab_ref[...] + caw_ref[0:1, :] * xpa_ref[pa - 3:pa - 3 + tc, :]
    for j in range(1, CONV_A):
        xa = xa + caw_ref[j:j + 1, :] * xpa_ref[pa - 3 + j:pa - 3 + j + tc, :]
    a, bt = _lru_gates(xa, wg_ref[...], bg_ref[...], lam_ref[...])
    a_ref[...] = a
    b_ref[...] = bt
    row8 = lax.broadcasted_iota(jnp.int32, (SUBLANES, w_a), 0)

    def scan_block(i, hprev):
        r0 = pl.multiple_of(i * SUBLANES, SUBLANES)
        ab = a_ref[pl.ds(r0, SUBLANES), :]
        bb = b_ref[pl.ds(r0, SUBLANES), :]
        for k in (1, 2, 4):
            a_sh = jnp.where(row8 >= k, pltpu.roll(ab, k, 0), 1.0)
            b_sh = jnp.where(row8 >= k, pltpu.roll(bb, k, 0), 0.0)
            bb = ab * b_sh + bb
            ab = ab * a_sh
        h = ab * hprev + bb
        hseq_ref[pl.ds(r0, SUBLANES), :] = h
        return jnp.broadcast_to(h[SUBLANES - 1:SUBLANES, :], (SUBLANES, w_a))

    hcar_ref[...] = lax.fori_loop(0, tc // SUBLANES, scan_block, hcar_ref[...])
    in_proj(2 * w_a, proj_ref.shape[1])
    mix_ref[:, 0:w_a] = (jax.nn.gelu(proj_ref[:, 0:w_a]) * hseq_ref[...]).astype(BF16)
    xpa_ref[0:pa, :] = xpa_ref[tc:tc + pa, :]

    b0 = 2 * w_a
    xpb_ref[pb:pb + tc, :] = proj_ref[:, b0:b0 + w_b] * _sigmoid(proj_ref[:, b0 + w_b:b0 + 2 * w_b])
    first = pb - (CONV_B - 1)
    for r in range(SUBLANES):
        span = tc + (CONV_B - 1 - r) // SUBLANES * SUBLANES
        sh_ref[r, 0:span, :] = xpb_ref[first + r:first + r + span, :]
    for blk in range(tc // CONV_B_ROWS):
        r0 = blk * CONV_B_ROWS
        acc = jnp.broadcast_to(cbb_ref[...], (CONV_B_ROWS, w_b))
        for j in range(CONV_B):
            r = j % SUBLANES
            acc = acc + cbw_ref[j:j + 1, :] * sh_ref[r, r0 + j - r:r0 + j - r + CONV_B_ROWS, :]
        out_b = _silu(_layernorm(acc, lng_ref[...], lnb_ref[...]))
        mix_ref[r0:r0 + CONV_B_ROWS, w_a:w_a + w_b] = out_b.astype(BF16)
    xpb_ref[0:pb, :] = xpb_ref[tc:tc + pb, :]

    c0 = 2 * w_a + 2 * w_b
    n_pair = w_c // LANES
    lane = lax.broadcasted_iota(jnp.int32, (chunk, LANES), 1)
    head0 = lane < (LANES // 2)
    first_half = (lane % (LANES // 2)) < (LANES // 4)
    rl = lax.broadcasted_iota(jnp.int32, (LANES, LANES), 0) < (LANES // 2)
    cl = lax.broadcasted_iota(jnp.int32, (LANES, LANES), 1) < (LANES // 2)
    same_head = rl == cl
    inv_d = 2.0 / LANES
    for c in range(tc // chunk):
        rows = slice(c * chunk, (c + 1) * chunk)
        cos = cos_ref[rows, :]
        sin = sin_ref[rows, :]
        for p in range(n_pair):
            cq = c0 + p * LANES
            q = _rope_pair(proj_ref[rows, cq:cq + LANES], cos, sin, first_half)
            k = _rope_pair(proj_ref[rows, cq + w_c:cq + w_c + LANES], cos, sin, first_half) * (2.0 / LANES) ** 0.5
            v = proj_ref[rows, cq + 2 * w_c:cq + 2 * w_c + LANES]
            g = proj_ref[rows, cq + 3 * w_c:cq + 3 * w_c + LANES]
            qb, kb, vb = q.astype(BF16), k.astype(BF16), v.astype(BF16)
            o = _retention_intra_pair(qb, kb, vb, dmat_ref[2 * p], dmat_ref[2 * p + 1], head0)
            s_old = s_ref[p]
            o = o + jnp.dot(qb, s_old.astype(BF16), preferred_element_type=F32) * xi_ref[p]
            kz = (k * zeta_ref[p]).astype(BF16)
            upd = lax.dot_general(kz, vb, (((0,), (0,)), ((), ())), preferred_element_type=F32)
            s_ref[p] = gmat_ref[p] * s_old + jnp.where(same_head, upd, 0.0)
            on = _head_norm_pair(o, head0, inv_d) * rng_ref[:, p * LANES:(p + 1) * LANES]
            mix_ref[rows, w_a + w_b + p * LANES:w_a + w_b + (p + 1) * LANES] = (_silu(g) * on).astype(BF16)

    xm = x_ref[...] + jnp.dot(mix_ref[...], woutb_ref[...], preferred_element_type=F32)
    xmid_ref[...] = xm
    _ffn_norm_route(xm, gffn_ref[...], wrh_ref[...], wrl_ref[...], br_ref[...], carry_ref,
                    xn_ref, rt_ref, rg_ref, counts_ref)

    @pl.when(t == n_t - 1)
    def _():
        hout_ref[0] = hcar_ref[...]
        bufa_out_ref[0] = xpa_ref[0:pa, :]
        bufb_out_ref[0] = xpb_ref[0:pb, :]
        sout_ref[0] = s_ref[...]


def _retention_tables(chunk):
    log_g = jnp.log1p(-jnp.exp2(-5.0 - jnp.arange(NH_C, dtype=F32)))
    idx = jnp.arange(chunk, dtype=F32)
    diff = idx[:, None] - idx[None, :]
    causal = diff >= 0
    dmat = jnp.where(causal[None], jnp.exp(jnp.where(causal, diff, 0.0)[None] * log_g[:, None, None]), 0.0)
    xi = jnp.exp((idx + 1.0)[None, :] * log_g[:, None])
    zeta = jnp.exp((chunk - 1.0 - idx)[None, :] * log_g[:, None])
    g_chunk = jnp.exp(chunk * log_g)
    return dmat, xi, zeta, g_chunk


def _pair_lanes(per_head):
    h, rows = per_head.shape
    half = LANES // 2
    x = jnp.broadcast_to(per_head.reshape(h // 2, 2, rows, 1), (h // 2, 2, rows, half))
    return x.transpose(0, 2, 1, 3).reshape(h // 2, rows, LANES)


def _rope_tables(pos):
    half = LANES // 4
    inv = ROPE_BASE ** (-jnp.arange(half, dtype=F32) / half)
    ang = pos.astype(F32)[:, None] * inv[None, :]
    cos, sin = jnp.cos(ang), jnp.sin(ang)
    return jnp.concatenate([cos] * 4, axis=1), jnp.concatenate([-sin, sin, -sin, sin], axis=1)


def _blockdiag_gate_weights(wa, wx):
    nh, hd, _ = wa.shape
    eye = jnp.eye(nh, dtype=wa.dtype)
    da = (eye[:, None, :, None] * wa[:, :, None, :]).reshape(nh * hd, nh * hd)
    dx = (eye[:, None, :, None] * wx[:, :, None, :]).reshape(nh * hd, nh * hd)
    return jnp.concatenate([da, dx], axis=1)


def _prompt_layer(x, n_seq, seq_len, layer, w_in, w_out, lw):
    w_a, w_b, w_c = lw["w_a"], lw["w_b"], lw["w_c"]
    n, d = x.shape
    d_in = w_in.shape[2]
    d_mix = w_a + w_b + w_c
    chunk = math.gcd(seq_len, RET_CHUNK)
    tc = 256 if seq_len % 256 == 0 else chunk
    assert tc % LANES == 0 and tc % CONV_B_ROWS == 0 and d_in % (4 * LANES) == 0
    n_t = seq_len // tc
    n_pair = w_c // LANES
    dmat, xi, zeta, g_chunk = _retention_tables(chunk)
    half = LANES // 2
    gm = jnp.zeros((n_pair, 2, half, 2, half), F32)
    gm = gm.at[:, 0, :, 0, :].set(g_chunk[0::2, None, None]).at[:, 1, :, 1, :].set(g_chunk[1::2, None, None])
    cos, sin = _rope_tables(jnp.arange(seq_len, dtype=jnp.int32))
    zeros = lambda *s: jnp.zeros(s, F32)
    full = lambda shape: _resident(shape, (0,) * len(shape))
    per_seq = lambda shape: pl.BlockSpec((1,) + shape, lambda b, t: (b,) + (0,) * len(shape))
    tile = lambda width: pl.BlockSpec((tc, width), lambda b, t: (b * n_t + t, 0))
    kern = functools.partial(_prompt_layer_kernel, tc=tc, w_a=w_a, w_b=w_b, w_c=w_c, chunk=chunk)
    return pl.pallas_call(
        kern,
        grid=(n_seq, n_t),
        in_specs=[
            tile(d), full((1, d)), _resident((1, d, d_in), (layer, 0, 0)),
            pl.BlockSpec((tc, LANES), lambda b, t: (t, 0)),
            pl.BlockSpec((tc, LANES), lambda b, t: (t, 0)),
            per_seq((SUBLANES, w_a)), per_seq((SUBLANES, w_a)), per_seq((4 * SUBLANES, w_b)),
            per_seq((n_pair, LANES, LANES)),
            full((CONV_A, w_a)), full((1, w_a)), full((w_a, 2 * w_a)), full((1, 2 * w_a)), full((1, w_a)),
            full((CONV_B, w_b)), full((1, w_b)), full((1, w_b)), full((1, w_b)), full((1, w_c)),
            full((NH_C, chunk, chunk)), full((n_pair, chunk, LANES)), full((n_pair, chunk, LANES)),
            full((n_pair, LANES, LANES)),
            _resident((1, d_mix, d), (layer, 0, 0)), full((1, d)),
            full((LANES, d)), full((LANES, d)), full((LANES, LANES)),
        ],
        out_specs=[
            tile(d), tile(d),
            pl.BlockSpec((SUBLANES, tc), lambda b, t: (0, b * n_t + t)),
            tile(LANES),
            pl.BlockSpec((LANES, LANES), lambda b, t: (0, 0)),
            per_seq((SUBLANES, w_a)), per_seq((SUBLANES, w_a)), per_seq((4 * SUBLANES, w_b)),
            per_seq((n_pair, LANES, LANES)),
        ],
        out_shape=[
            jax.ShapeDtypeStruct((n, d), F32),
            jax.ShapeDtypeStruct((n, d), F32),
            jax.ShapeDtypeStruct((SUBLANES, n), jnp.int32),
            jax.ShapeDtypeStruct((n, LANES), F32),
            jax.ShapeDtypeStruct((LANES, LANES), F32),
            jax.ShapeDtypeStruct((n_seq, SUBLANES, w_a), F32),
            jax.ShapeDtypeStruct((n_seq, SUBLANES, w_a), F32),
            jax.ShapeDtypeStruct((n_seq, 4 * SUBLANES, w_b), F32),
            jax.ShapeDtypeStruct((n_seq, n_pair, LANES, LANES), F32),
        ],
        scratch_shapes=[
            pltpu.VMEM((d, d_in), BF16),
            pltpu.VMEM((d_mix, d), BF16),
            pltpu.VMEM((tc, d), BF16),
            pltpu.VMEM((tc, d_in), F32),
            pltpu.VMEM((tc, d_mix), BF16),
            pltpu.VMEM((SUBLANES + tc, w_a), F32),
            pltpu.VMEM((4 * SUBLANES + tc, w_b), F32),
            pltpu.VMEM((SUBLANES, 3 * SUBLANES + tc, w_b), F32),
            pltpu.VMEM((tc, w_a), F32),
            pltpu.VMEM((tc, w_a), F32),
            pltpu.VMEM((tc, w_a), F32),
            pltpu.VMEM((SUBLANES, w_a), F32),
            pltpu.VMEM((n_pair, LANES, LANES), F32),
            pltpu.VMEM((LANES, LANES), F32),
        ],
        compiler_params=_cparams("arbitrary", "arbitrary"),
        name="prompt_layer",
    )(x, lw["g_mix"], w_in, cos, sin,
      zeros(n_seq, SUBLANES, w_a), zeros(n_seq, SUBLANES, w_a), zeros(n_seq, 4 * SUBLANES, w_b),
      zeros(n_seq, n_pair, LANES, LANES),
      lw["conv_a_w"], lw["conv_a_b"], lw["wg"], lw["bg"], lw["lam"],
      lw["conv_b_w"], lw["conv_b_b"], lw["ln_g"], lw["ln_b"], lw["ret_g"],
      dmat, _pair_lanes(xi), _pair_lanes(zeta), gm.reshape(n_pair, LANES, LANES),
      w_out, lw["g_ffn"], lw["wr_hi"], lw["wr_lo"], lw["b_route"])


def _unpair_state(s_pairs):
    b, n_pair = s_pairs.shape[:2]
    half = LANES // 2
    s = s_pairs.reshape(b, n_pair, 2, half, 2, half)
    return jnp.stack([s[:, :, 0, :, 0, :], s[:, :, 1, :, 1, :]], axis=2).reshape(b, 2 * n_pair, half, half)


def _in_proj_kernel(x_ref, g_ref, w_ref, o_ref):
    u = _rms(x_ref[...], g_ref[...]).astype(BF16)
    o_ref[...] = jnp.dot(u, w_ref[0].astype(BF16), preferred_element_type=F32)


def _in_proj(x, g, w_in, layer):
    n, d = x.shape
    d_in = w_in.shape[2]
    tm = _row_tile(n)
    return pl.pallas_call(
        _in_proj_kernel,
        grid=(n // tm,),
        in_specs=[
            pl.BlockSpec((tm, d), lambda i: (i, 0)),
            _resident((1, d), (0, 0)),
            _resident((1, d, d_in), (layer, 0, 0)),
        ],
        out_specs=pl.BlockSpec((tm, d_in), lambda i: (i, 0)),
        out_shape=jax.ShapeDtypeStruct((n, d_in), F32),
        compiler_params=_cparams("arbitrary"),
        name="in_proj_sample",
    )(x, g, w_in)


def _mix_sample_ab_kernel(p_ref, h0_ref, bufa_ref, bufb_ref,
                          caw_ref, cab_ref, wg_ref, bg_ref, lam_ref,
                          cbw_ref, cbb_ref, lng_ref, lnb_ref,
                          o_ref, h_ref, bufa_out_ref, bufb_out_ref, *, ts, w_a, w_b):
    xpa = [bufa_ref[j] for j in range(CONV_A - 1)] + [p_ref[t, :, w_a:2 * w_a] for t in range(ts)]
    xa = []
    for t in range(ts):
        acc = cab_ref[...] + caw_ref[0:1, :] * xpa[t]
        for j in range(1, CONV_A):
            acc = acc + caw_ref[j:j + 1, :] * xpa[t + j]
        xa.append(acc)
    nb = xa[0].shape[0]
    a, bt = _lru_gates(jnp.concatenate(xa, axis=0), wg_ref[...], bg_ref[...], lam_ref[...])
    h = h0_ref[...]
    for t in range(ts):
        h = a[t * nb:(t + 1) * nb, :] * h + bt[t * nb:(t + 1) * nb, :]
        o_ref[t, :, 0:w_a] = (jax.nn.gelu(p_ref[t, :, 0:w_a]) * h).astype(BF16)
    h_ref[...] = h
    for j in range(CONV_A - 1):
        bufa_out_ref[j] = xpa[ts + j]
    b0 = 2 * w_a
    hist = CONV_B - 1
    glu = [p_ref[t, :, b0:b0 + w_b] * _sigmoid(p_ref[t, :, b0 + w_b:b0 + 2 * w_b]) for t in range(ts)]
    xpb = lambda i: bufb_ref[i] if i < hist else glu[i - hist]
    for t in range(ts):
        acc = cbb_ref[...] + cbw_ref[0:1, :] * xpb(t)
        for j in range(1, CONV_B):
            acc = acc + cbw_ref[j:j + 1, :] * xpb(t + j)
        o_ref[t, :, w_a:w_a + w_b] = _silu(_layernorm(acc, lng_ref[...], lnb_ref[...])).astype(BF16)
    for i in range(hist):
        bufb_out_ref[i] = xpb(ts + i)


def _mix_sample_ab(p_t, h0, bufa_t, bufb_t, layer, lw):
    ts, n_seq, width = p_t.shape
    w_a, w_b = lw["w_a"], lw["w_b"]
    nb = 32 if n_seq % 32 == 0 else n_seq
    full = lambda shape: _resident(shape, (0,) * len(shape))
    kern = functools.partial(_mix_sample_ab_kernel, ts=ts, w_a=w_a, w_b=w_b)
    return pl.pallas_call(
        kern,
        grid=(n_seq // nb,),
        in_specs=[
            pl.BlockSpec((ts, nb, width), lambda i: (0, i, 0)),
            pl.BlockSpec((None, nb, w_a), lambda i: (layer, i, 0)),
            pl.BlockSpec((CONV_A - 1, nb, w_a), lambda i: (0, i, 0)),
            pl.BlockSpec((CONV_B - 1, nb, w_b), lambda i: (0, i, 0)),
            full((CONV_A, w_a)), full((1, w_a)), full((w_a, 2 * w_a)), full((1, 2 * w_a)), full((1, w_a)),
            full((CONV_B, w_b)), full((1, w_b)), full((1, w_b)), full((1, w_b)),
        ],
        out_specs=[
            pl.BlockSpec((ts, nb, w_a + w_b), lambda i: (0, i, 0)),
            pl.BlockSpec((nb, w_a), lambda i: (i, 0)),
            pl.BlockSpec((CONV_A - 1, nb, w_a), lambda i: (0, i, 0)),
            pl.BlockSpec((CONV_B - 1, nb, w_b), lambda i: (0, i, 0)),
        ],
        out_shape=[
            jax.ShapeDtypeStruct((ts, n_seq, w_a + w_b), BF16),
            jax.ShapeDtypeStruct((n_seq, w_a), F32),
            jax.ShapeDtypeStruct((CONV_A - 1, n_seq, w_a), F32),
            jax.ShapeDtypeStruct((CONV_B - 1, n_seq, w_b), F32),
        ],
        compiler_params=_cparams("arbitrary"),
        name="mix_sample_ab",
    )(p_t, h0, bufa_t, bufb_t,
      lw["conv_a_w"], lw["conv_a_b"], lw["wg"], lw["bg"], lw["lam"],
      lw["conv_b_w"], lw["conv_b_b"], lw["ln_g"], lw["ln_b"])


def _mix_sample_c_kernel(q_ref, k_ref, v_ref, g_ref, oab_ref, s_ref, cos_ref, sin_ref,
                         dmat_ref, xi_ref, zeta_ref, gcol_ref, rng_ref, tile_ref, fold_ref, *rest,
                         rows, ts, w_c, w_ab, layer):
    if layer:
        prev_ref, mix_ref, sout_ref, qs_ref, kzs_ref, vs_ref, oi_ref = rest
        for l in range(layer):
            sout_ref[l] = prev_ref[l]
    else:
        mix_ref, sout_ref, qs_ref, kzs_ref, vs_ref, oi_ref = rest
    n_pair = w_c // LANES
    hd = LANES // 2
    lane = lax.broadcasted_iota(jnp.int32, (rows, LANES), 1)
    head0 = lane < hd
    first_half = (lane % hd) < (hd // 2)
    cos = cos_ref[...]
    sin = sin_ref[...]
    for p in range(n_pair):
        cs = slice(p * LANES, (p + 1) * LANES)
        q = _rope_pair(q_ref[:, cs], cos, sin, first_half)
        k = _rope_pair(k_ref[:, cs], cos, sin, first_half) * (1.0 / hd) ** 0.5
        v = v_ref[:, cs]
        qb, kb, vb = q.astype(BF16), k.astype(BF16), v.astype(BF16)
        oi_ref[:, cs] = _retention_intra_pair(qb, kb, vb, dmat_ref[2 * p], dmat_ref[2 * p + 1], head0)
        qs_ref[:, cs] = q
        kzs_ref[:, cs] = k * zeta_ref[:, cs]
        vs_ref[:, cs] = v
    grp = SUBLANES
    per_grp = grp // ts
    stack = NH_C * grp
    srow = lax.broadcasted_iota(jnp.int32, (stack, w_c), 0)
    slane = lax.broadcasted_iota(jnp.int32, (stack, w_c), 1)
    own_head = (srow // grp) == (slane // hd)
    tile_m = tile_ref[...]
    fold_m = fold_ref[...]
    for gi in range(rows // grp):
        rs = slice(gi * grp, (gi + 1) * grp)
        q8 = jnp.concatenate([qs_ref[rs, :]] * NH_C, axis=0)
        kz8 = jnp.concatenate([kzs_ref[rs, :]] * NH_C, axis=0)
        v8 = jnp.concatenate([vs_ref[rs, :]] * NH_C, axis=0)
        vr = jnp.dot(jnp.where(own_head, v8, 0.0).astype(BF16), fold_m, preferred_element_type=F32).astype(BF16)
        o_stack = jnp.zeros((stack, hd), F32)
        for s2 in range(per_grp):
            sq = gi * per_grp + s2
            own_seq = ((srow % grp) // ts) == s2
            sel = jnp.logical_and(own_head, own_seq)
            s_old = s_ref[0, sq].reshape(w_c, hd)
            o_stack = o_stack + jnp.dot(jnp.where(sel, q8, 0.0).astype(BF16), s_old.astype(BF16),
                                        preferred_element_type=F32)
            upd = lax.dot_general(jnp.where(sel, kz8, 0.0).astype(BF16), vr, (((0,), (0,)), ((), ())),
                                  preferred_element_type=F32)
            sout_ref[layer, sq] = (gcol_ref[...] * s_old + upd).reshape(NH_C, hd, hd)
        p1 = o_stack.astype(BF16)
        r1 = o_stack - p1.astype(F32)
        p2 = r1.astype(BF16)
        p3 = (r1 - p2.astype(F32)).astype(BF16)
        spread = (jnp.dot(p1, tile_m, preferred_element_type=F32)
                  + jnp.dot(p2, tile_m, preferred_element_type=F32)
                  + jnp.dot(p3, tile_m, preferred_element_type=F32))
        spread = jnp.where(own_head, spread, 0.0).reshape(NH_C, grp, w_c)
        oi_ref[rs, :] = oi_ref[rs, :] + jnp.sum(spread, axis=0) * xi_ref[rs, :]
    mix_ref[:, 0:w_ab] = oab_ref[...]
    for p in range(n_pair):
        cs = slice(p * LANES, (p + 1) * LANES)
        on = _head_norm_pair(oi_ref[:, cs], head0, 1.0 / hd) * rng_ref[:, cs]
        mix_ref[:, w_ab + p * LANES:w_ab + (p + 1) * LANES] = (_silu(g_ref[:, cs]) * on).astype(BF16)


def _mix_sample_c(proj, n_seq, ts, oab, state, layer, prev_states, lw):
    w_a, w_b, w_c = lw["w_a"], lw["w_b"], lw["w_c"]
    hd = LANES // 2
    n_rows = n_seq * ts
    rows = 64 if n_rows % 64 == 0 else n_rows
    assert SUBLANES % ts == 0 and rows % SUBLANES == 0
    seq_per = rows // ts
    c0 = 2 * w_a + 2 * w_b
    assert c0 % w_c == 0
    cb = c0 // w_c
    log_g = jnp.log1p(-jnp.exp2(-5.0 - jnp.arange(NH_C, dtype=F32)))
    r = jnp.arange(rows)
    step = (r % ts).astype(F32)
    diff = step[:, None] - step[None, :]
    live = jnp.logical_and((r[:, None] // ts) == (r[None, :] // ts), diff >= 0)
    dmat = jnp.where(live[None], jnp.exp(jnp.where(live, diff, 0.0)[None] * log_g[:, None, None]), 0.0)
    xi = jnp.repeat(jnp.exp((step + 1.0)[None, :] * log_g[:, None]).T, hd, axis=1)
    zeta = jnp.repeat(jnp.exp((ts - 1.0 - step)[None, :] * log_g[:, None]).T, hd, axis=1)
    gcol = jnp.broadcast_to(jnp.repeat(jnp.exp(ts * log_g), hd)[:, None], (w_c, hd))
    pos = PAST_LEN + (r % ts).astype(jnp.int32)
    cos, sin = _rope_tables(pos)
    tile_m = jnp.tile(jnp.eye(hd, dtype=BF16), (1, NH_C))
    fold_m = jnp.tile(jnp.eye(hd, dtype=BF16), (NH_C, 1))
    full = lambda shape: _resident(shape, (0,) * len(shape))
    col = lambda j: pl.BlockSpec((rows, w_c), lambda i: (i, cb + j))
    states = lambda n_layers, first: pl.BlockSpec((n_layers, seq_per, NH_C, hd, hd), lambda i: (first, i, 0, 0, 0))
    kern = functools.partial(_mix_sample_c_kernel, rows=rows, ts=ts, w_c=w_c, w_ab=w_a + w_b, layer=layer)
    in_specs = [
        col(0), col(1), col(2), col(3),
        pl.BlockSpec((rows, w_a + w_b), lambda i: (i, 0)),
        states(1, layer),
        full((rows, LANES)), full((rows, LANES)),
        full((NH_C, rows, rows)), full((rows, w_c)), full((rows, w_c)), full((w_c, hd)), full((1, w_c)),
        full((hd, w_c)), full((w_c, hd)),
    ]
    args = [proj, proj, proj, proj, oab, state, cos, sin, dmat, xi, zeta, gcol, lw["ret_g"], tile_m, fold_m]
    if layer:
        in_specs.append(states(layer, 0))
        args.append(prev_states)
    return pl.pallas_call(
        kern,
        grid=(n_rows // rows,),
        in_specs=in_specs,
        out_specs=[
            pl.BlockSpec((rows, w_a + w_b + w_c), lambda i: (i, 0)),
            states(layer + 1, 0),
        ],
        out_shape=[
            jax.ShapeDtypeStruct((n_rows, w_a + w_b + w_c), BF16),
            jax.ShapeDtypeStruct((layer + 1,) + state.shape[1:], F32),
        ],
        scratch_shapes=[pltpu.VMEM((rows, w_c), F32)] * 4,
        compiler_params=_cparams("arbitrary"),
        name="mix_sample_c",
    )(*args)


def _out_proj_kernel(x_ref, mix_ref, w_ref, g_ref, wrh_ref, wrl_ref, br_ref, seen_ref,
                     xmid_ref, xn_ref, rt_ref, rg_ref, counts_ref, carry_ref):
    @pl.when(pl.program_id(0) == 0)
    def _():
        carry_ref[...] = seen_ref[...]

    xm = x_ref[...] + jnp.dot(mix_ref[...], w_ref[0].astype(BF16), preferred_element_type=F32)
    xmid_ref[...] = xm
    _ffn_norm_route(xm, g_ref[...], wrh_ref[...], wrl_ref[...], br_ref[...], carry_ref,
                    xn_ref, rt_ref, rg_ref, counts_ref)


def _out_proj(x, mix, w_out, layer, seen, lw):
    n, d = x.shape
    d_mix = mix.shape[1]
    tm = _row_tile(n)
    assert tm % LANES == 0
    full = lambda shape: _resident(shape, (0,) * len(shape))
    rows = lambda width: pl.BlockSpec((tm, width), lambda i: (i, 0))
    return pl.pallas_call(
        _out_proj_kernel,
        grid=(n // tm,),
        in_specs=[rows(d), rows(d_mix), _resident((1, d_mix, d), (layer, 0, 0)),
                  full((1, d)), full((LANES, d)), full((LANES, d)), full((LANES, LANES)), full((LANES, LANES))],
        out_specs=[rows(d), rows(d), pl.BlockSpec((SUBLANES, tm), lambda i: (0, i)), rows(LANES),
                   pl.BlockSpec((LANES, LANES), lambda i: (0, 0))],
        out_shape=[
            jax.ShapeDtypeStruct((n, d), F32),
            jax.ShapeDtypeStruct((n, d), F32),
            jax.ShapeDtypeStruct((SUBLANES, n), jnp.int32),
            jax.ShapeDtypeStruct((n, LANES), F32),
            jax.ShapeDtypeStruct((LANES, LANES), F32),
        ],
        scratch_shapes=[pltpu.VMEM((LANES, LANES), F32)],
        compiler_params=_cparams("arbitrary"),
        name="out_proj_sample",
    )(x, mix, w_out, lw["g_ffn"], lw["wr_hi"], lw["wr_lo"], lw["b_route"], seen)


SLOT_TILE = 256
DMA_UNROLL = 8


def _slot_plan(counts, n_tokens):
    n_slots = 2 * n_tokens + N_EXPERTS * SLOT_TILE
    n_slots = -(-n_slots // SLOT_TILE) * SLOT_TILE
    n_tiles = n_slots // SLOT_TILE
    padded = (counts + (SLOT_TILE - 1)) // SLOT_TILE * SLOT_TILE
    ends = jnp.cumsum(padded)
    offs = ends - padded
    total = ends[-1]
    tile_start = jnp.arange(n_tiles, dtype=jnp.int32) * SLOT_TILE
    probe = jnp.minimum(tile_start, total - 1)
    tile_expert = jnp.sum((ends[None, :] <= probe[:, None]).astype(jnp.int32), axis=1)
    tile_expert = jnp.minimum(tile_expert, N_EXPERTS - 1)
    n_valid = (total // SLOT_TILE).reshape(1)
    last_tile = jnp.where(padded > 0, ends - SLOT_TILE, 0)
    return n_slots, offs, tile_expert, n_valid, last_tile, (padded > 0).astype(jnp.int32)


def _row_copy(src_ref, src_row, dst_ref, dst_row, sem):
    return pltpu.make_async_copy(src_ref.at[pl.ds(src_row, 1), :], dst_ref.at[pl.ds(dst_row, 1), :], sem)


def _dispatch_kernel(last_ref, used_ref, nv_ref, pos1_ref, pos2_ref, xnp_ref, xns_ref, xs_ref,
                     zero_ref, zsem, sem, *, n_prompt_tiles):
    tm = xnp_ref.shape[0]
    n_tiles = xs_ref.shape[0] // SLOT_TILE
    i = pl.program_id(0)

    @pl.when(i == 0)
    def _():
        zero_ref[...] = jnp.zeros_like(zero_ref)
        tile = lambda row: pltpu.make_async_copy(
            zero_ref, xs_ref.at[pl.ds(pl.multiple_of(row, SLOT_TILE), SLOT_TILE), :], zsem)
        for e in range(N_EXPERTS):
            @pl.when(used_ref[e] != 0)
            def _():
                tile(last_ref[e]).start()
        lax.fori_loop(nv_ref[0], n_tiles, lambda j, c: (tile(j * SLOT_TILE).start(), c)[1], 0)
        for e in range(N_EXPERTS):
            @pl.when(used_ref[e] != 0)
            def _():
                tile(0).wait()
        lax.fori_loop(nv_ref[0], n_tiles, lambda j, c: (tile(0).wait(), c)[1], 0)

    def scatter(src_ref):
        def issue(t, c):
            _row_copy(src_ref, t, xs_ref, pos1_ref[t], sem).start()
            _row_copy(src_ref, t, xs_ref, pos2_ref[t], sem).start()
            return c

        lax.fori_loop(0, tm, issue, 0, unroll=DMA_UNROLL)

        def drain(t, c):
            _row_copy(src_ref, 0, xs_ref, 0, sem).wait()
            _row_copy(src_ref, 0, xs_ref, 0, sem).wait()
            return c

        lax.fori_loop(0, tm, drain, 0, unroll=DMA_UNROLL)

    @pl.when(i < n_prompt_tiles)
    def _():
        scatter(xnp_ref)

    @pl.when(i >= n_prompt_tiles)
    def _():
        scatter(xns_ref)


def _dispatch(xn_p, xn_s, pos1, pos2, n_slots, last_tile, used, n_valid):
    n_p, d = xn_p.shape
    n_s = xn_s.shape[0]
    tm = math.gcd(_row_tile(n_p), _row_tile(n_s))
    tp = n_p // tm
    ts_last = n_s // tm - 1
    smem_rows = lambda: pl.BlockSpec((tm,), lambda i, *_: (i,), memory_space=pltpu.SMEM)
    grid_spec = pltpu.PrefetchScalarGridSpec(
        num_scalar_prefetch=3,
        grid=((n_p + n_s) // tm,),
        in_specs=[smem_rows(), smem_rows(),
                  pl.BlockSpec((tm, d), lambda i, *_: (jnp.minimum(i, tp - 1), 0)),
                  pl.BlockSpec((tm, d), lambda i, *_: (jnp.clip(i - tp, 0, ts_last), 0))],
        out_specs=pl.BlockSpec(memory_space=pl.ANY),
        scratch_shapes=[pltpu.VMEM((SLOT_TILE, d), F32), pltpu.SemaphoreType.DMA, pltpu.SemaphoreType.DMA],
    )
    return pl.pallas_call(
        functools.partial(_dispatch_kernel, n_prompt_tiles=tp),
        grid_spec=grid_spec,
        out_shape=jax.ShapeDtypeStruct((n_slots, d), F32),
        compiler_params=_cparams("arbitrary"),
        name="moe_dispatch",
    )(last_tile, used, n_valid, pos1, pos2, xn_p, xn_s)


def _experts_kernel(te_ref, nv_ref, xs_ref, w1_ref, w3_ref, w2_ref, ys_ref, w1b_ref, w3b_ref, w2b_ref):
    j = pl.program_id(0)
    fresh = jnp.logical_or(j == 0, te_ref[j] != te_ref[jnp.maximum(j - 1, 0)])

    @pl.when(fresh)
    def _():
        w1b_ref[...] = w1_ref[0, 0].astype(BF16)
        w3b_ref[...] = w3_ref[0, 0].astype(BF16)
        w2b_ref[...] = w2_ref[0, 0].astype(BF16)

    @pl.when(j < nv_ref[0])
    def _():
        x = xs_ref[...].astype(BF16)
        h1 = jnp.dot(x, w1b_ref[...], preferred_element_type=F32)
        h3 = jnp.dot(x, w3b_ref[...], preferred_element_type=F32)
        h = (_silu(h1) * h3).astype(BF16)
        ys_ref[...] = jnp.dot(h, w2b_ref[...], preferred_element_type=F32)

    @pl.when(j >= nv_ref[0])
    def _():
        ys_ref[...] = jnp.zeros_like(ys_ref)


def _experts(xs, tile_expert, n_valid, w1, w3, w2, layer):
    n_slots, d = xs.shape
    d_e = w1.shape[3]
    tile = lambda j, te, nv: (jnp.minimum(j, nv[0] - 1), 0)
    expert = lambda j, te, nv: (layer, te[j], 0, 0)
    grid_spec = pltpu.PrefetchScalarGridSpec(
        num_scalar_prefetch=2,
        grid=(n_slots // SLOT_TILE,),
        in_specs=[
            pl.BlockSpec((SLOT_TILE, d), tile),
            pl.BlockSpec((1, 1, d, d_e), expert),
            pl.BlockSpec((1, 1, d, d_e), expert),
            pl.BlockSpec((1, 1, d_e, d), expert),
        ],
        out_specs=pl.BlockSpec((SLOT_TILE, d), lambda j, te, nv: (j, 0)),
        scratch_shapes=[pltpu.VMEM((d, d_e), BF16), pltpu.VMEM((d, d_e), BF16), pltpu.VMEM((d_e, d), BF16)],
    )
    return pl.pallas_call(
        _experts_kernel,
        grid_spec=grid_spec,
        out_shape=jax.ShapeDtypeStruct((n_slots, d), F32),
        compiler_params=_cparams("arbitrary"),
        name="moe_experts",
    )(tile_expert, n_valid, xs, w1, w3, w2)


def _combine_kernel(pos1_ref, pos2_ref, xmid_ref, rg_ref, gfin_ref, ys_ref, o_ref, y1_ref, y2_ref, sem,
                    *, final_norm):
    tm = xmid_ref.shape[0]

    def issue(t, c):
        _row_copy(ys_ref, pos1_ref[t], y1_ref, t, sem).start()
        _row_copy(ys_ref, pos2_ref[t], y2_ref, t, sem).start()
        return c

    lax.fori_loop(0, tm, issue, 0, unroll=DMA_UNROLL)

    def drain(t, c):
        _row_copy(ys_ref, 0, y1_ref, 0, sem).wait()
        _row_copy(ys_ref, 0, y2_ref, 0, sem).wait()
        return c

    lax.fori_loop(0, tm, drain, 0, unroll=DMA_UNROLL)
    rg = rg_ref[...]
    out = xmid_ref[...] + (rg[:, 0:1] * y1_ref[...] + rg[:, 1:2] * y2_ref[...])
    o_ref[...] = _rms(out, gfin_ref[...]) if final_norm else out


def _combine(xmid, rg, pos1, pos2, row0, ys, g_final, final_norm):
    n, d = xmid.shape
    tm = math.gcd(_row_tile(n), _row_tile(row0)) if row0 else _row_tile(n)
    t0 = row0 // tm
    smem_rows = lambda: pl.BlockSpec((tm,), lambda i: (t0 + i,), memory_space=pltpu.SMEM)
    kern = functools.partial(_combine_kernel, final_norm=final_norm)
    return pl.pallas_call(
        kern,
        grid=(n // tm,),
        in_specs=[smem_rows(), smem_rows(),
                  pl.BlockSpec((tm, d), lambda i: (i, 0)),
                  pl.BlockSpec((tm, LANES), lambda i: (i, 0)),
                  _resident((1, d), (0, 0)),
                  pl.BlockSpec(memory_space=pl.ANY)],
        out_specs=pl.BlockSpec((tm, d), lambda i: (i, 0)),
        out_shape=jax.ShapeDtypeStruct((n, d), F32),
        scratch_shapes=[pltpu.VMEM((tm, d), F32), pltpu.VMEM((tm, d), F32), pltpu.SemaphoreType.DMA],
        compiler_params=_cparams("arbitrary"),
        name="moe_combine",
    )(pos1, pos2, xmid, rg, g_final, ys)


def _moe_routed(prompt, sample, counts, w1, w3, w2, layer, g_final, final_norm):
    xmid_p, xn_p, rt_p, rg_p = prompt
    xmid_s, xn_s, rt_s, rg_s = sample
    n_p = xmid_p.shape[0]
    n = n_p + xmid_s.shape[0]
    n_slots, offs, tile_expert, n_valid, last_tile, used = _slot_plan(
        counts[:N_EXPERTS, 0].astype(jnp.int32), n)
    rt = jnp.concatenate([rt_p, rt_s], axis=1)
    experts = jnp.arange(N_EXPERTS, dtype=jnp.int32)[:, None]
    base = lambda e: jnp.sum(jnp.where(experts == e[None, :], offs[:, None], 0), axis=0)
    pos1 = base(rt[0]) + rt[2]
    pos2 = base(rt[1]) + rt[3]
    xs = _dispatch(xn_p, xn_s, pos1, pos2, n_slots, last_tile, used, n_valid)
    ys = _experts(xs, tile_expert, n_valid, w1, w3, w2, layer)
    y_p = _combine(xmid_p, rg_p, pos1, pos2, 0, ys, g_final, final_norm)
    y_s = _combine(xmid_s, rg_s, pos1, pos2, n_p, ys, g_final, final_norm)
    return y_p, y_s


def kernel(x_prompt, x_sample, state_lru_h, state_lru_conv, state_conv, state_ret, norm_mix_g, norm_ffn_g, final_norm_g, w_in, lru_conv_w, lru_conv_b, lru_wa, lru_ba, lru_wx, lru_bx, lru_lambda, cm_dw_w, cm_dw_b, cm_ln_g, cm_ln_b, ret_norm_g, w_out, moe_w_group, moe_b_group, moe_w_expert, moe_b_expert, moe_w1, moe_w3, moe_w2):
    bp, tp, d = x_prompt.shape
    bs, ts, _ = x_sample.shape
    depth = w_in.shape[0]
    w_a = lru_lambda.shape[1]
    w_b = cm_dw_b.shape[1]
    w_c = ret_norm_g.shape[1]
    n_p, n_s = bp * tp, bs * ts
    row = lambda v: v.reshape(1, -1)

    x_p, x_s = x_prompt.reshape(n_p, d), x_sample.reshape(n_s, d)
    outs = {k: [] for k in ("p_h", "p_ca", "p_cb", "p_s", "s_h", "s_ca", "s_cb")}
    s_states = None
    for l in range(depth):
        wr_hi, wr_lo, b_route = _route_weights(moe_w_group[l], moe_b_group[l], moe_w_expert[l], moe_b_expert[l])
        lw = dict(
            w_a=w_a, w_b=w_b, w_c=w_c, g_mix=row(norm_mix_g[l]), g_ffn=row(norm_ffn_g[l]),
            conv_a_w=lru_conv_w[l], conv_a_b=row(lru_conv_b[l]),
            wg=_blockdiag_gate_weights(lru_wa[l], lru_wx[l]).astype(BF16),
            bg=row(jnp.concatenate([lru_ba[l], lru_bx[l]])), lam=row(lru_lambda[l]),
            conv_b_w=cm_dw_w[l], conv_b_b=row(cm_dw_b[l]), ln_g=row(cm_ln_g[l]), ln_b=row(cm_ln_b[l]),
            ret_g=row(ret_norm_g[l]), wr_hi=wr_hi, wr_lo=wr_lo, b_route=b_route,
        )
        xmid_p, xn_p, rt_p, rg_p, seen, h_p, ca_p, cb_p, s_p = _prompt_layer(x_p, bp, tp, l, w_in, w_out, lw)
        outs["p_h"].append(h_p[:, 0, :])
        outs["p_ca"].append(ca_p[:, SUBLANES - (CONV_A - 1):, :])
        outs["p_cb"].append(cb_p[:, 4 * SUBLANES - (CONV_B - 1):, :])
        outs["p_s"].append(_unpair_state(s_p))
        proj_s = _in_proj(x_s, lw["g_mix"], w_in, l)
        p_t = proj_s[:, :2 * w_a + 2 * w_b].reshape(bs, ts, -1).transpose(1, 0, 2)
        oab_t, h_s, ca_t, cb_t = _mix_sample_ab(
            p_t, state_lru_h, state_lru_conv[l].transpose(1, 0, 2), state_conv[l].transpose(1, 0, 2), l, lw)
        oab = oab_t.transpose(1, 0, 2).reshape(n_s, w_a + w_b)
        mix_s, s_states = _mix_sample_c(proj_s, bs, ts, oab, state_ret, l, s_states, lw)
        outs["s_h"].append(h_s)
        outs["s_ca"].append(ca_t.transpose(1, 0, 2))
        outs["s_cb"].append(cb_t.transpose(1, 0, 2))
        xmid_s, xn_s, rt_s, rg_s, counts = _out_proj(x_s, mix_s, w_out, l, seen, lw)
        x_p, x_s = _moe_routed((xmid_p, xn_p, rt_p, rg_p), (xmid_s, xn_s, rt_s, rg_s), counts,
                               moe_w1, moe_w3, moe_w2, l, row(final_norm_g), l == depth - 1)
    st = lambda k: jnp.stack(outs[k])
    return (x_p.reshape(bp, tp, d), x_s.reshape(bs, ts, d), st("p_h"), st("p_ca"), st("p_cb"), st("p_s"),
            st("s_h"), st("s_ca"), st("s_cb"), s_states)
```

```python
import functools
import math

import jax
import jax.numpy as jnp
from jax import lax
from jax.experimental import pallas as pl
from jax.experimental.pallas import tpu as pltpu

PAST_LEN = 16384
LRU_C = 8.0
CONV_A = 4
CONV_B = 31
NH_C = 8
RET_CHUNK = 128
ROPE_BASE = 10000.0
N_GROUPS = 4
EXPERTS_PER_GROUP = 8
N_EXPERTS = N_GROUPS * EXPERTS_PER_GROUP
EPS = 1e-6

LANES = 128
SUBLANES = 8
VMEM_LIMIT_BYTES = 56 * 1024 * 1024

F32 = jnp.float32
BF16 = jnp.bfloat16


def _cparams(*sem):
    return pltpu.CompilerParams(dimension_semantics=sem, vmem_limit_bytes=VMEM_LIMIT_BYTES)


def _row_tile(n):
    for t in (512, 256, 128, 64, 32, 16, 8):
        if n % t == 0:
            return t
    raise ValueError(f"row count {n} must be a multiple of 8")


def _resident(shape, index):
    return pl.BlockSpec(shape, lambda *_: index, pipeline_mode=pl.Buffered(1))


def _rms(x, g):
    ms = jnp.mean(x * x, axis=-1, keepdims=True)
    return x * lax.rsqrt(ms + EPS) * g


def _sigmoid(x):
    return jax.nn.sigmoid(x)


def _silu(x):
    return x * jax.nn.sigmoid(x)


def _layernorm(x, g, b):
    mu = jnp.mean(x, axis=-1, keepdims=True)
    xc = x - mu
    var = jnp.mean(xc * xc, axis=-1, keepdims=True)
    return xc * lax.rsqrt(var + EPS) * g + b


def _rope_pair(x, cos, sin_signed, first_half):
    partner = jnp.where(first_half, pltpu.roll(x, 96, 1), pltpu.roll(x, 32, 1))
    return x * cos + partner * sin_signed


def _head_norm_pair(o, head0, inv_d):
    zero = jnp.zeros_like(o)
    s0 = jnp.sum(jnp.where(head0, o, zero), axis=-1, keepdims=True)
    s1 = jnp.sum(jnp.where(head0, zero, o), axis=-1, keepdims=True)
    oc = o - jnp.where(head0, s0, s1) * inv_d
    q = oc * oc
    v0 = jnp.sum(jnp.where(head0, q, zero), axis=-1, keepdims=True)
    v1 = jnp.sum(jnp.where(head0, zero, q), axis=-1, keepdims=True)
    return oc * lax.rsqrt(jnp.where(head0, v0, v1) * inv_d + EPS)


def _retention_intra_pair(qb, kb, vb, dmat0, dmat1, head0):
    outs = []
    for j, dm in ((0, dmat0), (1, dmat1)):
        keep = head0 if j == 0 else jnp.logical_not(head0)
        qm = jnp.where(keep, qb, jnp.zeros_like(qb))
        s = lax.dot_general(qm, kb, (((1,), (1,)), ((), ())), preferred_element_type=F32)
        inner = (s * dm).astype(BF16)
        outs.append(jnp.dot(inner, vb, preferred_element_type=F32))
    return jnp.where(head0, outs[0], outs[1])


def _lru_gates(xa, wg, bg, lam):
    w_a = xa.shape[-1]
    gates = jnp.dot(xa.astype(BF16), wg, preferred_element_type=F32) + bg
    r = _sigmoid(gates[:, :w_a])
    i = _sigmoid(gates[:, w_a:])
    z = -lam
    softplus = jnp.maximum(z, 0.0) + jnp.log1p(jnp.exp(-jnp.abs(z)))
    log_a = -LRU_C * r * softplus
    a = jnp.exp(log_a)
    bt = jnp.sqrt(-jnp.tanh(log_a) * (a * a + 1.0)) * (i * xa)
    return a, bt


ROUTE_TILE = 256
EXPERT_ROW0 = SUBLANES
assert EXPERTS_PER_GROUP == SUBLANES and N_GROUPS <= SUBLANES and EXPERT_ROW0 + N_EXPERTS <= LANES


def _split_bf16(x):
    hi = x.astype(BF16)
    return hi, (x - hi.astype(F32)).astype(BF16)


def _lane_tile(x, width):
    return x if width == x.shape[1] else jnp.concatenate([x] * (width // x.shape[1]), axis=1)


def _route_weights(w_group, b_group, w_expert, b_expert):
    d = w_group.shape[0]
    w = jnp.zeros((LANES, d), F32).at[:N_GROUPS].set(w_group.T).at[EXPERT_ROW0:EXPERT_ROW0 + N_EXPERTS].set(w_expert.T)
    b = jnp.zeros((LANES,), F32).at[:N_GROUPS].set(b_group).at[EXPERT_ROW0:EXPERT_ROW0 + N_EXPERTS].set(b_expert)
    hi, lo = _split_bf16(w)
    return hi, lo, jnp.broadcast_to(b[:, None], (LANES, LANES))


def _ffn_norm_route(xm, g_ffn, wr_hi, wr_lo, b_route, carry_ref, xn_ref, rt_ref, rg_ref, ts_ref, counts_ref):
    xn = _rms(xm, g_ffn)
    xn_ref[...] = xn
    tm = xn.shape[0]
    xh, xl = _split_bf16(xn)
    dot_t = lambda w, x: lax.dot_general(w, x, (((1,), (1,)), ((), ())), preferred_element_type=F32)
    lt = dot_t(wr_hi, xh) + (dot_t(wr_hi, xl) + dot_t(wr_lo, xh)) + _lane_tile(b_route, tm)
    row8 = lax.broadcasted_iota(jnp.int32, (SUBLANES, tm), 0)
    neg = jnp.full((SUBLANES, tm), -jnp.inf, F32)
    big = jnp.full((SUBLANES, tm), LANES, jnp.int32)
    first = lambda hit: jnp.min(jnp.where(hit, row8, big), axis=0, keepdims=True)
    gl = jnp.where(row8 < N_GROUPS, lt[0:SUBLANES, :], neg)
    gmax = jnp.max(gl, axis=0, keepdims=True)
    p_top = 1.0 / jnp.sum(jnp.exp(gl - gmax), axis=0, keepdims=True)
    g_sel = first(gl == gmax)
    slab = lambda g: lt[EXPERT_ROW0 + g * SUBLANES:EXPERT_ROW0 + (g + 1) * SUBLANES, :]
    el = slab(N_GROUPS - 1)
    for g in reversed(range(N_GROUPS - 1)):
        el = jnp.where(g_sel == g, slab(g), el)
    v1 = jnp.max(el, axis=0, keepdims=True)
    i1 = first(el == v1)
    el2 = jnp.where(row8 == i1, neg, el)
    v2 = jnp.max(el2, axis=0, keepdims=True)
    i2 = first(el2 == v2)
    ex = jnp.exp(v2 - v1)
    den = 1.0 + ex
    g1 = p_top * (1.0 / den)
    g2 = p_top * (ex / den)
    e1 = g_sel * EXPERTS_PER_GROUP + i1
    e2 = g_sel * EXPERTS_PER_GROUP + i2
    rowe = lax.broadcasted_iota(jnp.int32, (LANES, tm), 0)
    oh1 = rowe == e1
    oh2 = rowe == e2
    onehot = jnp.logical_or(oh1, oh2).astype(BF16)
    earlier = (lax.broadcasted_iota(jnp.int32, (tm, tm), 0)
               < lax.broadcasted_iota(jnp.int32, (tm, tm), 1)).astype(BF16)
    smaller = (lax.broadcasted_iota(jnp.int32, (LANES, LANES), 1)
               < lax.broadcasted_iota(jnp.int32, (LANES, LANES), 0))
    before = carry_ref[...]
    pairs = jnp.broadcast_to(jnp.sum(onehot.astype(F32), axis=1, keepdims=True), before.shape)
    here = jnp.ceil(pairs * (1.0 / SUBLANES)) * SUBLANES
    first_slot = jnp.sum(jnp.where(smaller, here.T, 0.0), axis=1, keepdims=True)
    place = first_slot + jnp.dot(onehot, earlier, preferred_element_type=F32)
    lp1 = jnp.sum(jnp.where(oh1, place, 0.0), axis=0, keepdims=True).astype(jnp.int32)
    lp2 = jnp.sum(jnp.where(oh2, place, 0.0), axis=0, keepdims=True).astype(jnp.int32)
    carry_ref[...] = before + here
    counts_ref[...] = before + here
    sub8 = lax.broadcasted_iota(jnp.int32, (SUBLANES, LANES), 0)
    ts_ref[...] = jnp.where(sub8 == 0, before.T[0:SUBLANES, :], here.T[0:SUBLANES, :]).astype(jnp.int32)
    zi = jnp.zeros((SUBLANES, tm), jnp.int32)
    rt_ref[...] = jnp.where(row8 == 0, e1, jnp.where(row8 == 1, e2, jnp.where(
        row8 == 2, lp1, jnp.where(row8 == 3, lp2, zi))))
    rg_ref[...] = jnp.where(rowe == 0, g1, jnp.where(rowe == 1, g2, 0.0)).T


CONV_B_ROWS = 64


def _prompt_layer_kernel(x_ref, gmix_ref, win_ref, cos_ref, sin_ref, h0_ref, bufa0_ref, bufb0_ref, s0_ref,
                         caw_ref, cab_ref, wg_ref, bg_ref, lam_ref,
                         cbw_ref, cbb_ref, lng_ref, lnb_ref, rng_ref,
                         dmat_ref, xi_ref, zeta_ref, gmat_ref,
                         wout_ref, gffn_ref, wrh_ref, wrl_ref, br_ref,
                         xmid_ref, xn_ref, rt_ref, rg_ref, ts_ref, counts_ref,
                         hout_ref, bufa_out_ref, bufb_out_ref, sout_ref,
                         winb_ref, woutb_ref, u_ref, proj_ref, mix_ref,
                         xpa_ref, xpb_ref, sh_ref, a_ref, b_ref, hseq_ref, hcar_ref, s_ref, carry_ref,
                         *, tc, w_a, w_b, w_c, chunk):
    t = pl.program_id(1)
    n_t = pl.num_programs(1)
    pa = SUBLANES
    pb = 4 * SUBLANES

    @pl.when(jnp.logical_and(pl.program_id(0) == 0, t == 0))
    def _():
        winb_ref[...] = win_ref[0].astype(BF16)
        woutb_ref[...] = wout_ref[0].astype(BF16)
        carry_ref[...] = jnp.zeros_like(carry_ref)

    @pl.when(t == 0)
    def _():
        xpa_ref[0:pa, :] = bufa0_ref[0]
        xpb_ref[0:pb, :] = bufb0_ref[0]
        hcar_ref[...] = h0_ref[0]
        s_ref[...] = s0_ref[0]

    u = _rms(x_ref[...], gmix_ref[...]).astype(BF16)
    u_ref[...] = u

    def in_proj(c0, c1):
        for c in range(c0, c1, 4 * LANES):
            ce = min(c + 4 * LANES, c1)
            proj_ref[:, c:ce] = jnp.dot(u_ref[...], winb_ref[:, c:ce], preferred_element_type=F32)

    in_proj(0, 2 * w_a)

    xpa_ref[pa:pa + tc, :] = proj_ref[:, w_a:2 * w_a]
    xa = cab_ref[...] + caw_ref[0:1, :] * xpa_ref[pa - 3:pa - 3 + tc, :]
    for j in range(1, CONV_A):
        xa = xa + caw_ref[j:j + 1, :] * xpa_ref[pa - 3 + j:pa - 3 + j + tc, :]
    a, bt = _lru_gates(xa, wg_ref[...], bg_ref[...], lam_ref[...])
    a_ref[...] = a
    b_ref[...] = bt
    row8 = lax.broadcasted_iota(jnp.int32, (SUBLANES, w_a), 0)

    def scan_block(i, hprev):
        r0 = pl.multiple_of(i * SUBLANES, SUBLANES)
        ab = a_ref[pl.ds(r0, SUBLANES), :]
        bb = b_ref[pl.ds(r0, SUBLANES), :]
        for k in (1, 2, 4):
            a_sh = jnp.where(row8 >= k, pltpu.roll(ab, k, 0), 1.0)
            b_sh = jnp.where(row8 >= k, pltpu.roll(bb, k, 0), 0.0)
            bb = ab * b_sh + bb
            ab = ab * a_sh
        h = ab * hprev + bb
        hseq_ref[pl.ds(r0, SUBLANES), :] = h
        return jnp.broadcast_to(h[SUBLANES - 1:SUBLANES, :], (SUBLANES, w_a))

    hcar_ref[...] = lax.fori_loop(0, tc // SUBLANES, scan_block, hcar_ref[...])
    in_proj(2 * w_a, proj_ref.shape[1])
    mix_ref[:, 0:w_a] = (jax.nn.gelu(proj_ref[:, 0:w_a]) * hseq_ref[...]).astype(BF16)
    xpa_ref[0:pa, :] = xpa_ref[tc:tc + pa, :]

    b0 = 2 * w_a
    xpb_ref[pb:pb + tc, :] = proj_ref[:, b0:b0 + w_b] * _sigmoid(proj_ref[:, b0 + w_b:b0 + 2 * w_b])
    first = pb - (CONV_B - 1)
    for r in range(SUBLANES):
        span = tc + (CONV_B - 1 - r) // SUBLANES * SUBLANES
        sh_ref[r, 0:span, :] = xpb_ref[first + r:first + r + span, :]
    for blk in range(tc // CONV_B_ROWS):
        r0 = blk * CONV_B_ROWS
        acc = jnp.broadcast_to(cbb_ref[...], (CONV_B_ROWS, w_b))
        for j in range(CONV_B):
            r = j % SUBLANES
            acc = acc + cbw_ref[j:j + 1, :] * sh_ref[r, r0 + j - r:r0 + j - r + CONV_B_ROWS, :]
        out_b = _silu(_layernorm(acc, lng_ref[...], lnb_ref[...]))
        mix_ref[r0:r0 + CONV_B_ROWS, w_a:w_a + w_b] = out_b.astype(BF16)
    xpb_ref[0:pb, :] = xpb_ref[tc:tc + pb, :]

    c0 = 2 * w_a + 2 * w_b
    n_pair = w_c // LANES
    lane = lax.broadcasted_iota(jnp.int32, (chunk, LANES), 1)
    head0 = lane < (LANES // 2)
    first_half = (lane % (LANES // 2)) < (LANES // 4)
    rl = lax.broadcasted_iota(jnp.int32, (LANES, LANES), 0) < (LANES // 2)
    cl = lax.broadcasted_iota(jnp.int32, (LANES, LANES), 1) < (LANES // 2)
    same_head = rl == cl
    inv_d = 2.0 / LANES
    for c in range(tc // chunk):
        rows = slice(c * chunk, (c + 1) * chunk)
        cos = cos_ref[rows, :]
        sin = sin_ref[rows, :]
        for p in range(n_pair):
            cq = c0 + p * LANES
            q = _rope_pair(proj_ref[rows, cq:cq + LANES], cos, sin, first_half)
            k = _rope_pair(proj_ref[rows, cq + w_c:cq + w_c + LANES], cos, sin, first_half) * (2.0 / LANES) ** 0.5
            v = proj_ref[rows, cq + 2 * w_c:cq + 2 * w_c + LANES]
            g = proj_ref[rows, cq + 3 * w_c:cq + 3 * w_c + LANES]
            qb, kb, vb = q.astype(BF16), k.astype(BF16), v.astype(BF16)
            o = _retention_intra_pair(qb, kb, vb, dmat_ref[2 * p], dmat_ref[2 * p + 1], head0)
            s_old = s_ref[p]
            o = o + jnp.dot(qb, s_old.astype(BF16), preferred_element_type=F32) * xi_ref[p]
            kz = (k * zeta_ref[p]).astype(BF16)
            upd = lax.dot_general(kz, vb, (((0,), (0,)), ((), ())), preferred_element_type=F32)
            s_ref[p] = gmat_ref[p] * s_old + jnp.where(same_head, upd, 0.0)
            on = _head_norm_pair(o, head0, inv_d) * rng_ref[:, p * LANES:(p + 1) * LANES]
            mix_ref[rows, w_a + w_b + p * LANES:w_a + w_b + (p + 1) * LANES] = (_silu(g) * on).astype(BF16)

    xm = x_ref[...] + jnp.dot(mix_ref[...], woutb_ref[...], preferred_element_type=F32)
    xmid_ref[...] = xm
    _ffn_norm_route(xm, gffn_ref[...], wrh_ref[...], wrl_ref[...], br_ref[...], carry_ref,
                    xn_ref, rt_ref, rg_ref, ts_ref, counts_ref)

    @pl.when(t == n_t - 1)
    def _():
        hout_ref[0] = hcar_ref[...]
        bufa_out_ref[0] = xpa_ref[0:pa, :]
        bufb_out_ref[0] = xpb_ref[0:pb, :]
        sout_ref[0] = s_ref[...]


def _retention_tables(chunk):
    log_g = jnp.log1p(-jnp.exp2(-5.0 - jnp.arange(NH_C, dtype=F32)))
    idx = jnp.arange(chunk, dtype=F32)
    diff = idx[:, None] - idx[None, :]
    causal = diff >= 0
    dmat = jnp.where(causal[None], jnp.exp(jnp.where(causal, diff, 0.0)[None] * log_g[:, None, None]), 0.0)
    xi = jnp.exp((idx + 1.0)[None, :] * log_g[:, None])
    zeta = jnp.exp((chunk - 1.0 - idx)[None, :] * log_g[:, None])
    g_chunk = jnp.exp(chunk * log_g)
    return dmat, xi, zeta, g_chunk


def _pair_lanes(per_head):
    h, rows = per_head.shape
    half = LANES // 2
    x = jnp.broadcast_to(per_head.reshape(h // 2, 2, rows, 1), (h // 2, 2, rows, half))
    return x.transpose(0, 2, 1, 3).reshape(h // 2, rows, LANES)


def _rope_tables(pos):
    half = LANES // 4
    inv = ROPE_BASE ** (-jnp.arange(half, dtype=F32) / half)
    ang = pos.astype(F32)[:, None] * inv[None, :]
    cos, sin = jnp.cos(ang), jnp.sin(ang)
    return jnp.concatenate([cos] * 4, axis=1), jnp.concatenate([-sin, sin, -sin, sin], axis=1)


def _blockdiag_gate_weights(wa, wx):
    nh, hd, _ = wa.shape
    eye = jnp.eye(nh, dtype=wa.dtype)
    da = (eye[:, None, :, None] * wa[:, :, None, :]).reshape(nh * hd, nh * hd)
    dx = (eye[:, None, :, None] * wx[:, :, None, :]).reshape(nh * hd, nh * hd)
    return jnp.concatenate([da, dx], axis=1)


def _prompt_layer(x, n_seq, seq_len, layer, w_in, w_out, lw):
    w_a, w_b, w_c = lw["w_a"], lw["w_b"], lw["w_c"]
    n, d = x.shape
    d_in = w_in.shape[2]
    d_mix = w_a + w_b + w_c
    chunk = math.gcd(seq_len, RET_CHUNK)
    tc = ROUTE_TILE
    assert seq_len % tc == 0 and tc % chunk == 0 and tc % CONV_B_ROWS == 0
    n_t = seq_len // tc
    n_pair = w_c // LANES
    dmat, xi, zeta, g_chunk = _retention_tables(chunk)
    half = LANES // 2
    gm = jnp.zeros((n_pair, 2, half, 2, half), F32)
    gm = gm.at[:, 0, :, 0, :].set(g_chunk[0::2, None, None]).at[:, 1, :, 1, :].set(g_chunk[1::2, None, None])
    cos, sin = _rope_tables(jnp.arange(seq_len, dtype=jnp.int32))
    zeros = lambda *s: jnp.zeros(s, F32)
    full = lambda shape: _resident(shape, (0,) * len(shape))
    per_seq = lambda shape: pl.BlockSpec((1,) + shape, lambda b, t: (b,) + (0,) * len(shape))
    tile = lambda width: pl.BlockSpec((tc, width), lambda b, t: (b * n_t + t, 0))
    kern = functools.partial(_prompt_layer_kernel, tc=tc, w_a=w_a, w_b=w_b, w_c=w_c, chunk=chunk)
    return pl.pallas_call(
        kern,
        grid=(n_seq, n_t),
        in_specs=[
            tile(d), full((1, d)), _resident((1, d, d_in), (layer, 0, 0)),
            pl.BlockSpec((tc, LANES), lambda b, t: (t, 0)),
            pl.BlockSpec((tc, LANES), lambda b, t: (t, 0)),
            per_seq((SUBLANES, w_a)), per_seq((SUBLANES, w_a)), per_seq((4 * SUBLANES, w_b)),
            per_seq((n_pair, LANES, LANES)),
            full((CONV_A, w_a)), full((1, w_a)), full((w_a, 2 * w_a)), full((1, 2 * w_a)), full((1, w_a)),
            full((CONV_B, w_b)), full((1, w_b)), full((1, w_b)), full((1, w_b)), full((1, w_c)),
            full((NH_C, chunk, chunk)), full((n_pair, chunk, LANES)), full((n_pair, chunk, LANES)),
            full((n_pair, LANES, LANES)),
            _resident((1, d_mix, d), (layer, 0, 0)), full((1, d)),
            full((LANES, d)), full((LANES, d)), full((LANES, LANES)),
        ],
        out_specs=[
            tile(d), tile(d),
            pl.BlockSpec((SUBLANES, tc), lambda b, t: (0, b * n_t + t)),
            tile(LANES),
            pl.BlockSpec((SUBLANES, LANES), lambda b, t: (b * n_t + t, 0)),
            pl.BlockSpec((LANES, LANES), lambda b, t: (0, 0)),
            per_seq((SUBLANES, w_a)), per_seq((SUBLANES, w_a)), per_seq((4 * SUBLANES, w_b)),
            per_seq((n_pair, LANES, LANES)),
        ],
        out_shape=[
            jax.ShapeDtypeStruct((n, d), F32),
            jax.ShapeDtypeStruct((n, d), F32),
            jax.ShapeDtypeStruct((SUBLANES, n), jnp.int32),
            jax.ShapeDtypeStruct((n, LANES), F32),
            jax.ShapeDtypeStruct((n // tc * SUBLANES, LANES), jnp.int32),
            jax.ShapeDtypeStruct((LANES, LANES), F32),
            jax.ShapeDtypeStruct((n_seq, SUBLANES, w_a), F32),
            jax.ShapeDtypeStruct((n_seq, SUBLANES, w_a), F32),
            jax.ShapeDtypeStruct((n_seq, 4 * SUBLANES, w_b), F32),
            jax.ShapeDtypeStruct((n_seq, n_pair, LANES, LANES), F32),
        ],
        scratch_shapes=[
            pltpu.VMEM((d, d_in), BF16),
            pltpu.VMEM((d_mix, d), BF16),
            pltpu.VMEM((tc, d), BF16),
            pltpu.VMEM((tc, d_in), F32),
            pltpu.VMEM((tc, d_mix), BF16),
            pltpu.VMEM((SUBLANES + tc, w_a), F32),
            pltpu.VMEM((4 * SUBLANES + tc, w_b), F32),
            pltpu.VMEM((SUBLANES, 3 * SUBLANES + tc, w_b), F32),
            pltpu.VMEM((tc, w_a), F32),
            pltpu.VMEM((tc, w_a), F32),
            pltpu.VMEM((tc, w_a), F32),
            pltpu.VMEM((SUBLANES, w_a), F32),
            pltpu.VMEM((n_pair, LANES, LANES), F32),
            pltpu.VMEM((LANES, LANES), F32),
        ],
        compiler_params=_cparams("arbitrary", "arbitrary"),
        name="prompt_layer",
    )(x, lw["g_mix"], w_in, cos, sin,
      zeros(n_seq, SUBLANES, w_a), zeros(n_seq, SUBLANES, w_a), zeros(n_seq, 4 * SUBLANES, w_b),
      zeros(n_seq, n_pair, LANES, LANES),
      lw["conv_a_w"], lw["conv_a_b"], lw["wg"], lw["bg"], lw["lam"],
      lw["conv_b_w"], lw["conv_b_b"], lw["ln_g"], lw["ln_b"], lw["ret_g"],
      dmat, _pair_lanes(xi), _pair_lanes(zeta), gm.reshape(n_pair, LANES, LANES),
      w_out, lw["g_ffn"], lw["wr_hi"], lw["wr_lo"], lw["b_route"])


def _unpair_state(s_pairs):
    b, n_pair = s_pairs.shape[:2]
    half = LANES // 2
    s = s_pairs.reshape(b, n_pair, 2, half, 2, half)
    return jnp.stack([s[:, :, 0, :, 0, :], s[:, :, 1, :, 1, :]], axis=2).reshape(b, 2 * n_pair, half, half)


def _in_proj_kernel(x_ref, g_ref, w_ref, o_ref):
    u = _rms(x_ref[...], g_ref[...]).astype(BF16)
    o_ref[...] = jnp.dot(u, w_ref[0].astype(BF16), preferred_element_type=F32)


def _in_proj(x, g, w_in, layer):
    n, d = x.shape
    d_in = w_in.shape[2]
    tm = _row_tile(n)
    return pl.pallas_call(
        _in_proj_kernel,
        grid=(n // tm,),
        in_specs=[
            pl.BlockSpec((tm, d), lambda i: (i, 0)),
            _resident((1, d), (0, 0)),
            _resident((1, d, d_in), (layer, 0, 0)),
        ],
        out_specs=pl.BlockSpec((tm, d_in), lambda i: (i, 0)),
        out_shape=jax.ShapeDtypeStruct((n, d_in), F32),
        compiler_params=_cparams("arbitrary"),
        name="in_proj_sample",
    )(x, g, w_in)


def _mix_sample_ab_kernel(p_ref, h0_ref, bufa_ref, bufb_ref,
                          caw_ref, cab_ref, wg_ref, bg_ref, lam_ref,
                          cbw_ref, cbb_ref, lng_ref, lnb_ref,
                          o_ref, h_ref, bufa_out_ref, bufb_out_ref, *, ts, w_a, w_b):
    xpa = [bufa_ref[j] for j in range(CONV_A - 1)] + [p_ref[t, :, w_a:2 * w_a] for t in range(ts)]
    xa = []
    for t in range(ts):
        acc = cab_ref[...] + caw_ref[0:1, :] * xpa[t]
        for j in range(1, CONV_A):
            acc = acc + caw_ref[j:j + 1, :] * xpa[t + j]
        xa.append(acc)
    nb = xa[0].shape[0]
    a, bt = _lru_gates(jnp.concatenate(xa, axis=0), wg_ref[...], bg_ref[...], lam_ref[...])
    h = h0_ref[...]
    for t in range(ts):
        h = a[t * nb:(t + 1) * nb, :] * h + bt[t * nb:(t + 1) * nb, :]
        o_ref[t, :, 0:w_a] = (jax.nn.gelu(p_ref[t, :, 0:w_a]) * h).astype(BF16)
    h_ref[...] = h
    for j in range(CONV_A - 1):
        bufa_out_ref[j] = xpa[ts + j]
    b0 = 2 * w_a
    hist = CONV_B - 1
    glu = [p_ref[t, :, b0:b0 + w_b] * _sigmoid(p_ref[t, :, b0 + w_b:b0 + 2 * w_b]) for t in range(ts)]
    xpb = lambda i: bufb_ref[i] if i < hist else glu[i - hist]
    for t in range(ts):
        acc = cbb_ref[...] + cbw_ref[0:1, :] * xpb(t)
        for j in range(1, CONV_B):
            acc = acc + cbw_ref[j:j + 1, :] * xpb(t + j)
        o_ref[t, :, w_a:w_a + w_b] = _silu(_layernorm(acc, lng_ref[...], lnb_ref[...])).astype(BF16)
    for i in range(hist):
        bufb_out_ref[i] = xpb(ts + i)


def _mix_sample_ab(p_t, h0, bufa_t, bufb_t, layer, lw):
    ts, n_seq, width = p_t.shape
    w_a, w_b = lw["w_a"], lw["w_b"]
    nb = 32 if n_seq % 32 == 0 else n_seq
    full = lambda shape: _resident(shape, (0,) * len(shape))
    kern = functools.partial(_mix_sample_ab_kernel, ts=ts, w_a=w_a, w_b=w_b)
    return pl.pallas_call(
        kern,
        grid=(n_seq // nb,),
        in_specs=[
            pl.BlockSpec((ts, nb, width), lambda i: (0, i, 0)),
            pl.BlockSpec((None, nb, w_a), lambda i: (layer, i, 0)),
            pl.BlockSpec((CONV_A - 1, nb, w_a), lambda i: (0, i, 0)),
            pl.BlockSpec((CONV_B - 1, nb, w_b), lambda i: (0, i, 0)),
            full((CONV_A, w_a)), full((1, w_a)), full((w_a, 2 * w_a)), full((1, 2 * w_a)), full((1, w_a)),
            full((CONV_B, w_b)), full((1, w_b)), full((1, w_b)), full((1, w_b)),
        ],
        out_specs=[
            pl.BlockSpec((ts, nb, w_a + w_b), lambda i: (0, i, 0)),
            pl.BlockSpec((nb, w_a), lambda i: (i, 0)),
            pl.BlockSpec((CONV_A - 1, nb, w_a), lambda i: (0, i, 0)),
            pl.BlockSpec((CONV_B - 1, nb, w_b), lambda i: (0, i, 0)),
        ],
        out_shape=[
            jax.ShapeDtypeStruct((ts, n_seq, w_a + w_b), BF16),
            jax.ShapeDtypeStruct((n_seq, w_a), F32),
            jax.ShapeDtypeStruct((CONV_A - 1, n_seq, w_a), F32),
            jax.ShapeDtypeStruct((CONV_B - 1, n_seq, w_b), F32),
        ],
        compiler_params=_cparams("arbitrary"),
        name="mix_sample_ab",
    )(p_t, h0, bufa_t, bufb_t,
      lw["conv_a_w"], lw["conv_a_b"], lw["wg"], lw["bg"], lw["lam"],
      lw["conv_b_w"], lw["conv_b_b"], lw["ln_g"], lw["ln_b"])


def _mix_sample_c_kernel(q_ref, k_ref, v_ref, g_ref, oab_ref, s_ref, cos_ref, sin_ref,
                         dmat_ref, xi_ref, zeta_ref, gcol_ref, rng_ref, tile_ref, fold_ref, *rest,
                         rows, ts, w_c, w_ab, layer):
    if layer:
        prev_ref, mix_ref, sout_ref, qs_ref, kzs_ref, vs_ref, oi_ref = rest
        for l in range(layer):
            sout_ref[l] = prev_ref[l]
    else:
        mix_ref, sout_ref, qs_ref, kzs_ref, vs_ref, oi_ref = rest
    n_pair = w_c // LANES
    hd = LANES // 2
    lane = lax.broadcasted_iota(jnp.int32, (rows, LANES), 1)
    head0 = lane < hd
    first_half = (lane % hd) < (hd // 2)
    cos = cos_ref[...]
    sin = sin_ref[...]
    for p in range(n_pair):
        cs = slice(p * LANES, (p + 1) * LANES)
        q = _rope_pair(q_ref[:, cs], cos, sin, first_half)
        k = _rope_pair(k_ref[:, cs], cos, sin, first_half) * (1.0 / hd) ** 0.5
        v = v_ref[:, cs]
        qb, kb, vb = q.astype(BF16), k.astype(BF16), v.astype(BF16)
        oi_ref[:, cs] = _retention_intra_pair(qb, kb, vb, dmat_ref[2 * p], dmat_ref[2 * p + 1], head0)
        qs_ref[:, cs] = q
        kzs_ref[:, cs] = k * zeta_ref[:, cs]
        vs_ref[:, cs] = v
    grp = SUBLANES
    per_grp = grp // ts
    stack = NH_C * grp
    srow = lax.broadcasted_iota(jnp.int32, (stack, w_c), 0)
    slane = lax.broadcasted_iota(jnp.int32, (stack, w_c), 1)
    own_head = (srow // grp) == (slane // hd)
    tile_m = tile_ref[...]
    fold_m = fold_ref[...]
    for gi in range(rows // grp):
        rs = slice(gi * grp, (gi + 1) * grp)
        q8 = jnp.concatenate([qs_ref[rs, :]] * NH_C, axis=0)
        kz8 = jnp.concatenate([kzs_ref[rs, :]] * NH_C, axis=0)
        v8 = jnp.concatenate([vs_ref[rs, :]] * NH_C, axis=0)
        vr = jnp.dot(jnp.where(own_head, v8, 0.0).astype(BF16), fold_m, preferred_element_type=F32).astype(BF16)
        o_stack = jnp.zeros((stack, hd), F32)
        for s2 in range(per_grp):
            sq = gi * per_grp + s2
            own_seq = ((srow % grp) // ts) == s2
            sel = jnp.logical_and(own_head, own_seq)
            s_old = s_ref[0, sq].reshape(w_c, hd)
            o_stack = o_stack + jnp.dot(jnp.where(sel, q8, 0.0).astype(BF16), s_old.astype(BF16),
                                        preferred_element_type=F32)
            upd = lax.dot_general(jnp.where(sel, kz8, 0.0).astype(BF16), vr, (((0,), (0,)), ((), ())),
                                  preferred_element_type=F32)
            sout_ref[layer, sq] = (gcol_ref[...] * s_old + upd).reshape(NH_C, hd, hd)
        p1 = o_stack.astype(BF16)
        r1 = o_stack - p1.astype(F32)
        p2 = r1.astype(BF16)
        p3 = (r1 - p2.astype(F32)).astype(BF16)
        spread = (jnp.dot(p1, tile_m, preferred_element_type=F32)
                  + jnp.dot(p2, tile_m, preferred_element_type=F32)
                  + jnp.dot(p3, tile_m, preferred_element_type=F32))
        spread = jnp.where(own_head, spread, 0.0).reshape(NH_C, grp, w_c)
        oi_ref[rs, :] = oi_ref[rs, :] + jnp.sum(spread, axis=0) * xi_ref[rs, :]
    mix_ref[:, 0:w_ab] = oab_ref[...]
    for p in range(n_pair):
        cs = slice(p * LANES, (p + 1) * LANES)
        on = _head_norm_pair(oi_ref[:, cs], head0, 1.0 / hd) * rng_ref[:, cs]
        mix_ref[:, w_ab + p * LANES:w_ab + (p + 1) * LANES] = (_silu(g_ref[:, cs]) * on).astype(BF16)


def _mix_sample_c(proj, n_seq, ts, oab, state, layer, prev_states, lw):
    w_a, w_b, w_c = lw["w_a"], lw["w_b"], lw["w_c"]
    hd = LANES // 2
    n_rows = n_seq * ts
    rows = 64 if n_rows % 64 == 0 else n_rows
    assert SUBLANES % ts == 0 and rows % SUBLANES == 0
    seq_per = rows // ts
    c0 = 2 * w_a + 2 * w_b
    assert c0 % w_c == 0
    cb = c0 // w_c
    log_g = jnp.log1p(-jnp.exp2(-5.0 - jnp.arange(NH_C, dtype=F32)))
    r = jnp.arange(rows)
    step = (r % ts).astype(F32)
    diff = step[:, None] - step[None, :]
    live = jnp.logical_and((r[:, None] // ts) == (r[None, :] // ts), diff >= 0)
    dmat = jnp.where(live[None], jnp.exp(jnp.where(live, diff, 0.0)[None] * log_g[:, None, None]), 0.0)
    xi = jnp.repeat(jnp.exp((step + 1.0)[None, :] * log_g[:, None]).T, hd, axis=1)
    zeta = jnp.repeat(jnp.exp((ts - 1.0 - step)[None, :] * log_g[:, None]).T, hd, axis=1)
    gcol = jnp.broadcast_to(jnp.repeat(jnp.exp(ts * log_g), hd)[:, None], (w_c, hd))
    pos = PAST_LEN + (r % ts).astype(jnp.int32)
    cos, sin = _rope_tables(pos)
    tile_m = jnp.tile(jnp.eye(hd, dtype=BF16), (1, NH_C))
    fold_m = jnp.tile(jnp.eye(hd, dtype=BF16), (NH_C, 1))
    full = lambda shape: _resident(shape, (0,) * len(shape))
    col = lambda j: pl.BlockSpec((rows, w_c), lambda i: (i, cb + j))
    states = lambda n_layers, first: pl.BlockSpec((n_layers, seq_per, NH_C, hd, hd), lambda i: (first, i, 0, 0, 0))
    kern = functools.partial(_mix_sample_c_kernel, rows=rows, ts=ts, w_c=w_c, w_ab=w_a + w_b, layer=layer)
    in_specs = [
        col(0), col(1), col(2), col(3),
        pl.BlockSpec((rows, w_a + w_b), lambda i: (i, 0)),
        states(1, layer),
        full((rows, LANES)), full((rows, LANES)),
        full((NH_C, rows, rows)), full((rows, w_c)), full((rows, w_c)), full((w_c, hd)), full((1, w_c)),
        full((hd, w_c)), full((w_c, hd)),
    ]
    args = [proj, proj, proj, proj, oab, state, cos, sin, dmat, xi, zeta, gcol, lw["ret_g"], tile_m, fold_m]
    if layer:
        in_specs.append(states(layer, 0))
        args.append(prev_states)
    return pl.pallas_call(
        kern,
        grid=(n_rows // rows,),
        in_specs=in_specs,
        out_specs=[
            pl.BlockSpec((rows, w_a + w_b + w_c), lambda i: (i, 0)),
            states(layer + 1, 0),
        ],
        out_shape=[
            jax.ShapeDtypeStruct((n_rows, w_a + w_b + w_c), BF16),
            jax.ShapeDtypeStruct((layer + 1,) + state.shape[1:], F32),
        ],
        scratch_shapes=[pltpu.VMEM((rows, w_c), F32)] * 4,
        compiler_params=_cparams("arbitrary"),
        name="mix_sample_c",
    )(*args)


def _out_proj_kernel(x_ref, mix_ref, w_ref, g_ref, wrh_ref, wrl_ref, br_ref, seen_ref,
                     xmid_ref, xn_ref, rt_ref, rg_ref, ts_ref, counts_ref, carry_ref):
    @pl.when(pl.program_id(0) == 0)
    def _():
        carry_ref[...] = seen_ref[...]

    xm = x_ref[...] + jnp.dot(mix_ref[...], w_ref[0].astype(BF16), preferred_element_type=F32)
    xmid_ref[...] = xm
    _ffn_norm_route(xm, g_ref[...], wrh_ref[...], wrl_ref[...], br_ref[...], carry_ref,
                    xn_ref, rt_ref, rg_ref, ts_ref, counts_ref)


def _out_proj(x, mix, w_out, layer, seen, lw):
    n, d = x.shape
    d_mix = mix.shape[1]
    tm = ROUTE_TILE
    assert n % tm == 0
    full = lambda shape: _resident(shape, (0,) * len(shape))
    rows = lambda width: pl.BlockSpec((tm, width), lambda i: (i, 0))
    return pl.pallas_call(
        _out_proj_kernel,
        grid=(n // tm,),
        in_specs=[rows(d), rows(d_mix), _resident((1, d_mix, d), (layer, 0, 0)),
                  full((1, d)), full((LANES, d)), full((LANES, d)), full((LANES, LANES)), full((LANES, LANES))],
        out_specs=[rows(d), rows(d), pl.BlockSpec((SUBLANES, tm), lambda i: (0, i)), rows(LANES),
                   pl.BlockSpec((SUBLANES, LANES), lambda i: (i, 0)),
                   pl.BlockSpec((LANES, LANES), lambda i: (0, 0))],
        out_shape=[
            jax.ShapeDtypeStruct((n, d), F32),
            jax.ShapeDtypeStruct((n, d), F32),
            jax.ShapeDtypeStruct((SUBLANES, n), jnp.int32),
            jax.ShapeDtypeStruct((n, LANES), F32),
            jax.ShapeDtypeStruct((n // tm * SUBLANES, LANES), jnp.int32),
            jax.ShapeDtypeStruct((LANES, LANES), F32),
        ],
        scratch_shapes=[pltpu.VMEM((LANES, LANES), F32)],
        compiler_params=_cparams("arbitrary"),
        name="out_proj_sample",
    )(x, mix, w_out, lw["g_ffn"], lw["wr_hi"], lw["wr_lo"], lw["b_route"], seen)


SLOT_TILE = 256


def _slot_plan(counts, n_tokens):
    n_runs = n_tokens // ROUTE_TILE * N_EXPERTS
    n_slots = 2 * n_tokens + n_runs * (SUBLANES - 1) + N_EXPERTS * SLOT_TILE
    n_slots = -(-n_slots // SLOT_TILE) * SLOT_TILE
    n_tiles = n_slots // SLOT_TILE
    padded = (counts + (SLOT_TILE - 1)) // SLOT_TILE * SLOT_TILE
    ends = jnp.cumsum(padded)
    offs = ends - padded
    total = ends[-1]
    tile_start = jnp.arange(n_tiles, dtype=jnp.int32) * SLOT_TILE
    probe = jnp.minimum(tile_start, total - 1)
    tile_expert = jnp.sum((ends[None, :] <= probe[:, None]).astype(jnp.int32), axis=1)
    tile_expert = jnp.minimum(tile_expert, N_EXPERTS - 1)
    n_valid = (total // SLOT_TILE).reshape(1)
    last_tile = jnp.where(padded > 0, ends - SLOT_TILE, 0)
    return n_slots, offs, tile_expert, n_valid, last_tile, (padded > 0).astype(jnp.int32)


RUN_SIZES = tuple(1 << b for b in range(ROUTE_TILE.bit_length() - 1, 2, -1))
SORT_ROWS = 6 * LANES
assert SORT_ROWS >= 2 * ROUTE_TILE + N_EXPERTS * (SUBLANES - 1)


def _start_runs(cnt_ref, loc_ref, slot_ref, copy):
    for e in range(N_EXPERTS):
        n = cnt_ref[e]
        loc = pl.multiple_of(loc_ref[e], SUBLANES)
        slot = pl.multiple_of(slot_ref[e], SUBLANES)
        for size in RUN_SIZES:
            take = (n & size) != 0

            @pl.when(take)
            def _():
                copy(loc, slot, size).start()

            step = jnp.where(take, size, 0)
            loc = pl.multiple_of(loc + step, SUBLANES)
            slot = pl.multiple_of(slot + step, SUBLANES)
    return loc_ref[N_EXPERTS - 1] + cnt_ref[N_EXPERTS - 1]


def _wait_rows(rows, group_copy):
    lax.fori_loop(0, rows // SUBLANES, lambda g, c: (group_copy.wait(), c)[1], 0)


def _pair_onehots(rt_ref, n_pairs):
    tokens = rt_ref.shape[1]
    r = lax.broadcasted_iota(jnp.int32, (n_pairs, tokens), 0)
    return r == rt_ref[2:3, :], r == rt_ref[3:4, :]


def _dispatch_kernel(last_ref, used_ref, nv_ref, cnt_ref, loc_ref, slot_ref, rt_ref, xnp_ref, xns_ref,
                     rgp_ref, rgs_ref, xs_ref, sorted_ref, zero_ref, zsem, sem, *, n_prompt_tiles):
    tm, d = xnp_ref.shape
    n_tiles = xs_ref.shape[0] // SLOT_TILE
    i = pl.program_id(0)

    @pl.when(i == 0)
    def _():
        zero_ref[...] = jnp.zeros_like(zero_ref)
        tile = lambda row: pltpu.make_async_copy(
            zero_ref, xs_ref.at[pl.ds(pl.multiple_of(row, SLOT_TILE), SLOT_TILE), :], zsem)
        for e in range(N_EXPERTS):
            @pl.when(used_ref[e] != 0)
            def _():
                tile(last_ref[e]).start()
        lax.fori_loop(nv_ref[0], n_tiles, lambda j, c: (tile(j * SLOT_TILE).start(), c)[1], 0)
        for e in range(N_EXPERTS):
            @pl.when(used_ref[e] != 0)
            def _():
                tile(0).wait()
        lax.fori_loop(nv_ref[0], n_tiles, lambda j, c: (tile(0).wait(), c)[1], 0)

    is_prompt = i < n_prompt_tiles
    xn = jnp.where(is_prompt, xnp_ref[...], xns_ref[...]).astype(BF16)
    rg = jnp.where(is_prompt, rgp_ref[...], rgs_ref[...])
    first, second = _pair_onehots(rt_ref, SORT_ROWS)
    sorted_ref[:, 0:d] = jnp.dot(jnp.logical_or(first, second).astype(BF16), xn, preferred_element_type=F32)
    gate = jnp.zeros((SORT_ROWS, LANES), F32)
    for onehot, lane in ((first, 0), (second, 1)):
        part = jnp.broadcast_to(rg[:, lane:lane + 1], (tm, LANES))
        for _ in range(3):
            hi = part.astype(BF16)
            gate = gate + jnp.dot(onehot.astype(BF16), hi, preferred_element_type=F32)
            part = part - hi.astype(F32)
    sorted_ref[:, d:d + LANES] = gate
    run = lambda loc, slot, size: pltpu.make_async_copy(
        sorted_ref.at[pl.ds(loc, size), :], xs_ref.at[pl.ds(slot, size), :], sem)
    _wait_rows(_start_runs(cnt_ref, loc_ref, slot_ref, run), run(0, 0, SUBLANES))


def _dispatch(xn_p, xn_s, rg_p, rg_s, rt, cnt, loc, slot, n_slots, last_tile, used, n_valid):
    n_p, d = xn_p.shape
    n_s = xn_s.shape[0]
    tm = ROUTE_TILE
    tp = n_p // tm
    ts_last = n_s // tm - 1
    per_tile = lambda: pl.BlockSpec((LANES,), lambda i, *_: (i,), memory_space=pltpu.SMEM)
    prompt = lambda width: pl.BlockSpec((tm, width), lambda i, *_: (jnp.minimum(i, tp - 1), 0))
    sample = lambda width: pl.BlockSpec((tm, width), lambda i, *_: (jnp.clip(i - tp, 0, ts_last), 0))
    grid_spec = pltpu.PrefetchScalarGridSpec(
        num_scalar_prefetch=3,
        grid=((n_p + n_s) // tm,),
        in_specs=[per_tile(), per_tile(), per_tile(),
                  pl.BlockSpec((SUBLANES, tm), lambda i, *_: (0, i)),
                  prompt(d), sample(d), prompt(LANES), sample(LANES)],
        out_specs=pl.BlockSpec(memory_space=pl.ANY),
        scratch_shapes=[pltpu.VMEM((SORT_ROWS, d + LANES), F32), pltpu.VMEM((SLOT_TILE, d + LANES), F32),
                        pltpu.SemaphoreType.DMA, pltpu.SemaphoreType.DMA],
    )
    return pl.pallas_call(
        functools.partial(_dispatch_kernel, n_prompt_tiles=tp),
        grid_spec=grid_spec,
        out_shape=jax.ShapeDtypeStruct((n_slots, d + LANES), F32),
        compiler_params=_cparams("arbitrary"),
        name="moe_dispatch",
    )(last_tile, used, n_valid, cnt, loc, slot, rt, xn_p, xn_s, rg_p, rg_s)


def _experts_kernel(te_ref, nv_ref, xs_ref, w1_ref, w3_ref, w2_ref, ys_ref, w1b_ref, w3b_ref, w2b_ref):
    j = pl.program_id(0)
    fresh = jnp.logical_or(j == 0, te_ref[j] != te_ref[jnp.maximum(j - 1, 0)])

    @pl.when(fresh)
    def _():
        w1b_ref[...] = w1_ref[0, 0].astype(BF16)
        w3b_ref[...] = w3_ref[0, 0].astype(BF16)
        w2b_ref[...] = w2_ref[0, 0].astype(BF16)

    @pl.when(j < nv_ref[0])
    def _():
        d = ys_ref.shape[1]
        x = xs_ref[:, 0:d].astype(BF16)
        h1 = jnp.dot(x, w1b_ref[...], preferred_element_type=F32)
        h3 = jnp.dot(x, w3b_ref[...], preferred_element_type=F32)
        h = (_silu(h1) * h3).astype(BF16)
        ys_ref[...] = xs_ref[:, d:d + 1] * jnp.dot(h, w2b_ref[...], preferred_element_type=F32)

    @pl.when(j >= nv_ref[0])
    def _():
        ys_ref[...] = jnp.zeros_like(ys_ref)


def _experts(xs, tile_expert, n_valid, w1, w3, w2, layer):
    n_slots = xs.shape[0]
    d, d_e = w1.shape[2:]
    tile = lambda j, te, nv: (jnp.minimum(j, nv[0] - 1), 0)
    expert = lambda j, te, nv: (layer, te[j], 0, 0)
    grid_spec = pltpu.PrefetchScalarGridSpec(
        num_scalar_prefetch=2,
        grid=(n_slots // SLOT_TILE,),
        in_specs=[
            pl.BlockSpec((SLOT_TILE, d + LANES), tile),
            pl.BlockSpec((1, 1, d, d_e), expert),
            pl.BlockSpec((1, 1, d, d_e), expert),
            pl.BlockSpec((1, 1, d_e, d), expert),
        ],
        out_specs=pl.BlockSpec((SLOT_TILE, d), lambda j, te, nv: (j, 0)),
        scratch_shapes=[pltpu.VMEM((d, d_e), BF16), pltpu.VMEM((d, d_e), BF16), pltpu.VMEM((d_e, d), BF16)],
    )
    return pl.pallas_call(
        _experts_kernel,
        grid_spec=grid_spec,
        out_shape=jax.ShapeDtypeStruct((n_slots, d), F32),
        compiler_params=_cparams("arbitrary"),
        name="moe_experts",
    )(tile_expert, n_valid, xs, w1, w3, w2)


def _combine_kernel(cnt_ref, loc_ref, slot_ref, rt_ref, xmid_ref, gfin_ref, ys_ref, o_ref, y_ref, sem,
                    *, final_norm):
    run = lambda loc, slot, size: pltpu.make_async_copy(
        ys_ref.at[pl.ds(slot, size), :], y_ref.at[pl.ds(loc, size), :], sem)
    rows = _start_runs(cnt_ref, loc_ref, slot_ref, run)
    _wait_rows(rows, run(0, 0, SUBLANES))
    first, second = _pair_onehots(rt_ref, SORT_ROWS)
    pick = jnp.logical_or(first, second).astype(BF16)
    fetched = lax.broadcasted_iota(jnp.int32, y_ref.shape, 0) < rows
    part = jnp.where(fetched, y_ref[...], 0.0)
    out = xmid_ref[...]
    for _ in range(3):
        hi = part.astype(BF16)
        out = out + lax.dot_general(pick, hi, (((0,), (0,)), ((), ())), preferred_element_type=F32)
        part = part - hi.astype(F32)
    o_ref[...] = _rms(out, gfin_ref[...]) if final_norm else out


def _combine(xmid, rt, cnt, loc, slot, row0, ys, g_final, final_norm):
    n, d = xmid.shape
    tm = ROUTE_TILE
    t0 = row0 // tm
    per_tile = lambda: pl.BlockSpec((LANES,), lambda i: (t0 + i,), memory_space=pltpu.SMEM)
    kern = functools.partial(_combine_kernel, final_norm=final_norm)
    return pl.pallas_call(
        kern,
        grid=(n // tm,),
        in_specs=[per_tile(), per_tile(), per_tile(),
                  pl.BlockSpec((SUBLANES, tm), lambda i: (0, t0 + i)),
                  pl.BlockSpec((tm, d), lambda i: (i, 0)),
                  _resident((1, d), (0, 0)),
                  pl.BlockSpec(memory_space=pl.ANY)],
        out_specs=pl.BlockSpec((tm, d), lambda i: (i, 0)),
        out_shape=jax.ShapeDtypeStruct((n, d), F32),
        scratch_shapes=[pltpu.VMEM((SORT_ROWS, d), F32), pltpu.SemaphoreType.DMA],
        compiler_params=_cparams("arbitrary"),
        name="moe_combine",
    )(cnt, loc, slot, rt, xmid, g_final, ys)


def _moe_routed(prompt, sample, counts, w1, w3, w2, layer, g_final, final_norm):
    xmid_p, xn_p, rt_p, rg_p, ts_p = prompt
    xmid_s, xn_s, rt_s, rg_s, ts_s = sample
    n_p = xmid_p.shape[0]
    n = n_p + xmid_s.shape[0]
    n_slots, offs, tile_expert, n_valid, last_tile, used = _slot_plan(
        counts[:N_EXPERTS, 0].astype(jnp.int32), n)
    rt = jnp.concatenate([rt_p, rt_s], axis=1)
    ts = jnp.concatenate([ts_p, ts_s], axis=0).reshape(n // ROUTE_TILE, SUBLANES, LANES)
    before, cnt = ts[:, 0, :], ts[:, 1, :]
    slot = (jnp.pad(offs, (0, LANES - N_EXPERTS))[None, :] + before).reshape(-1)
    loc = (jnp.cumsum(cnt, axis=1) - cnt).reshape(-1)
    cnt = cnt.reshape(-1)
    xs = _dispatch(xn_p, xn_s, rg_p, rg_s, rt, cnt, loc, slot, n_slots, last_tile, used, n_valid)
    ys = _experts(xs, tile_expert, n_valid, w1, w3, w2, layer)
    y_p = _combine(xmid_p, rt, cnt, loc, slot, 0, ys, g_final, final_norm)
    y_s = _combine(xmid_s, rt, cnt, loc, slot, n_p, ys, g_final, final_norm)
    return y_p, y_s


def kernel(x_prompt, x_sample, state_lru_h, state_lru_conv, state_conv, state_ret, norm_mix_g, norm_ffn_g, final_norm_g, w_in, lru_conv_w, lru_conv_b, lru_wa, lru_ba, lru_wx, lru_bx, lru_lambda, cm_dw_w, cm_dw_b, cm_ln_g, cm_ln_b, ret_norm_g, w_out, moe_w_group, moe_b_group, moe_w_expert, moe_b_expert, moe_w1, moe_w3, moe_w2):
    bp, tp, d = x_prompt.shape
    bs, ts, _ = x_sample.shape
    depth = w_in.shape[0]
    w_a = lru_lambda.shape[1]
    w_b = cm_dw_b.shape[1]
    w_c = ret_norm_g.shape[1]
    n_p, n_s = bp * tp, bs * ts
    row = lambda v: v.reshape(1, -1)

    x_p, x_s = x_prompt.reshape(n_p, d), x_sample.reshape(n_s, d)
    outs = {k: [] for k in ("p_h", "p_ca", "p_cb", "p_s", "s_h", "s_ca", "s_cb")}
    s_states = None
    for l in range(depth):
        wr_hi, wr_lo, b_route = _route_weights(moe_w_group[l], moe_b_group[l], moe_w_expert[l], moe_b_expert[l])
        lw = dict(
            w_a=w_a, w_b=w_b, w_c=w_c, g_mix=row(norm_mix_g[l]), g_ffn=row(norm_ffn_g[l]),
            conv_a_w=lru_conv_w[l], conv_a_b=row(lru_conv_b[l]),
            wg=_blockdiag_gate_weights(lru_wa[l], lru_wx[l]).astype(BF16),
            bg=row(jnp.concatenate([lru_ba[l], lru_bx[l]])), lam=row(lru_lambda[l]),
            conv_b_w=cm_dw_w[l], conv_b_b=row(cm_dw_b[l]), ln_g=row(cm_ln_g[l]), ln_b=row(cm_ln_b[l]),
            ret_g=row(ret_norm_g[l]), wr_hi=wr_hi, wr_lo=wr_lo, b_route=b_route,
        )
        xmid_p, xn_p, rt_p, rg_p, ts_p, seen, h_p, ca_p, cb_p, s_p = _prompt_layer(
            x_p, bp, tp, l, w_in, w_out, lw)
        outs["p_h"].append(h_p[:, 0, :])
        outs["p_ca"].append(ca_p[:, SUBLANES - (CONV_A - 1):, :])
        outs["p_cb"].append(cb_p[:, 4 * SUBLANES - (CONV_B - 1):, :])
        outs["p_s"].append(_unpair_state(s_p))
        proj_s = _in_proj(x_s, lw["g_mix"], w_in, l)
        p_t = proj_s[:, :2 * w_a + 2 * w_b].reshape(bs, ts, -1).transpose(1, 0, 2)
        oab_t, h_s, ca_t, cb_t = _mix_sample_ab(
            p_t, state_lru_h, state_lru_conv[l].transpose(1, 0, 2), state_conv[l].transpose(1, 0, 2), l, lw)
        oab = oab_t.transpose(1, 0, 2).reshape(n_s, w_a + w_b)
        mix_s, s_states = _mix_sample_c(proj_s, bs, ts, oab, state_ret, l, s_states, lw)
        outs["s_h"].append(h_s)
        outs["s_ca"].append(ca_t.transpose(1, 0, 2))
        outs["s_cb"].append(cb_t.transpose(1, 0, 2))
        xmid_s, xn_s, rt_s, rg_s, ts_s, counts = _out_proj(x_s, mix_s, w_out, l, seen, lw)
        x_p, x_s = _moe_routed((xmid_p, xn_p, rt_p, rg_p, ts_p), (xmid_s, xn_s, rt_s, rg_s, ts_s), counts,
                               moe_w1, moe_w3, moe_w2, l, row(final_norm_g), l == depth - 1)
    st = lambda k: jnp.stack(outs[k])
    return (x_p.reshape(bp, tp, d), x_s.reshape(bs, ts, d), st("p_h"), st("p_ca"), st("p_cb"), st("p_s"),
            st("s_h"), st("s_ca"), st("s_cb"), s_states)
```

```python
import functools
import math

import jax
import jax.numpy as jnp
from jax import lax
from jax.experimental import pallas as pl
from jax.experimental.pallas import tpu as pltpu

PAST_LEN = 16384
LRU_C = 8.0
CONV_A = 4
CONV_B = 31
NH_C = 8
RET_CHUNK = 128
ROPE_BASE = 10000.0
N_GROUPS = 4
EXPERTS_PER_GROUP = 8
N_EXPERTS = N_GROUPS * EXPERTS_PER_GROUP
EPS = 1e-6

LANES = 128
SUBLANES = 8
VMEM_LIMIT_BYTES = 56 * 1024 * 1024

F32 = jnp.float32
BF16 = jnp.bfloat16


def _cparams(*sem):
    return pltpu.CompilerParams(dimension_semantics=sem, vmem_limit_bytes=VMEM_LIMIT_BYTES)


def _row_tile(n):
    for t in (512, 256, 128, 64, 32, 16, 8):
        if n % t == 0:
            return t
    raise ValueError(f"row count {n} must be a multiple of 8")


def _resident(shape, index):
    return pl.BlockSpec(shape, lambda *_: index, pipeline_mode=pl.Buffered(1))


def _rms(x, g):
    ms = jnp.mean(x * x, axis=-1, keepdims=True)
    return x * lax.rsqrt(ms + EPS) * g


def _sigmoid(x):
    return jax.nn.sigmoid(x)


def _silu(x):
    return x * jax.nn.sigmoid(x)


def _layernorm(x, g, b):
    mu = jnp.mean(x, axis=-1, keepdims=True)
    xc = x - mu
    var = jnp.mean(xc * xc, axis=-1, keepdims=True)
    return xc * lax.rsqrt(var + EPS) * g + b


def _rope_pair(x, cos, sin_signed, first_half):
    partner = jnp.where(first_half, pltpu.roll(x, 96, 1), pltpu.roll(x, 32, 1))
    return x * cos + partner * sin_signed


def _head_norm_pair(o, head0, inv_d):
    zero = jnp.zeros_like(o)
    s0 = jnp.sum(jnp.where(head0, o, zero), axis=-1, keepdims=True)
    s1 = jnp.sum(jnp.where(head0, zero, o), axis=-1, keepdims=True)
    oc = o - jnp.where(head0, s0, s1) * inv_d
    q = oc * oc
    v0 = jnp.sum(jnp.where(head0, q, zero), axis=-1, keepdims=True)
    v1 = jnp.sum(jnp.where(head0, zero, q), axis=-1, keepdims=True)
    return oc * lax.rsqrt(jnp.where(head0, v0, v1) * inv_d + EPS)


def _retention_intra_pair(qb, kb, vb, dmat0, dmat1, head0):
    outs = []
    for j, dm in ((0, dmat0), (1, dmat1)):
        keep = head0 if j == 0 else jnp.logical_not(head0)
        qm = jnp.where(keep, qb, jnp.zeros_like(qb))
        s = lax.dot_general(qm, kb, (((1,), (1,)), ((), ())), preferred_element_type=F32)
        inner = (s * dm).astype(BF16)
        outs.append(jnp.dot(inner, vb, preferred_element_type=F32))
    return jnp.where(head0, outs[0], outs[1])


def _lru_gates(xa, wg, bg, lam):
    w_a = xa.shape[-1]
    gates = jnp.dot(xa.astype(BF16), wg, preferred_element_type=F32) + bg
    r = _sigmoid(gates[:, :w_a])
    i = _sigmoid(gates[:, w_a:])
    z = -lam
    softplus = jnp.maximum(z, 0.0) + jnp.log1p(jnp.exp(-jnp.abs(z)))
    log_a = -LRU_C * r * softplus
    a = jnp.exp(log_a)
    bt = jnp.sqrt(-jnp.tanh(log_a) * (a * a + 1.0)) * (i * xa)
    return a, bt


ROUTE_TILE = 256
EXPERT_ROW0 = SUBLANES
assert EXPERTS_PER_GROUP == SUBLANES and N_GROUPS <= SUBLANES and EXPERT_ROW0 + N_EXPERTS <= LANES


def _split_bf16(x):
    hi = x.astype(BF16)
    return hi, (x - hi.astype(F32)).astype(BF16)


def _lane_tile(x, width):
    return x if width == x.shape[1] else jnp.concatenate([x] * (width // x.shape[1]), axis=1)


def _route_weights(w_group, b_group, w_expert, b_expert):
    d = w_group.shape[0]
    w = jnp.zeros((LANES, d), F32).at[:N_GROUPS].set(w_group.T).at[EXPERT_ROW0:EXPERT_ROW0 + N_EXPERTS].set(w_expert.T)
    b = jnp.zeros((LANES,), F32).at[:N_GROUPS].set(b_group).at[EXPERT_ROW0:EXPERT_ROW0 + N_EXPERTS].set(b_expert)
    hi, lo = _split_bf16(w)
    return hi, lo, jnp.broadcast_to(b[:, None], (LANES, LANES))


def _ffn_norm_route(xm, g_ffn, wr_hi, wr_lo, b_route, carry_ref, xn_ref, rt_ref, rg_ref, counts_ref):
    xn = _rms(xm, g_ffn)
    xn_ref[...] = xn
    tm = xn.shape[0]
    xh, xl = _split_bf16(xn)
    dot_t = lambda w, x: lax.dot_general(w, x, (((1,), (1,)), ((), ())), preferred_element_type=F32)
    lt = dot_t(wr_hi, xh) + (dot_t(wr_hi, xl) + dot_t(wr_lo, xh)) + _lane_tile(b_route, tm)
    row8 = lax.broadcasted_iota(jnp.int32, (SUBLANES, tm), 0)
    neg = jnp.full((SUBLANES, tm), -jnp.inf, F32)
    big = jnp.full((SUBLANES, tm), LANES, jnp.int32)
    first = lambda hit: jnp.min(jnp.where(hit, row8, big), axis=0, keepdims=True)
    gl = jnp.where(row8 < N_GROUPS, lt[0:SUBLANES, :], neg)
    gmax = jnp.max(gl, axis=0, keepdims=True)
    p_top = 1.0 / jnp.sum(jnp.exp(gl - gmax), axis=0, keepdims=True)
    g_sel = first(gl == gmax)
    slab = lambda g: lt[EXPERT_ROW0 + g * SUBLANES:EXPERT_ROW0 + (g + 1) * SUBLANES, :]
    el = slab(N_GROUPS - 1)
    for g in reversed(range(N_GROUPS - 1)):
        el = jnp.where(g_sel == g, slab(g), el)
    v1 = jnp.max(el, axis=0, keepdims=True)
    i1 = first(el == v1)
    el2 = jnp.where(row8 == i1, neg, el)
    v2 = jnp.max(el2, axis=0, keepdims=True)
    i2 = first(el2 == v2)
    ex = jnp.exp(v2 - v1)
    den = 1.0 + ex
    g1 = p_top * (1.0 / den)
    g2 = p_top * (ex / den)
    e1 = g_sel * EXPERTS_PER_GROUP + i1
    e2 = g_sel * EXPERTS_PER_GROUP + i2
    rowe = lax.broadcasted_iota(jnp.int32, (LANES, tm), 0)
    oh1 = rowe == e1
    oh2 = rowe == e2
    onehot = jnp.logical_or(oh1, oh2).astype(BF16)
    earlier = (lax.broadcasted_iota(jnp.int32, (tm, tm), 0)
               < lax.broadcasted_iota(jnp.int32, (tm, tm), 1)).astype(BF16)
    seen = _lane_tile(carry_ref[...], tm) + jnp.dot(onehot, earlier, preferred_element_type=F32)
    rank1 = jnp.sum(jnp.where(oh1, seen, 0.0), axis=0, keepdims=True).astype(jnp.int32)
    rank2 = jnp.sum(jnp.where(oh2, seen, 0.0), axis=0, keepdims=True).astype(jnp.int32)
    total = carry_ref[...] + jnp.sum(onehot.astype(F32), axis=1, keepdims=True)
    carry_ref[...] = total
    counts_ref[...] = total
    zi = jnp.zeros((SUBLANES, tm), jnp.int32)
    rt_ref[...] = jnp.where(row8 == 0, e1, jnp.where(row8 == 1, e2, jnp.where(
        row8 == 2, rank1, jnp.where(row8 == 3, rank2, zi))))
    rg_ref[...] = jnp.where(rowe == 0, g1, jnp.where(rowe == 1, g2, 0.0)).T


CONV_B_ROWS = 64


def _prompt_layer_kernel(x_ref, gmix_ref, win_ref, cos_ref, sin_ref, h0_ref, bufa0_ref, bufb0_ref, s0_ref,
                         caw_ref, cab_ref, wg_ref, bg_ref, lam_ref,
                         cbw_ref, cbb_ref, lng_ref, lnb_ref, rng_ref,
                         dmat_ref, xi_ref, zeta_ref, gmat_ref,
                         wout_ref, gffn_ref, wrh_ref, wrl_ref, br_ref,
                         xmid_ref, xn_ref, rt_ref, rg_ref, counts_ref,
                         hout_ref, bufa_out_ref, bufb_out_ref, sout_ref,
                         winb_ref, woutb_ref, u_ref, proj_ref, mix_ref,
                         xpa_ref, xpb_ref, sh_ref, a_ref, b_ref, hseq_ref, hcar_ref, s_ref, carry_ref,
                         *, tc, w_a, w_b, w_c, chunk):
    t = pl.program_id(1)
    n_t = pl.num_programs(1)
    pa = SUBLANES
    pb = 4 * SUBLANES

    @pl.when(jnp.logical_and(pl.program_id(0) == 0, t == 0))
    def _():
        winb_ref[...] = win_ref[0].astype(BF16)
        woutb_ref[...] = wout_ref[0].astype(BF16)
        carry_ref[...] = jnp.zeros_like(carry_ref)

    @pl.when(t == 0)
    def _():
        xpa_ref[0:pa, :] = bufa0_ref[0]
        xpb_ref[0:pb, :] = bufb0_ref[0]
        hcar_ref[...] = h0_ref[0]
        s_ref[...] = s0_ref[0]

    u = _rms(x_ref[...], gmix_ref[...]).astype(BF16)
    u_ref[...] = u

    def in_proj(c0, c1):
        for c in range(c0, c1, 4 * LANES):
            ce = min(c + 4 * LANES, c1)
            proj_ref[:, c:ce] = jnp.dot(u_ref[...], winb_ref[:, c:ce], preferred_element_type=F32)

    in_proj(0, 2 * w_a)

    xpa_ref[pa:pa + tc, :] = proj_ref[:, w_a:2 * w_a]
    xa = cab_ref[...] + caw_ref[0:1, :] * xpa_ref[pa - 3:pa - 3 + tc, :]
    for j in range(1, CONV_A):
        xa = xa + caw_ref[j:j + 1, :] * xpa_ref[pa - 3 + j:pa - 3 + j + tc, :]
    a, bt = _lru_gates(xa, wg_ref[...], bg_ref[...], lam_ref[...])
    a_ref[...] = a
    b_ref[...] = bt
    row8 = lax.broadcasted_iota(jnp.int32, (SUBLANES, w_a), 0)

    def scan_block(i, hprev):
        r0 = pl.multiple_of(i * SUBLANES, SUBLANES)
        ab = a_ref[pl.ds(r0, SUBLANES), :]
        bb = b_ref[pl.ds(r0, SUBLANES), :]
        for k in (1, 2, 4):
            a_sh = jnp.where(row8 >= k, pltpu.roll(ab, k, 0), 1.0)
            b_sh = jnp.where(row8 >= k, pltpu.roll(bb, k, 0), 0.0)
            bb = ab * b_sh + bb
            ab = ab * a_sh
        h = ab * hprev + bb
        hseq_ref[pl.ds(r0, SUBLANES), :] = h
        return jnp.broadcast_to(h[SUBLANES - 1:SUBLANES, :], (SUBLANES, w_a))

    hcar_ref[...] = lax.fori_loop(0, tc // SUBLANES, scan_block, hcar_ref[...])
    in_proj(2 * w_a, proj_ref.shape[1])
    mix_ref[:, 0:w_a] = (jax.nn.gelu(proj_ref[:, 0:w_a]) * hseq_ref[...]).astype(BF16)
    xpa_ref[0:pa, :] = xpa_ref[tc:tc + pa, :]

    b0 = 2 * w_a
    xpb_ref[pb:pb + tc, :] = proj_ref[:, b0:b0 + w_b] * _sigmoid(proj_ref[:, b0 + w_b:b0 + 2 * w_b])
    first = pb - (CONV_B - 1)
    for r in range(SUBLANES):
        span = tc + (CONV_B - 1 - r) // SUBLANES * SUBLANES
        sh_ref[r, 0:span, :] = xpb_ref[first + r:first + r + span, :]
    for blk in range(tc // CONV_B_ROWS):
        r0 = blk * CONV_B_ROWS
        acc = jnp.broadcast_to(cbb_ref[...], (CONV_B_ROWS, w_b))
        for j in range(CONV_B):
            r = j % SUBLANES
            acc = acc + cbw_ref[j:j + 1, :] * sh_ref[r, r0 + j - r:r0 + j - r + CONV_B_ROWS, :]
        out_b = _silu(_layernorm(acc, lng_ref[...], lnb_ref[...]))
        mix_ref[r0:r0 + CONV_B_ROWS, w_a:w_a + w_b] = out_b.astype(BF16)
    xpb_ref[0:pb, :] = xpb_ref[tc:tc + pb, :]

    c0 = 2 * w_a + 2 * w_b
    n_pair = w_c // LANES
    lane = lax.broadcasted_iota(jnp.int32, (chunk, LANES), 1)
    head0 = lane < (LANES // 2)
    first_half = (lane % (LANES // 2)) < (LANES // 4)
    rl = lax.broadcasted_iota(jnp.int32, (LANES, LANES), 0) < (LANES // 2)
    cl = lax.broadcasted_iota(jnp.int32, (LANES, LANES), 1) < (LANES // 2)
    same_head = rl == cl
    inv_d = 2.0 / LANES
    for c in range(tc // chunk):
        rows = slice(c * chunk, (c + 1) * chunk)
        cos = cos_ref[rows, :]
        sin = sin_ref[rows, :]
        for p in range(n_pair):
            cq = c0 + p * LANES
            q = _rope_pair(proj_ref[rows, cq:cq + LANES], cos, sin, first_half)
            k = _rope_pair(proj_ref[rows, cq + w_c:cq + w_c + LANES], cos, sin, first_half) * (2.0 / LANES) ** 0.5
            v = proj_ref[rows, cq + 2 * w_c:cq + 2 * w_c + LANES]
            g = proj_ref[rows, cq + 3 * w_c:cq + 3 * w_c + LANES]
            qb, kb, vb = q.astype(BF16), k.astype(BF16), v.astype(BF16)
            o = _retention_intra_pair(qb, kb, vb, dmat_ref[2 * p], dmat_ref[2 * p + 1], head0)
            s_old = s_ref[p]
            o = o + jnp.dot(qb, s_old.astype(BF16), preferred_element_type=F32) * xi_ref[p]
            kz = (k * zeta_ref[p]).astype(BF16)
            upd = lax.dot_general(kz, vb, (((0,), (0,)), ((), ())), preferred_element_type=F32)
            s_ref[p] = gmat_ref[p] * s_old + jnp.where(same_head, upd, 0.0)
            on = _head_norm_pair(o, head0, inv_d) * rng_ref[:, p * LANES:(p + 1) * LANES]
            mix_ref[rows, w_a + w_b + p * LANES:w_a + w_b + (p + 1) * LANES] = (_silu(g) * on).astype(BF16)

    xm = x_ref[...] + jnp.dot(mix_ref[...], woutb_ref[...], preferred_element_type=F32)
    xmid_ref[...] = xm
    _ffn_norm_route(xm, gffn_ref[...], wrh_ref[...], wrl_ref[...], br_ref[...], carry_ref,
                    xn_ref, rt_ref, rg_ref, counts_ref)

    @pl.when(t == n_t - 1)
    def _():
        hout_ref[0] = hcar_ref[...]
        bufa_out_ref[0] = xpa_ref[0:pa, :]
        bufb_out_ref[0] = xpb_ref[0:pb, :]
        sout_ref[0] = s_ref[...]


def _retention_tables(chunk):
    log_g = jnp.log1p(-jnp.exp2(-5.0 - jnp.arange(NH_C, dtype=F32)))
    idx = jnp.arange(chunk, dtype=F32)
    diff = idx[:, None] - idx[None, :]
    causal = diff >= 0
    dmat = jnp.where(causal[None], jnp.exp(jnp.where(causal, diff, 0.0)[None] * log_g[:, None, None]), 0.0)
    xi = jnp.exp((idx + 1.0)[None, :] * log_g[:, None])
    zeta = jnp.exp((chunk - 1.0 - idx)[None, :] * log_g[:, None])
    g_chunk = jnp.exp(chunk * log_g)
    return dmat, xi, zeta, g_chunk


def _pair_lanes(per_head):
    h, rows = per_head.shape
    half = LANES // 2
    x = jnp.broadcast_to(per_head.reshape(h // 2, 2, rows, 1), (h // 2, 2, rows, half))
    return x.transpose(0, 2, 1, 3).reshape(h // 2, rows, LANES)


def _rope_tables(pos):
    half = LANES // 4
    inv = ROPE_BASE ** (-jnp.arange(half, dtype=F32) / half)
    ang = pos.astype(F32)[:, None] * inv[None, :]
    cos, sin = jnp.cos(ang), jnp.sin(ang)
    return jnp.concatenate([cos] * 4, axis=1), jnp.concatenate([-sin, sin, -sin, sin], axis=1)


def _blockdiag_gate_weights(wa, wx):
    nh, hd, _ = wa.shape
    eye = jnp.eye(nh, dtype=wa.dtype)
    da = (eye[:, None, :, None] * wa[:, :, None, :]).reshape(nh * hd, nh * hd)
    dx = (eye[:, None, :, None] * wx[:, :, None, :]).reshape(nh * hd, nh * hd)
    return jnp.concatenate([da, dx], axis=1)


def _prompt_layer(x, n_seq, seq_len, layer, w_in, w_out, lw):
    w_a, w_b, w_c = lw["w_a"], lw["w_b"], lw["w_c"]
    n, d = x.shape
    d_in = w_in.shape[2]
    d_mix = w_a + w_b + w_c
    chunk = math.gcd(seq_len, RET_CHUNK)
    tc = ROUTE_TILE
    assert seq_len % tc == 0 and tc % chunk == 0 and tc % CONV_B_ROWS == 0
    n_t = seq_len // tc
    n_pair = w_c // LANES
    dmat, xi, zeta, g_chunk = _retention_tables(chunk)
    half = LANES // 2
    gm = jnp.zeros((n_pair, 2, half, 2, half), F32)
    gm = gm.at[:, 0, :, 0, :].set(g_chunk[0::2, None, None]).at[:, 1, :, 1, :].set(g_chunk[1::2, None, None])
    cos, sin = _rope_tables(jnp.arange(seq_len, dtype=jnp.int32))
    zeros = lambda *s: jnp.zeros(s, F32)
    full = lambda shape: _resident(shape, (0,) * len(shape))
    per_seq = lambda shape: pl.BlockSpec((1,) + shape, lambda b, t: (b,) + (0,) * len(shape))
    tile = lambda width: pl.BlockSpec((tc, width), lambda b, t: (b * n_t + t, 0))
    kern = functools.partial(_prompt_layer_kernel, tc=tc, w_a=w_a, w_b=w_b, w_c=w_c, chunk=chunk)
    return pl.pallas_call(
        kern,
        grid=(n_seq, n_t),
        in_specs=[
            tile(d), full((1, d)), _resident((1, d, d_in), (layer, 0, 0)),
            pl.BlockSpec((tc, LANES), lambda b, t: (t, 0)),
            pl.BlockSpec((tc, LANES), lambda b, t: (t, 0)),
            per_seq((SUBLANES, w_a)), per_seq((SUBLANES, w_a)), per_seq((4 * SUBLANES, w_b)),
            per_seq((n_pair, LANES, LANES)),
            full((CONV_A, w_a)), full((1, w_a)), full((w_a, 2 * w_a)), full((1, 2 * w_a)), full((1, w_a)),
            full((CONV_B, w_b)), full((1, w_b)), full((1, w_b)), full((1, w_b)), full((1, w_c)),
            full((NH_C, chunk, chunk)), full((n_pair, chunk, LANES)), full((n_pair, chunk, LANES)),
            full((n_pair, LANES, LANES)),
            _resident((1, d_mix, d), (layer, 0, 0)), full((1, d)),
            full((LANES, d)), full((LANES, d)), full((LANES, LANES)),
        ],
        out_specs=[
            tile(d), tile(d),
            pl.BlockSpec((SUBLANES, tc), lambda b, t: (0, b * n_t + t)),
            tile(LANES),
            pl.BlockSpec((LANES, LANES), lambda b, t: (0, 0)),
            per_seq((SUBLANES, w_a)), per_seq((SUBLANES, w_a)), per_seq((4 * SUBLANES, w_b)),
            per_seq((n_pair, LANES, LANES)),
        ],
        out_shape=[
            jax.ShapeDtypeStruct((n, d), F32),
            jax.ShapeDtypeStruct((n, d), F32),
            jax.ShapeDtypeStruct((SUBLANES, n), jnp.int32),
            jax.ShapeDtypeStruct((n, LANES), F32),
            jax.ShapeDtypeStruct((LANES, LANES), F32),
            jax.ShapeDtypeStruct((n_seq, SUBLANES, w_a), F32),
            jax.ShapeDtypeStruct((n_seq, SUBLANES, w_a), F32),
            jax.ShapeDtypeStruct((n_seq, 4 * SUBLANES, w_b), F32),
            jax.ShapeDtypeStruct((n_seq, n_pair, LANES, LANES), F32),
        ],
        scratch_shapes=[
            pltpu.VMEM((d, d_in), BF16),
            pltpu.VMEM((d_mix, d), BF16),
            pltpu.VMEM((tc, d), BF16),
            pltpu.VMEM((tc, d_in), F32),
            pltpu.VMEM((tc, d_mix), BF16),
            pltpu.VMEM((SUBLANES + tc, w_a), F32),
            pltpu.VMEM((4 * SUBLANES + tc, w_b), F32),
            pltpu.VMEM((SUBLANES, 3 * SUBLANES + tc, w_b), F32),
            pltpu.VMEM((tc, w_a), F32),
            pltpu.VMEM((tc, w_a), F32),
            pltpu.VMEM((tc, w_a), F32),
            pltpu.VMEM((SUBLANES, w_a), F32),
            pltpu.VMEM((n_pair, LANES, LANES), F32),
            pltpu.VMEM((LANES, LANES), F32),
        ],
        compiler_params=_cparams("arbitrary", "arbitrary"),
        name="prompt_layer",
    )(x, lw["g_mix"], w_in, cos, sin,
      zeros(n_seq, SUBLANES, w_a), zeros(n_seq, SUBLANES, w_a), zeros(n_seq, 4 * SUBLANES, w_b),
      zeros(n_seq, n_pair, LANES, LANES),
      lw["conv_a_w"], lw["conv_a_b"], lw["wg"], lw["bg"], lw["lam"],
      lw["conv_b_w"], lw["conv_b_b"], lw["ln_g"], lw["ln_b"], lw["ret_g"],
      dmat, _pair_lanes(xi), _pair_lanes(zeta), gm.reshape(n_pair, LANES, LANES),
      w_out, lw["g_ffn"], lw["wr_hi"], lw["wr_lo"], lw["b_route"])


def _unpair_state(s_pairs):
    b, n_pair = s_pairs.shape[:2]
    half = LANES // 2
    s = s_pairs.reshape(b, n_pair, 2, half, 2, half)
    return jnp.stack([s[:, :, 0, :, 0, :], s[:, :, 1, :, 1, :]], axis=2).reshape(b, 2 * n_pair, half, half)


def _in_proj_kernel(x_ref, g_ref, w_ref, o_ref):
    u = _rms(x_ref[...], g_ref[...]).astype(BF16)
    o_ref[...] = jnp.dot(u, w_ref[0].astype(BF16), preferred_element_type=F32)


def _in_proj(x, g, w_in, layer):
    n, d = x.shape
    d_in = w_in.shape[2]
    tm = _row_tile(n)
    return pl.pallas_call(
        _in_proj_kernel,
        grid=(n // tm,),
        in_specs=[
            pl.BlockSpec((tm, d), lambda i: (i, 0)),
            _resident((1, d), (0, 0)),
            _resident((1, d, d_in), (layer, 0, 0)),
        ],
        out_specs=pl.BlockSpec((tm, d_in), lambda i: (i, 0)),
        out_shape=jax.ShapeDtypeStruct((n, d_in), F32),
        compiler_params=_cparams("arbitrary"),
        name="in_proj_sample",
    )(x, g, w_in)


def _mix_sample_ab_kernel(p_ref, h0_ref, bufa_ref, bufb_ref,
                          caw_ref, cab_ref, wg_ref, bg_ref, lam_ref,
                          cbw_ref, cbb_ref, lng_ref, lnb_ref,
                          o_ref, h_ref, bufa_out_ref, bufb_out_ref, *, ts, w_a, w_b):
    xpa = [bufa_ref[j] for j in range(CONV_A - 1)] + [p_ref[t, :, w_a:2 * w_a] for t in range(ts)]
    xa = []
    for t in range(ts):
        acc = cab_ref[...] + caw_ref[0:1, :] * xpa[t]
        for j in range(1, CONV_A):
            acc = acc + caw_ref[j:j + 1, :] * xpa[t + j]
        xa.append(acc)
    nb = xa[0].shape[0]
    a, bt = _lru_gates(jnp.concatenate(xa, axis=0), wg_ref[...], bg_ref[...], lam_ref[...])
    h = h0_ref[...]
    for t in range(ts):
        h = a[t * nb:(t + 1) * nb, :] * h + bt[t * nb:(t + 1) * nb, :]
        o_ref[t, :, 0:w_a] = (jax.nn.gelu(p_ref[t, :, 0:w_a]) * h).astype(BF16)
    h_ref[...] = h
    for j in range(CONV_A - 1):
        bufa_out_ref[j] = xpa[ts + j]
    b0 = 2 * w_a
    hist = CONV_B - 1
    glu = [p_ref[t, :, b0:b0 + w_b] * _sigmoid(p_ref[t, :, b0 + w_b:b0 + 2 * w_b]) for t in range(ts)]
    xpb = lambda i: bufb_ref[i] if i < hist else glu[i - hist]
    for t in range(ts):
        acc = cbb_ref[...] + cbw_ref[0:1, :] * xpb(t)
        for j in range(1, CONV_B):
            acc = acc + cbw_ref[j:j + 1, :] * xpb(t + j)
        o_ref[t, :, w_a:w_a + w_b] = _silu(_layernorm(acc, lng_ref[...], lnb_ref[...])).astype(BF16)
    for i in range(hist):
        bufb_out_ref[i] = xpb(ts + i)


def _mix_sample_ab(p_t, h0, bufa_t, bufb_t, layer, lw):
    ts, n_seq, width = p_t.shape
    w_a, w_b = lw["w_a"], lw["w_b"]
    nb = 32 if n_seq % 32 == 0 else n_seq
    full = lambda shape: _resident(shape, (0,) * len(shape))
    kern = functools.partial(_mix_sample_ab_kernel, ts=ts, w_a=w_a, w_b=w_b)
    return pl.pallas_call(
        kern,
        grid=(n_seq // nb,),
        in_specs=[
            pl.BlockSpec((ts, nb, width), lambda i: (0, i, 0)),
            pl.BlockSpec((None, nb, w_a), lambda i: (layer, i, 0)),
            pl.BlockSpec((CONV_A - 1, nb, w_a), lambda i: (0, i, 0)),
            pl.BlockSpec((CONV_B - 1, nb, w_b), lambda i: (0, i, 0)),
            full((CONV_A, w_a)), full((1, w_a)), full((w_a, 2 * w_a)), full((1, 2 * w_a)), full((1, w_a)),
            full((CONV_B, w_b)), full((1, w_b)), full((1, w_b)), full((1, w_b)),
        ],
        out_specs=[
            pl.BlockSpec((ts, nb, w_a + w_b), lambda i: (0, i, 0)),
            pl.BlockSpec((nb, w_a), lambda i: (i, 0)),
            pl.BlockSpec((CONV_A - 1, nb, w_a), lambda i: (0, i, 0)),
            pl.BlockSpec((CONV_B - 1, nb, w_b), lambda i: (0, i, 0)),
        ],
        out_shape=[
            jax.ShapeDtypeStruct((ts, n_seq, w_a + w_b), BF16),
            jax.ShapeDtypeStruct((n_seq, w_a), F32),
            jax.ShapeDtypeStruct((CONV_A - 1, n_seq, w_a), F32),
            jax.ShapeDtypeStruct((CONV_B - 1, n_seq, w_b), F32),
        ],
        compiler_params=_cparams("arbitrary"),
        name="mix_sample_ab",
    )(p_t, h0, bufa_t, bufb_t,
      lw["conv_a_w"], lw["conv_a_b"], lw["wg"], lw["bg"], lw["lam"],
      lw["conv_b_w"], lw["conv_b_b"], lw["ln_g"], lw["ln_b"])


def _mix_sample_c_kernel(q_ref, k_ref, v_ref, g_ref, oab_ref, s_ref, cos_ref, sin_ref,
                         dmat_ref, xi_ref, zeta_ref, gcol_ref, rng_ref, tile_ref, fold_ref, *rest,
                         rows, ts, w_c, w_ab, layer):
    if layer:
        prev_ref, mix_ref, sout_ref, qs_ref, kzs_ref, vs_ref, oi_ref = rest
        for l in range(layer):
            sout_ref[l] = prev_ref[l]
    else:
        mix_ref, sout_ref, qs_ref, kzs_ref, vs_ref, oi_ref = rest
    n_pair = w_c // LANES
    hd = LANES // 2
    lane = lax.broadcasted_iota(jnp.int32, (rows, LANES), 1)
    head0 = lane < hd
    first_half = (lane % hd) < (hd // 2)
    cos = cos_ref[...]
    sin = sin_ref[...]
    for p in range(n_pair):
        cs = slice(p * LANES, (p + 1) * LANES)
        q = _rope_pair(q_ref[:, cs], cos, sin, first_half)
        k = _rope_pair(k_ref[:, cs], cos, sin, first_half) * (1.0 / hd) ** 0.5
        v = v_ref[:, cs]
        qb, kb, vb = q.astype(BF16), k.astype(BF16), v.astype(BF16)
        oi_ref[:, cs] = _retention_intra_pair(qb, kb, vb, dmat_ref[2 * p], dmat_ref[2 * p + 1], head0)
        qs_ref[:, cs] = q
        kzs_ref[:, cs] = k * zeta_ref[:, cs]
        vs_ref[:, cs] = v
    grp = SUBLANES
    per_grp = grp // ts
    stack = NH_C * grp
    srow = lax.broadcasted_iota(jnp.int32, (stack, w_c), 0)
    slane = lax.broadcasted_iota(jnp.int32, (stack, w_c), 1)
    own_head = (srow // grp) == (slane // hd)
    tile_m = tile_ref[...]
    fold_m = fold_ref[...]
    for gi in range(rows // grp):
        rs = slice(gi * grp, (gi + 1) * grp)
        q8 = jnp.concatenate([qs_ref[rs, :]] * NH_C, axis=0)
        kz8 = jnp.concatenate([kzs_ref[rs, :]] * NH_C, axis=0)
        v8 = jnp.concatenate([vs_ref[rs, :]] * NH_C, axis=0)
        vr = jnp.dot(jnp.where(own_head, v8, 0.0).astype(BF16), fold_m, preferred_element_type=F32).astype(BF16)
        o_stack = jnp.zeros((stack, hd), F32)
        for s2 in range(per_grp):
            sq = gi * per_grp + s2
            own_seq = ((srow % grp) // ts) == s2
            sel = jnp.logical_and(own_head, own_seq)
            s_old = s_ref[0, sq].reshape(w_c, hd)
            o_stack = o_stack + jnp.dot(jnp.where(sel, q8, 0.0).astype(BF16), s_old.astype(BF16),
                                        preferred_element_type=F32)
            upd = lax.dot_general(jnp.where(sel, kz8, 0.0).astype(BF16), vr, (((0,), (0,)), ((), ())),
                                  preferred_element_type=F32)
            sout_ref[layer, sq] = (gcol_ref[...] * s_old + upd).reshape(NH_C, hd, hd)
        p1 = o_stack.astype(BF16)
        r1 = o_stack - p1.astype(F32)
        p2 = r1.astype(BF16)
        p3 = (r1 - p2.astype(F32)).astype(BF16)
        spread = (jnp.dot(p1, tile_m, preferred_element_type=F32)
                  + jnp.dot(p2, tile_m, preferred_element_type=F32)
                  + jnp.dot(p3, tile_m, preferred_element_type=F32))
        spread = jnp.where(own_head, spread, 0.0).reshape(NH_C, grp, w_c)
        oi_ref[rs, :] = oi_ref[rs, :] + jnp.sum(spread, axis=0) * xi_ref[rs, :]
    mix_ref[:, 0:w_ab] = oab_ref[...]
    for p in range(n_pair):
        cs = slice(p * LANES, (p + 1) * LANES)
        on = _head_norm_pair(oi_ref[:, cs], head0, 1.0 / hd) * rng_ref[:, cs]
        mix_ref[:, w_ab + p * LANES:w_ab + (p + 1) * LANES] = (_silu(g_ref[:, cs]) * on).astype(BF16)


def _mix_sample_c(proj, n_seq, ts, oab, state, layer, prev_states, lw):
    w_a, w_b, w_c = lw["w_a"], lw["w_b"], lw["w_c"]
    hd = LANES // 2
    n_rows = n_seq * ts
    rows = 64 if n_rows % 64 == 0 else n_rows
    assert SUBLANES % ts == 0 and rows % SUBLANES == 0
    seq_per = rows // ts
    c0 = 2 * w_a + 2 * w_b
    assert c0 % w_c == 0
    cb = c0 // w_c
    log_g = jnp.log1p(-jnp.exp2(-5.0 - jnp.arange(NH_C, dtype=F32)))
    r = jnp.arange(rows)
    step = (r % ts).astype(F32)
    diff = step[:, None] - step[None, :]
    live = jnp.logical_and((r[:, None] // ts) == (r[None, :] // ts), diff >= 0)
    dmat = jnp.where(live[None], jnp.exp(jnp.where(live, diff, 0.0)[None] * log_g[:, None, None]), 0.0)
    xi = jnp.repeat(jnp.exp((step + 1.0)[None, :] * log_g[:, None]).T, hd, axis=1)
    zeta = jnp.repeat(jnp.exp((ts - 1.0 - step)[None, :] * log_g[:, None]).T, hd, axis=1)
    gcol = jnp.broadcast_to(jnp.repeat(jnp.exp(ts * log_g), hd)[:, None], (w_c, hd))
    pos = PAST_LEN + (r % ts).astype(jnp.int32)
    cos, sin = _rope_tables(pos)
    tile_m = jnp.tile(jnp.eye(hd, dtype=BF16), (1, NH_C))
    fold_m = jnp.tile(jnp.eye(hd, dtype=BF16), (NH_C, 1))
    full = lambda shape: _resident(shape, (0,) * len(shape))
    col = lambda j: pl.BlockSpec((rows, w_c), lambda i: (i, cb + j))
    states = lambda n_layers, first: pl.BlockSpec((n_layers, seq_per, NH_C, hd, hd), lambda i: (first, i, 0, 0, 0))
    kern = functools.partial(_mix_sample_c_kernel, rows=rows, ts=ts, w_c=w_c, w_ab=w_a + w_b, layer=layer)
    in_specs = [
        col(0), col(1), col(2), col(3),
        pl.BlockSpec((rows, w_a + w_b), lambda i: (i, 0)),
        states(1, layer),
        full((rows, LANES)), full((rows, LANES)),
        full((NH_C, rows, rows)), full((rows, w_c)), full((rows, w_c)), full((w_c, hd)), full((1, w_c)),
        full((hd, w_c)), full((w_c, hd)),
    ]
    args = [proj, proj, proj, proj, oab, state, cos, sin, dmat, xi, zeta, gcol, lw["ret_g"], tile_m, fold_m]
    if layer:
        in_specs.append(states(layer, 0))
        args.append(prev_states)
    return pl.pallas_call(
        kern,
        grid=(n_rows // rows,),
        in_specs=in_specs,
        out_specs=[
            pl.BlockSpec((rows, w_a + w_b + w_c), lambda i: (i, 0)),
            states(layer + 1, 0),
        ],
        out_shape=[
            jax.ShapeDtypeStruct((n_rows, w_a + w_b + w_c), BF16),
            jax.ShapeDtypeStruct((layer + 1,) + state.shape[1:], F32),
        ],
        scratch_shapes=[pltpu.VMEM((rows, w_c), F32)] * 4,
        compiler_params=_cparams("arbitrary"),
        name="mix_sample_c",
    )(*args)


def _out_proj_kernel(x_ref, mix_ref, w_ref, g_ref, wrh_ref, wrl_ref, br_ref, seen_ref,
                     xmid_ref, xn_ref, rt_ref, rg_ref, counts_ref, carry_ref):
    @pl.when(pl.program_id(0) == 0)
    def _():
        carry_ref[...] = seen_ref[...]

    xm = x_ref[...] + jnp.dot(mix_ref[...], w_ref[0].astype(BF16), preferred_element_type=F32)
    xmid_ref[...] = xm
    _ffn_norm_route(xm, g_ref[...], wrh_ref[...], wrl_ref[...], br_ref[...], carry_ref,
                    xn_ref, rt_ref, rg_ref, counts_ref)


def _out_proj(x, mix, w_out, layer, seen, lw):
    n, d = x.shape
    d_mix = mix.shape[1]
    tm = ROUTE_TILE
    assert n % tm == 0
    full = lambda shape: _resident(shape, (0,) * len(shape))
    rows = lambda width: pl.BlockSpec((tm, width), lambda i: (i, 0))
    return pl.pallas_call(
        _out_proj_kernel,
        grid=(n // tm,),
        in_specs=[rows(d), rows(d_mix), _resident((1, d_mix, d), (layer, 0, 0)),
                  full((1, d)), full((LANES, d)), full((LANES, d)), full((LANES, LANES)), full((LANES, LANES))],
        out_specs=[rows(d), rows(d), pl.BlockSpec((SUBLANES, tm), lambda i: (0, i)), rows(LANES),
                   pl.BlockSpec((LANES, LANES), lambda i: (0, 0))],
        out_shape=[
            jax.ShapeDtypeStruct((n, d), F32),
            jax.ShapeDtypeStruct((n, d), F32),
            jax.ShapeDtypeStruct((SUBLANES, n), jnp.int32),
            jax.ShapeDtypeStruct((n, LANES), F32),
            jax.ShapeDtypeStruct((LANES, LANES), F32),
        ],
        scratch_shapes=[pltpu.VMEM((LANES, LANES), F32)],
        compiler_params=_cparams("arbitrary"),
        name="out_proj_sample",
    )(x, mix, w_out, lw["g_ffn"], lw["wr_hi"], lw["wr_lo"], lw["b_route"], seen)


SLOT_TILE = 256


def _slot_plan(counts, n_tokens):
    n_slots = 2 * n_tokens + N_EXPERTS * SLOT_TILE
    n_slots = -(-n_slots // SLOT_TILE) * SLOT_TILE
    n_tiles = n_slots // SLOT_TILE
    padded = (counts + (SLOT_TILE - 1)) // SLOT_TILE * SLOT_TILE
    ends = jnp.cumsum(padded)
    offs = ends - padded
    total = ends[-1]
    tile_start = jnp.arange(n_tiles, dtype=jnp.int32) * SLOT_TILE
    probe = jnp.minimum(tile_start, total - 1)
    tile_expert = jnp.sum((ends[None, :] <= probe[:, None]).astype(jnp.int32), axis=1)
    tile_expert = jnp.minimum(tile_expert, N_EXPERTS - 1)
    n_valid = (total // SLOT_TILE).reshape(1)
    last_tile = jnp.where(padded > 0, ends - SLOT_TILE, 0)
    return n_slots, offs, tile_expert, n_valid, last_tile, (padded > 0).astype(jnp.int32)


def _row_copy(src_ref, src_row, dst_ref, dst_row, sem):
    return pltpu.make_async_copy(src_ref.at[pl.ds(src_row, 1), :], dst_ref.at[pl.ds(dst_row, 1), :], sem)


def _dispatch_kernel(last_ref, used_ref, nv_ref, pos1_ref, pos2_ref, xnp_ref, xns_ref, xs_ref,
                     src_ref, zero_ref, zsem, sem, *, n_prompt_tiles):
    tm = xnp_ref.shape[0]
    n_tiles = xs_ref.shape[0] // SLOT_TILE
    i = pl.program_id(0)

    @pl.when(i == 0)
    def _():
        zero_ref[...] = jnp.zeros_like(zero_ref)
        tile = lambda row: pltpu.make_async_copy(
            zero_ref, xs_ref.at[pl.ds(pl.multiple_of(row, SLOT_TILE), SLOT_TILE), :], zsem)
        for e in range(N_EXPERTS):
            @pl.when(used_ref[e] != 0)
            def _():
                tile(last_ref[e]).start()
        lax.fori_loop(nv_ref[0], n_tiles, lambda j, c: (tile(j * SLOT_TILE).start(), c)[1], 0)
        for e in range(N_EXPERTS):
            @pl.when(used_ref[e] != 0)
            def _():
                tile(0).wait()
        lax.fori_loop(nv_ref[0], n_tiles, lambda j, c: (tile(0).wait(), c)[1], 0)

    src_ref[...] = jnp.where(i < n_prompt_tiles, xnp_ref[...], xns_ref[...])
    for t in range(tm):
        _row_copy(src_ref, t, xs_ref, pos1_ref[t], sem).start(priority=0)
        _row_copy(src_ref, t, xs_ref, pos2_ref[t], sem).start(priority=1)
    for _ in range(2 * tm):
        _row_copy(src_ref, 0, xs_ref, 0, sem).wait()


def _dispatch(xn_p, xn_s, pos1, pos2, n_slots, last_tile, used, n_valid):
    n_p, d = xn_p.shape
    n_s = xn_s.shape[0]
    tm = ROUTE_TILE
    tp = n_p // tm
    ts_last = n_s // tm - 1
    smem_rows = lambda: pl.BlockSpec((tm,), lambda i, *_: (i,), memory_space=pltpu.SMEM)
    grid_spec = pltpu.PrefetchScalarGridSpec(
        num_scalar_prefetch=3,
        grid=((n_p + n_s) // tm,),
        in_specs=[smem_rows(), smem_rows(),
                  pl.BlockSpec((tm, d), lambda i, *_: (jnp.minimum(i, tp - 1), 0)),
                  pl.BlockSpec((tm, d), lambda i, *_: (jnp.clip(i - tp, 0, ts_last), 0))],
        out_specs=pl.BlockSpec(memory_space=pl.ANY),
        scratch_shapes=[pltpu.VMEM((tm, d), F32), pltpu.VMEM((SLOT_TILE, d), F32),
                        pltpu.SemaphoreType.DMA, pltpu.SemaphoreType.DMA],
    )
    return pl.pallas_call(
        functools.partial(_dispatch_kernel, n_prompt_tiles=tp),
        grid_spec=grid_spec,
        out_shape=jax.ShapeDtypeStruct((n_slots, d), F32),
        compiler_params=_cparams("arbitrary"),
        name="moe_dispatch",
    )(last_tile, used, n_valid, pos1, pos2, xn_p, xn_s)


def _experts_kernel(te_ref, nv_ref, xs_ref, w1_ref, w3_ref, w2_ref, ys_ref, w1b_ref, w3b_ref, w2b_ref):
    j = pl.program_id(0)
    fresh = jnp.logical_or(j == 0, te_ref[j] != te_ref[jnp.maximum(j - 1, 0)])

    @pl.when(fresh)
    def _():
        w1b_ref[...] = w1_ref[0, 0].astype(BF16)
        w3b_ref[...] = w3_ref[0, 0].astype(BF16)
        w2b_ref[...] = w2_ref[0, 0].astype(BF16)

    @pl.when(j < nv_ref[0])
    def _():
        x = xs_ref[...].astype(BF16)
        h1 = jnp.dot(x, w1b_ref[...], preferred_element_type=F32)
        h3 = jnp.dot(x, w3b_ref[...], preferred_element_type=F32)
        h = (_silu(h1) * h3).astype(BF16)
        ys_ref[...] = jnp.dot(h, w2b_ref[...], preferred_element_type=F32)

    @pl.when(j >= nv_ref[0])
    def _():
        ys_ref[...] = jnp.zeros_like(ys_ref)


def _experts(xs, tile_expert, n_valid, w1, w3, w2, layer):
    n_slots, d = xs.shape
    d_e = w1.shape[3]
    tile = lambda j, te, nv: (jnp.minimum(j, nv[0] - 1), 0)
    expert = lambda j, te, nv: (layer, te[j], 0, 0)
    grid_spec = pltpu.PrefetchScalarGridSpec(
        num_scalar_prefetch=2,
        grid=(n_slots // SLOT_TILE,),
        in_specs=[
            pl.BlockSpec((SLOT_TILE, d), tile),
            pl.BlockSpec((1, 1, d, d_e), expert),
            pl.BlockSpec((1, 1, d, d_e), expert),
            pl.BlockSpec((1, 1, d_e, d), expert),
        ],
        out_specs=pl.BlockSpec((SLOT_TILE, d), lambda j, te, nv: (j, 0)),
        scratch_shapes=[pltpu.VMEM((d, d_e), BF16), pltpu.VMEM((d, d_e), BF16), pltpu.VMEM((d_e, d), BF16)],
    )
    return pl.pallas_call(
        _experts_kernel,
        grid_spec=grid_spec,
        out_shape=jax.ShapeDtypeStruct((n_slots, d), F32),
        compiler_params=_cparams("arbitrary"),
        name="moe_experts",
    )(tile_expert, n_valid, xs, w1, w3, w2)


def _combine_kernel(pos1_ref, pos2_ref, xmid_ref, rg_ref, gfin_ref, ys_ref, o_ref, y1_ref, y2_ref, sem,
                    *, final_norm):
    tm = xmid_ref.shape[0]
    for t in range(tm):
        _row_copy(ys_ref, pos1_ref[t], y1_ref, t, sem).start(priority=0)
        _row_copy(ys_ref, pos2_ref[t], y2_ref, t, sem).start(priority=1)
    for _ in range(2 * tm):
        _row_copy(ys_ref, 0, y1_ref, 0, sem).wait()
    rg = rg_ref[...]
    out = xmid_ref[...] + (rg[:, 0:1] * y1_ref[...] + rg[:, 1:2] * y2_ref[...])
    o_ref[...] = _rms(out, gfin_ref[...]) if final_norm else out


def _combine(xmid, rg, pos1, pos2, row0, ys, g_final, final_norm):
    n, d = xmid.shape
    tm = ROUTE_TILE
    t0 = row0 // tm
    smem_rows = lambda: pl.BlockSpec((tm,), lambda i: (t0 + i,), memory_space=pltpu.SMEM)
    kern = functools.partial(_combine_kernel, final_norm=final_norm)
    return pl.pallas_call(
        kern,
        grid=(n // tm,),
        in_specs=[smem_rows(), smem_rows(),
                  pl.BlockSpec((tm, d), lambda i: (i, 0)),
                  pl.BlockSpec((tm, LANES), lambda i: (i, 0)),
                  _resident((1, d), (0, 0)),
                  pl.BlockSpec(memory_space=pl.ANY)],
        out_specs=pl.BlockSpec((tm, d), lambda i: (i, 0)),
        out_shape=jax.ShapeDtypeStruct((n, d), F32),
        scratch_shapes=[pltpu.VMEM((tm, d), F32), pltpu.VMEM((tm, d), F32), pltpu.SemaphoreType.DMA],
        compiler_params=_cparams("arbitrary"),
        name="moe_combine",
    )(pos1, pos2, xmid, rg, g_final, ys)


def _moe_routed(prompt, sample, counts, w1, w3, w2, layer, g_final, final_norm):
    xmid_p, xn_p, rt_p, rg_p = prompt
    xmid_s, xn_s, rt_s, rg_s = sample
    n_p = xmid_p.shape[0]
    n = n_p + xmid_s.shape[0]
    n_slots, offs, tile_expert, n_valid, last_tile, used = _slot_plan(
        counts[:N_EXPERTS, 0].astype(jnp.int32), n)
    rt = jnp.concatenate([rt_p, rt_s], axis=1)
    experts = jnp.arange(N_EXPERTS, dtype=jnp.int32)[:, None]
    base = lambda e: jnp.sum(jnp.where(experts == e[None, :], offs[:, None], 0), axis=0)
    pos1 = base(rt[0]) + rt[2]
    pos2 = base(rt[1]) + rt[3]
    xs = _dispatch(xn_p, xn_s, pos1, pos2, n_slots, last_tile, used, n_valid)
    ys = _experts(xs, tile_expert, n_valid, w1, w3, w2, layer)
    y_p = _combine(xmid_p, rg_p, pos1, pos2, 0, ys, g_final, final_norm)
    y_s = _combine(xmid_s, rg_s, pos1, pos2, n_p, ys, g_final, final_norm)
    return y_p, y_s


def kernel(x_prompt, x_sample, state_lru_h, state_lru_conv, state_conv, state_ret, norm_mix_g, norm_ffn_g, final_norm_g, w_in, lru_conv_w, lru_conv_b, lru_wa, lru_ba, lru_wx, lru_bx, lru_lambda, cm_dw_w, cm_dw_b, cm_ln_g, cm_ln_b, ret_norm_g, w_out, moe_w_group, moe_b_group, moe_w_expert, moe_b_expert, moe_w1, moe_w3, moe_w2):
    bp, tp, d = x_prompt.shape
    bs, ts, _ = x_sample.shape
    depth = w_in.shape[0]
    w_a = lru_lambda.shape[1]
    w_b = cm_dw_b.shape[1]
    w_c = ret_norm_g.shape[1]
    n_p, n_s = bp * tp, bs * ts
    row = lambda v: v.reshape(1, -1)

    x_p, x_s = x_prompt.reshape(n_p, d), x_sample.reshape(n_s, d)
    outs = {k: [] for k in ("p_h", "p_ca", "p_cb", "p_s", "s_h", "s_ca", "s_cb")}
    s_states = None
    for l in range(depth):
        wr_hi, wr_lo, b_route = _route_weights(moe_w_group[l], moe_b_group[l], moe_w_expert[l], moe_b_expert[l])
        lw = dict(
            w_a=w_a, w_b=w_b, w_c=w_c, g_mix=row(norm_mix_g[l]), g_ffn=row(norm_ffn_g[l]),
            conv_a_w=lru_conv_w[l], conv_a_b=row(lru_conv_b[l]),
            wg=_blockdiag_gate_weights(lru_wa[l], lru_wx[l]).astype(BF16),
            bg=row(jnp.concatenate([lru_ba[l], lru_bx[l]])), lam=row(lru_lambda[l]),
            conv_b_w=cm_dw_w[l], conv_b_b=row(cm_dw_b[l]), ln_g=row(cm_ln_g[l]), ln_b=row(cm_ln_b[l]),
            ret_g=row(ret_norm_g[l]), wr_hi=wr_hi, wr_lo=wr_lo, b_route=b_route,
        )
        xmid_p, xn_p, rt_p, rg_p, seen, h_p, ca_p, cb_p, s_p = _prompt_layer(x_p, bp, tp, l, w_in, w_out, lw)
        outs["p_h"].append(h_p[:, 0, :])
        outs["p_ca"].append(ca_p[:, SUBLANES - (CONV_A - 1):, :])
        outs["p_cb"].append(cb_p[:, 4 * SUBLANES - (CONV_B - 1):, :])
        outs["p_s"].append(_unpair_state(s_p))
        proj_s = _in_proj(x_s, lw["g_mix"], w_in, l)
        p_t = proj_s[:, :2 * w_a + 2 * w_b].reshape(bs, ts, -1).transpose(1, 0, 2)
        oab_t, h_s, ca_t, cb_t = _mix_sample_ab(
            p_t, state_lru_h, state_lru_conv[l].transpose(1, 0, 2), state_conv[l].transpose(1, 0, 2), l, lw)
        oab = oab_t.transpose(1, 0, 2).reshape(n_s, w_a + w_b)
        mix_s, s_states = _mix_sample_c(proj_s, bs, ts, oab, state_ret, l, s_states, lw)
        outs["s_h"].append(h_s)
        outs["s_ca"].append(ca_t.transpose(1, 0, 2))
        outs["s_cb"].append(cb_t.transpose(1, 0, 2))
        xmid_s, xn_s, rt_s, rg_s, counts = _out_proj(x_s, mix_s, w_out, l, seen, lw)
        x_p, x_s = _moe_routed((xmid_p, xn_p, rt_p, rg_p), (xmid_s, xn_s, rt_s, rg_s), counts,
                               moe_w1, moe_w3, moe_w2, l, row(final_norm_g), l == depth - 1)
    st = lambda k: jnp.stack(outs[k])
    return (x_p.reshape(bp, tp, d), x_s.reshape(bs, ts, d), st("p_h"), st("p_ca"), st("p_cb"), st("p_s"),
            st("s_h"), st("s_ca"), st("s_cb"), s_states)
```

```python
import functools
import math

import jax
import jax.numpy as jnp
from jax import lax
from jax.experimental import pallas as pl
from jax.experimental.pallas import tpu as pltpu

PAST_LEN = 16384
LRU_C = 8.0
CONV_A = 4
CONV_B = 31
NH_C = 8
RET_CHUNK = 128
ROPE_BASE = 10000.0
N_GROUPS = 4
EXPERTS_PER_GROUP = 8
N_EXPERTS = N_GROUPS * EXPERTS_PER_GROUP
EPS = 1e-6

LANES = 128
SUBLANES = 8
VMEM_LIMIT_BYTES = 56 * 1024 * 1024

F32 = jnp.float32
BF16 = jnp.bfloat16


def _cparams(*sem):
    return pltpu.CompilerParams(dimension_semantics=sem, vmem_limit_bytes=VMEM_LIMIT_BYTES)


def _row_tile(n):
    for t in (512, 256, 128, 64, 32, 16, 8):
        if n % t == 0:
            return t
    raise ValueError(f"row count {n} must be a multiple of 8")


def _resident(shape, index):
    return pl.BlockSpec(shape, lambda *_: index, pipeline_mode=pl.Buffered(1))


def _rms(x, g):
    ms = jnp.mean(x * x, axis=-1, keepdims=True)
    return x * lax.rsqrt(ms + EPS) * g


def _sigmoid(x):
    return jax.nn.sigmoid(x)


def _silu(x):
    return x * jax.nn.sigmoid(x)


def _layernorm(x, g, b):
    mu = jnp.mean(x, axis=-1, keepdims=True)
    xc = x - mu
    var = jnp.mean(xc * xc, axis=-1, keepdims=True)
    return xc * lax.rsqrt(var + EPS) * g + b


def _rope_pair(x, cos, sin_signed, first_half):
    partner = jnp.where(first_half, pltpu.roll(x, 96, 1), pltpu.roll(x, 32, 1))
    return x * cos + partner * sin_signed


def _head_norm_pair(o, head0, inv_d):
    zero = jnp.zeros_like(o)
    s0 = jnp.sum(jnp.where(head0, o, zero), axis=-1, keepdims=True)
    s1 = jnp.sum(jnp.where(head0, zero, o), axis=-1, keepdims=True)
    oc = o - jnp.where(head0, s0, s1) * inv_d
    q = oc * oc
    v0 = jnp.sum(jnp.where(head0, q, zero), axis=-1, keepdims=True)
    v1 = jnp.sum(jnp.where(head0, zero, q), axis=-1, keepdims=True)
    return oc * lax.rsqrt(jnp.where(head0, v0, v1) * inv_d + EPS)


def _mm(a, b, dims, parts):
    dot = lambda x, y: lax.dot_general(x, y, (dims, ((), ())), preferred_element_type=F32)
    ah, bh = a.astype(BF16), b.astype(BF16)
    if parts == 1:
        return dot(ah, bh)
    al = (a - ah.astype(F32)).astype(BF16)
    bl = (b - bh.astype(F32)).astype(BF16)
    return dot(ah, bh) + (dot(al, bh) + dot(ah, bl))


ROWS_BY_COLS = ((1,), (0,))
ROWS_BY_ROWS = ((1,), (1,))
COLS_BY_COLS = ((0,), (0,))


def _retention_intra_pair(q, k, v, dmat0, dmat1, head0, parts):
    outs = []
    for j, dm in ((0, dmat0), (1, dmat1)):
        keep = head0 if j == 0 else jnp.logical_not(head0)
        s = _mm(jnp.where(keep, q, 0.0), k, ROWS_BY_ROWS, parts)
        outs.append(_mm(s * dm, v, ROWS_BY_COLS, parts))
    return jnp.where(head0, outs[0], outs[1])


def _lru_gates(xa, wg, bg, lam, parts):
    w_a = xa.shape[-1]
    gates = _mm(xa, wg, ROWS_BY_COLS, parts) + bg
    r = _sigmoid(gates[:, :w_a])
    i = _sigmoid(gates[:, w_a:])
    z = -lam
    softplus = jnp.maximum(z, 0.0) + jnp.log1p(jnp.exp(-jnp.abs(z)))
    log_a = -LRU_C * r * softplus
    a = jnp.exp(log_a)
    bt = jnp.sqrt(-jnp.tanh(log_a) * (a * a + 1.0)) * (i * xa)
    return a, bt


ROUTE_TILE = 256
EXPERT_ROW0 = SUBLANES
assert EXPERTS_PER_GROUP == SUBLANES and N_GROUPS <= SUBLANES and EXPERT_ROW0 + N_EXPERTS <= LANES


def _split_bf16(x):
    hi = x.astype(BF16)
    return hi, (x - hi.astype(F32)).astype(BF16)


def _store_parts(ref, idx, val):
    hi = val.astype(BF16)
    ref[(0,) + idx] = hi
    if ref.shape[0] == 2:
        ref[(1,) + idx] = (val - hi.astype(F32)).astype(BF16)


def _dot_parts(a_ref, a_idx, w_ref, w_idx):
    dot = lambda i, j: jnp.dot(a_ref[(i,) + a_idx], w_ref[(j,) + w_idx], preferred_element_type=F32)
    if a_ref.shape[0] == 1:
        return dot(0, 0)
    return dot(0, 0) + (dot(1, 0) + dot(0, 1))


def _lane_tile(x, width):
    return x if width == x.shape[1] else jnp.concatenate([x] * (width // x.shape[1]), axis=1)


def _bf16_parts(w):
    hi = lax.reduce_precision(w, exponent_bits=8, mantissa_bits=7)
    return hi.astype(BF16), (w - hi).astype(BF16)


def _weight_parts(w, parts):
    if parts == 1:
        return w.astype(BF16)[None]
    return jnp.stack(_bf16_parts(w))


def _route_weights(w_group, b_group, w_expert, b_expert):
    d = w_group.shape[0]
    w = jnp.zeros((LANES, d), F32).at[:N_GROUPS].set(w_group.T).at[EXPERT_ROW0:EXPERT_ROW0 + N_EXPERTS].set(w_expert.T)
    b = jnp.zeros((LANES,), F32).at[:N_GROUPS].set(b_group).at[EXPERT_ROW0:EXPERT_ROW0 + N_EXPERTS].set(b_expert)
    hi, lo = _bf16_parts(w)
    return hi, lo, jnp.broadcast_to(b[:, None], (LANES, LANES))


def _ffn_norm_route(xm, g_ffn, wr_hi, wr_lo, b_route, carry_ref, xn_ref, rt_ref, rg_ref, counts_ref):
    xn = _rms(xm, g_ffn)
    xn_ref[...] = xn
    tm = xn.shape[0]
    xh, xl = _split_bf16(xn)
    dot_t = lambda w, x: lax.dot_general(w, x, (((1,), (1,)), ((), ())), preferred_element_type=F32)
    lt = dot_t(wr_hi, xh) + (dot_t(wr_hi, xl) + dot_t(wr_lo, xh)) + _lane_tile(b_route, tm)
    row8 = lax.broadcasted_iota(jnp.int32, (SUBLANES, tm), 0)
    neg = jnp.full((SUBLANES, tm), -jnp.inf, F32)
    big = jnp.full((SUBLANES, tm), LANES, jnp.int32)
    first = lambda hit: jnp.min(jnp.where(hit, row8, big), axis=0, keepdims=True)
    gl = jnp.where(row8 < N_GROUPS, lt[0:SUBLANES, :], neg)
    gmax = jnp.max(gl, axis=0, keepdims=True)
    p_top = 1.0 / jnp.sum(jnp.exp(gl - gmax), axis=0, keepdims=True)
    g_sel = first(gl == gmax)
    slab = lambda g: lt[EXPERT_ROW0 + g * SUBLANES:EXPERT_ROW0 + (g + 1) * SUBLANES, :]
    el = slab(N_GROUPS - 1)
    for g in reversed(range(N_GROUPS - 1)):
        el = jnp.where(g_sel == g, slab(g), el)
    v1 = jnp.max(el, axis=0, keepdims=True)
    i1 = first(el == v1)
    el2 = jnp.where(row8 == i1, neg, el)
    v2 = jnp.max(el2, axis=0, keepdims=True)
    i2 = first(el2 == v2)
    ex = jnp.exp(v2 - v1)
    den = 1.0 + ex
    g1 = p_top * (1.0 / den)
    g2 = p_top * (ex / den)
    e1 = g_sel * EXPERTS_PER_GROUP + i1
    e2 = g_sel * EXPERTS_PER_GROUP + i2
    rowe = lax.broadcasted_iota(jnp.int32, (LANES, tm), 0)
    oh1 = rowe == e1
    oh2 = rowe == e2
    onehot = jnp.logical_or(oh1, oh2).astype(BF16)
    earlier = (lax.broadcasted_iota(jnp.int32, (tm, tm), 0)
               < lax.broadcasted_iota(jnp.int32, (tm, tm), 1)).astype(BF16)
    seen = _lane_tile(carry_ref[...], tm) + jnp.dot(onehot, earlier, preferred_element_type=F32)
    rank1 = jnp.sum(jnp.where(oh1, seen, 0.0), axis=0, keepdims=True).astype(jnp.int32)
    rank2 = jnp.sum(jnp.where(oh2, seen, 0.0), axis=0, keepdims=True).astype(jnp.int32)
    total = carry_ref[...] + jnp.sum(onehot.astype(F32), axis=1, keepdims=True)
    carry_ref[...] = total
    counts_ref[...] = total
    zi = jnp.zeros((SUBLANES, tm), jnp.int32)
    rt_ref[...] = jnp.where(row8 == 0, e1, jnp.where(row8 == 1, e2, jnp.where(
        row8 == 2, rank1, jnp.where(row8 == 3, rank2, zi))))
    rg_ref[...] = jnp.where(rowe == 0, g1, jnp.where(rowe == 1, g2, 0.0)).T


CONV_B_ROWS = 64


def _row_copy(src_ref, src_row, dst_ref, dst_row, sem):
    return pltpu.make_async_copy(src_ref.at[pl.ds(src_row, 1), :], dst_ref.at[pl.ds(dst_row, 1), :], sem)


def _prompt_layer_kernel(*refs, tc, w_a, w_b, w_c, chunk, fused_combine):
    if fused_combine:
        (p1c_ref, p2c_ref, p1n_ref, p2n_ref, rgp_ref, ys_ref), refs = refs[:6], refs[6:]
        (y_ref, ysem), refs = refs[-2:], refs[:-2]
    (x_ref, gmix_ref, winb_ref, cos_ref, sin_ref, h0_ref, bufa0_ref, bufb0_ref, s0_ref,
     caw_ref, cab_ref, wg_ref, bg_ref, lam_ref,
     cbw_ref, cbb_ref, lng_ref, lnb_ref, rng_ref,
     dmat_ref, xi_ref, zeta_ref, gmat_ref,
     woutb_ref, gffn_ref, wrh_ref, wrl_ref, br_ref,
     xmid_ref, xn_ref, rt_ref, rg_ref, counts_ref,
     hout_ref, bufa_out_ref, bufb_out_ref, sout_ref,
     xin_ref, u_ref, proj_ref, mix_ref,
     xpa_ref, xpb_ref, sh_ref, a_ref, b_ref, hseq_ref, hcar_ref, s_ref, carry_ref) = refs
    t = pl.program_id(1)
    n_t = pl.num_programs(1)
    pa = SUBLANES
    pb = 4 * SUBLANES
    parts = winb_ref.shape[0]
    everything = (slice(None), slice(None))

    if fused_combine:
        step = pl.program_id(0) * n_t + t
        slot = step % 2

        def gather(p1_ref, p2_ref, into):
            for r in range(tc):
                _row_copy(ys_ref, p1_ref[r], y_ref.at[into, 0], r, ysem.at[into]).start(priority=0)
                _row_copy(ys_ref, p2_ref[r], y_ref.at[into, 1], r, ysem.at[into]).start(priority=1)

        @pl.when(step == 0)
        def _():
            gather(p1c_ref, p2c_ref, 0)

        for _ in range(2 * tc):
            _row_copy(ys_ref, 0, y_ref.at[slot, 0], 0, ysem.at[slot]).wait()
        gates = rgp_ref[...]
        xin_ref[...] = x_ref[...] + (gates[:, 0:1] * y_ref[slot, 0] + gates[:, 1:2] * y_ref[slot, 1])
    else:
        xin_ref[...] = x_ref[...]

    @pl.when(jnp.logical_and(pl.program_id(0) == 0, t == 0))
    def _():
        carry_ref[...] = jnp.zeros_like(carry_ref)

    @pl.when(t == 0)
    def _():
        xpa_ref[0:pa, :] = bufa0_ref[0]
        xpb_ref[0:pb, :] = bufb0_ref[0]
        hcar_ref[...] = h0_ref[0]
        s_ref[...] = s0_ref[0]

    _store_parts(u_ref, everything, _rms(xin_ref[...], gmix_ref[...]))

    def in_proj(c0, c1):
        for c in range(c0, c1, 4 * LANES):
            cols = (slice(None), slice(c, min(c + 4 * LANES, c1)))
            proj_ref[cols] = _dot_parts(u_ref, everything, winb_ref, cols)

    in_proj(0, 2 * w_a)

    xpa_ref[pa:pa + tc, :] = proj_ref[:, w_a:2 * w_a]
    xa = cab_ref[...] + caw_ref[0:1, :] * xpa_ref[pa - 3:pa - 3 + tc, :]
    for j in range(1, CONV_A):
        xa = xa + caw_ref[j:j + 1, :] * xpa_ref[pa - 3 + j:pa - 3 + j + tc, :]
    a, bt = _lru_gates(xa, wg_ref[...], bg_ref[...], lam_ref[...], parts)
    a_ref[...] = a
    b_ref[...] = bt
    row8 = lax.broadcasted_iota(jnp.int32, (SUBLANES, w_a), 0)

    def scan_block(i, hprev):
        r0 = pl.multiple_of(i * SUBLANES, SUBLANES)
        ab = a_ref[pl.ds(r0, SUBLANES), :]
        bb = b_ref[pl.ds(r0, SUBLANES), :]
        for k in (1, 2, 4):
            a_sh = jnp.where(row8 >= k, pltpu.roll(ab, k, 0), 1.0)
            b_sh = jnp.where(row8 >= k, pltpu.roll(bb, k, 0), 0.0)
            bb = ab * b_sh + bb
            ab = ab * a_sh
        h = ab * hprev + bb
        hseq_ref[pl.ds(r0, SUBLANES), :] = h
        return jnp.broadcast_to(h[SUBLANES - 1:SUBLANES, :], (SUBLANES, w_a))

    hcar_ref[...] = lax.fori_loop(0, tc // SUBLANES, scan_block, hcar_ref[...])
    if fused_combine:
        gather(p1n_ref, p2n_ref, 1 - slot)
    in_proj(2 * w_a, proj_ref.shape[1])
    _store_parts(mix_ref, (slice(None), slice(0, w_a)), jax.nn.gelu(proj_ref[:, 0:w_a]) * hseq_ref[...])
    xpa_ref[0:pa, :] = xpa_ref[tc:tc + pa, :]

    b0 = 2 * w_a
    xpb_ref[pb:pb + tc, :] = proj_ref[:, b0:b0 + w_b] * _sigmoid(proj_ref[:, b0 + w_b:b0 + 2 * w_b])
    first = pb - (CONV_B - 1)
    for r in range(SUBLANES):
        span = tc + (CONV_B - 1 - r) // SUBLANES * SUBLANES
        sh_ref[r, 0:span, :] = xpb_ref[first + r:first + r + span, :]
    for blk in range(tc // CONV_B_ROWS):
        r0 = blk * CONV_B_ROWS
        acc = jnp.broadcast_to(cbb_ref[...], (CONV_B_ROWS, w_b))
        for j in range(CONV_B):
            r = j % SUBLANES
            acc = acc + cbw_ref[j:j + 1, :] * sh_ref[r, r0 + j - r:r0 + j - r + CONV_B_ROWS, :]
        out_b = _silu(_layernorm(acc, lng_ref[...], lnb_ref[...]))
        _store_parts(mix_ref, (slice(r0, r0 + CONV_B_ROWS), slice(w_a, w_a + w_b)), out_b)
    xpb_ref[0:pb, :] = xpb_ref[tc:tc + pb, :]

    c0 = 2 * w_a + 2 * w_b
    n_pair = w_c // LANES
    lane = lax.broadcasted_iota(jnp.int32, (chunk, LANES), 1)
    head0 = lane < (LANES // 2)
    first_half = (lane % (LANES // 2)) < (LANES // 4)
    rl = lax.broadcasted_iota(jnp.int32, (LANES, LANES), 0) < (LANES // 2)
    cl = lax.broadcasted_iota(jnp.int32, (LANES, LANES), 1) < (LANES // 2)
    same_head = rl == cl
    inv_d = 2.0 / LANES
    for c in range(tc // chunk):
        rows = slice(c * chunk, (c + 1) * chunk)
        cos = cos_ref[rows, :]
        sin = sin_ref[rows, :]
        for p in range(n_pair):
            cq = c0 + p * LANES
            q = _rope_pair(proj_ref[rows, cq:cq + LANES], cos, sin, first_half)
            k = _rope_pair(proj_ref[rows, cq + w_c:cq + w_c + LANES], cos, sin, first_half) * (2.0 / LANES) ** 0.5
            v = proj_ref[rows, cq + 2 * w_c:cq + 2 * w_c + LANES]
            g = proj_ref[rows, cq + 3 * w_c:cq + 3 * w_c + LANES]
            o = _retention_intra_pair(q, k, v, dmat_ref[2 * p], dmat_ref[2 * p + 1], head0, parts)
            s_old = s_ref[p]
            o = o + _mm(q, s_old, ROWS_BY_COLS, parts) * xi_ref[p]
            upd = _mm(k * zeta_ref[p], v, COLS_BY_COLS, parts)
            s_ref[p] = gmat_ref[p] * s_old + jnp.where(same_head, upd, 0.0)
            on = _head_norm_pair(o, head0, inv_d) * rng_ref[:, p * LANES:(p + 1) * LANES]
            _store_parts(mix_ref, (rows, slice(w_a + w_b + p * LANES, w_a + w_b + (p + 1) * LANES)), _silu(g) * on)

    xm = xin_ref[...] + _dot_parts(mix_ref, everything, woutb_ref, everything)
    xmid_ref[...] = xm
    _ffn_norm_route(xm, gffn_ref[...], wrh_ref[...], wrl_ref[...], br_ref[...], carry_ref,
                    xn_ref, rt_ref, rg_ref, counts_ref)

    @pl.when(t == n_t - 1)
    def _():
        hout_ref[0] = hcar_ref[...]
        bufa_out_ref[0] = xpa_ref[0:pa, :]
        bufb_out_ref[0] = xpb_ref[0:pb, :]
        sout_ref[0] = s_ref[...]

    if fused_combine:
        @pl.when(step == pl.num_programs(0) * n_t - 1)
        def _():
            for _ in range(2 * tc):
                _row_copy(ys_ref, 0, y_ref.at[1 - slot, 0], 0, ysem.at[1 - slot]).wait()


def _retention_tables(chunk):
    log_g = jnp.log1p(-jnp.exp2(-5.0 - jnp.arange(NH_C, dtype=F32)))
    idx = jnp.arange(chunk, dtype=F32)
    diff = idx[:, None] - idx[None, :]
    causal = diff >= 0
    dmat = jnp.where(causal[None], jnp.exp(jnp.where(causal, diff, 0.0)[None] * log_g[:, None, None]), 0.0)
    xi = jnp.exp((idx + 1.0)[None, :] * log_g[:, None])
    zeta = jnp.exp((chunk - 1.0 - idx)[None, :] * log_g[:, None])
    g_chunk = jnp.exp(chunk * log_g)
    return dmat, xi, zeta, g_chunk


def _pair_lanes(per_head):
    h, rows = per_head.shape
    half = LANES // 2
    x = jnp.broadcast_to(per_head.reshape(h // 2, 2, rows, 1), (h // 2, 2, rows, half))
    return x.transpose(0, 2, 1, 3).reshape(h // 2, rows, LANES)


def _rope_tables(pos):
    half = LANES // 4
    inv = ROPE_BASE ** (-jnp.arange(half, dtype=F32) / half)
    ang = pos.astype(F32)[:, None] * inv[None, :]
    cos, sin = jnp.cos(ang), jnp.sin(ang)
    return jnp.concatenate([cos] * 4, axis=1), jnp.concatenate([-sin, sin, -sin, sin], axis=1)


def _blockdiag_gate_weights(wa, wx):
    nh, hd, _ = wa.shape
    eye = jnp.eye(nh, dtype=wa.dtype)
    da = (eye[:, None, :, None] * wa[:, :, None, :]).reshape(nh * hd, nh * hd)
    dx = (eye[:, None, :, None] * wx[:, :, None, :]).reshape(nh * hd, nh * hd)
    return jnp.concatenate([da, dx], axis=1)


def _prompt_layer(x, n_seq, seq_len, layer, w_in, w_out, lw, parts, combine=None):
    w_a, w_b, w_c = lw["w_a"], lw["w_b"], lw["w_c"]
    n, d = x.shape
    d_in = w_in.shape[2]
    d_mix = w_a + w_b + w_c
    chunk = math.gcd(seq_len, RET_CHUNK)
    tc = ROUTE_TILE
    assert seq_len % tc == 0 and tc % chunk == 0 and tc % CONV_B_ROWS == 0
    n_t = seq_len // tc
    n_pair = w_c // LANES
    dmat, xi, zeta, g_chunk = _retention_tables(chunk)
    half = LANES // 2
    gm = jnp.zeros((n_pair, 2, half, 2, half), F32)
    gm = gm.at[:, 0, :, 0, :].set(g_chunk[0::2, None, None]).at[:, 1, :, 1, :].set(g_chunk[1::2, None, None])
    cos, sin = _rope_tables(jnp.arange(seq_len, dtype=jnp.int32))
    zeros = lambda *s: jnp.zeros(s, F32)
    full = lambda shape: _resident(shape, (0,) * len(shape))
    per_seq = lambda shape: pl.BlockSpec((1,) + shape, lambda b, t: (b,) + (0,) * len(shape))
    tile = lambda width: pl.BlockSpec((tc, width), lambda b, t: (b * n_t + t, 0))
    kern = functools.partial(_prompt_layer_kernel, tc=tc, w_a=w_a, w_b=w_b, w_c=w_c, chunk=chunk,
                             fused_combine=combine is not None)
    extra_in, extra_scratch, extra_args = [], [], []
    if combine is not None:
        rg_prev, pos1, pos2, ys = combine
        last = n // tc - 1
        cur = lambda: pl.BlockSpec((tc,), lambda b, t: (b * n_t + t,), memory_space=pltpu.SMEM)
        nxt = lambda: pl.BlockSpec((tc,), lambda b, t: (jnp.minimum(b * n_t + t + 1, last),),
                                   memory_space=pltpu.SMEM)
        extra_in = [cur(), cur(), nxt(), nxt(), tile(LANES), pl.BlockSpec(memory_space=pl.ANY)]
        extra_args = [pos1, pos2, pos1, pos2, rg_prev, ys]
        extra_scratch = [pltpu.VMEM((2, 2, tc, d), F32), pltpu.SemaphoreType.DMA((2,))]
    return pl.pallas_call(
        kern,
        grid=(n_seq, n_t),
        in_specs=extra_in + [
            tile(d), full((1, d)), full((parts, d, d_in)),
            pl.BlockSpec((tc, LANES), lambda b, t: (t, 0)),
            pl.BlockSpec((tc, LANES), lambda b, t: (t, 0)),
            per_seq((SUBLANES, w_a)), per_seq((SUBLANES, w_a)), per_seq((4 * SUBLANES, w_b)),
            per_seq((n_pair, LANES, LANES)),
            full((CONV_A, w_a)), full((1, w_a)), full((w_a, 2 * w_a)), full((1, 2 * w_a)), full((1, w_a)),
            full((CONV_B, w_b)), full((1, w_b)), full((1, w_b)), full((1, w_b)), full((1, w_c)),
            full((NH_C, chunk, chunk)), full((n_pair, chunk, LANES)), full((n_pair, chunk, LANES)),
            full((n_pair, LANES, LANES)),
            full((parts, d_mix, d)), full((1, d)),
            full((LANES, d)), full((LANES, d)), full((LANES, LANES)),
        ],
        out_specs=[
            tile(d), tile(d),
            pl.BlockSpec((SUBLANES, tc), lambda b, t: (0, b * n_t + t)),
            tile(LANES),
            pl.BlockSpec((LANES, LANES), lambda b, t: (0, 0)),
            per_seq((SUBLANES, w_a)), per_seq((SUBLANES, w_a)), per_seq((4 * SUBLANES, w_b)),
            per_seq((n_pair, LANES, LANES)),
        ],
        out_shape=[
            jax.ShapeDtypeStruct((n, d), F32),
            jax.ShapeDtypeStruct((n, d), F32),
            jax.ShapeDtypeStruct((SUBLANES, n), jnp.int32),
            jax.ShapeDtypeStruct((n, LANES), F32),
            jax.ShapeDtypeStruct((LANES, LANES), F32),
            jax.ShapeDtypeStruct((n_seq, SUBLANES, w_a), F32),
            jax.ShapeDtypeStruct((n_seq, SUBLANES, w_a), F32),
            jax.ShapeDtypeStruct((n_seq, 4 * SUBLANES, w_b), F32),
            jax.ShapeDtypeStruct((n_seq, n_pair, LANES, LANES), F32),
        ],
        scratch_shapes=[
            pltpu.VMEM((tc, d), F32),
            pltpu.VMEM((parts, tc, d), BF16),
            pltpu.VMEM((tc, d_in), F32),
            pltpu.VMEM((parts, tc, d_mix), BF16),
            pltpu.VMEM((SUBLANES + tc, w_a), F32),
            pltpu.VMEM((4 * SUBLANES + tc, w_b), F32),
            pltpu.VMEM((SUBLANES, 3 * SUBLANES + tc, w_b), F32),
            pltpu.VMEM((tc, w_a), F32),
            pltpu.VMEM((tc, w_a), F32),
            pltpu.VMEM((tc, w_a), F32),
            pltpu.VMEM((SUBLANES, w_a), F32),
            pltpu.VMEM((n_pair, LANES, LANES), F32),
            pltpu.VMEM((LANES, LANES), F32),
        ] + extra_scratch,
        compiler_params=_cparams("arbitrary", "arbitrary"),
        name="prompt_layer",
    )(*extra_args, x, lw["g_mix"], _weight_parts(w_in[layer], parts), cos, sin,
      zeros(n_seq, SUBLANES, w_a), zeros(n_seq, SUBLANES, w_a), zeros(n_seq, 4 * SUBLANES, w_b),
      zeros(n_seq, n_pair, LANES, LANES),
      lw["conv_a_w"], lw["conv_a_b"], lw["wg"], lw["bg"], lw["lam"],
      lw["conv_b_w"], lw["conv_b_b"], lw["ln_g"], lw["ln_b"], lw["ret_g"],
      dmat, _pair_lanes(xi), _pair_lanes(zeta), gm.reshape(n_pair, LANES, LANES),
      _weight_parts(w_out[layer], parts), lw["g_ffn"], lw["wr_hi"], lw["wr_lo"], lw["b_route"])


def _unpair_state(s_pairs):
    b, n_pair = s_pairs.shape[:2]
    half = LANES // 2
    s = s_pairs.reshape(b, n_pair, 2, half, 2, half)
    return jnp.stack([s[:, :, 0, :, 0, :], s[:, :, 1, :, 1, :]], axis=2).reshape(b, 2 * n_pair, half, half)


def _in_proj_kernel(x_ref, g_ref, w_ref, o_ref, *, parts):
    o_ref[...] = _mm(_rms(x_ref[...], g_ref[...]), w_ref[0], ROWS_BY_COLS, parts)


def _in_proj(x, g, w_in, layer, parts):
    n, d = x.shape
    d_in = w_in.shape[2]
    tm = _row_tile(n)
    return pl.pallas_call(
        functools.partial(_in_proj_kernel, parts=parts),
        grid=(n // tm,),
        in_specs=[
            pl.BlockSpec((tm, d), lambda i: (i, 0)),
            _resident((1, d), (0, 0)),
            _resident((1, d, d_in), (layer, 0, 0)),
        ],
        out_specs=pl.BlockSpec((tm, d_in), lambda i: (i, 0)),
        out_shape=jax.ShapeDtypeStruct((n, d_in), F32),
        compiler_params=_cparams("arbitrary"),
        name="in_proj_sample",
    )(x, g, w_in)


def _mix_sample_ab_kernel(p_ref, h0_ref, bufa_ref, bufb_ref,
                          caw_ref, cab_ref, wg_ref, bg_ref, lam_ref,
                          cbw_ref, cbb_ref, lng_ref, lnb_ref,
                          o_ref, h_ref, bufa_out_ref, bufb_out_ref, *, ts, w_a, w_b, parts):
    xpa = [bufa_ref[j] for j in range(CONV_A - 1)] + [p_ref[t, :, w_a:2 * w_a] for t in range(ts)]
    xa = []
    for t in range(ts):
        acc = cab_ref[...] + caw_ref[0:1, :] * xpa[t]
        for j in range(1, CONV_A):
            acc = acc + caw_ref[j:j + 1, :] * xpa[t + j]
        xa.append(acc)
    nb = xa[0].shape[0]
    a, bt = _lru_gates(jnp.concatenate(xa, axis=0), wg_ref[...], bg_ref[...], lam_ref[...], parts)
    h = h0_ref[...]
    for t in range(ts):
        h = a[t * nb:(t + 1) * nb, :] * h + bt[t * nb:(t + 1) * nb, :]
        o_ref[t, :, 0:w_a] = jax.nn.gelu(p_ref[t, :, 0:w_a]) * h
    h_ref[...] = h
    for j in range(CONV_A - 1):
        bufa_out_ref[j] = xpa[ts + j]
    b0 = 2 * w_a
    hist = CONV_B - 1
    glu = [p_ref[t, :, b0:b0 + w_b] * _sigmoid(p_ref[t, :, b0 + w_b:b0 + 2 * w_b]) for t in range(ts)]
    xpb = lambda i: bufb_ref[i] if i < hist else glu[i - hist]
    for t in range(ts):
        acc = cbb_ref[...] + cbw_ref[0:1, :] * xpb(t)
        for j in range(1, CONV_B):
            acc = acc + cbw_ref[j:j + 1, :] * xpb(t + j)
        o_ref[t, :, w_a:w_a + w_b] = _silu(_layernorm(acc, lng_ref[...], lnb_ref[...]))
    for i in range(hist):
        bufb_out_ref[i] = xpb(ts + i)


def _mix_sample_ab(p_t, h0, bufa_t, bufb_t, layer, lw, parts):
    ts, n_seq, width = p_t.shape
    w_a, w_b = lw["w_a"], lw["w_b"]
    nb = 32 if n_seq % 32 == 0 else n_seq
    full = lambda shape: _resident(shape, (0,) * len(shape))
    kern = functools.partial(_mix_sample_ab_kernel, ts=ts, w_a=w_a, w_b=w_b, parts=parts)
    return pl.pallas_call(
        kern,
        grid=(n_seq // nb,),
        in_specs=[
            pl.BlockSpec((ts, nb, width), lambda i: (0, i, 0)),
            pl.BlockSpec((None, nb, w_a), lambda i: (layer, i, 0)),
            pl.BlockSpec((CONV_A - 1, nb, w_a), lambda i: (0, i, 0)),
            pl.BlockSpec((CONV_B - 1, nb, w_b), lambda i: (0, i, 0)),
            full((CONV_A, w_a)), full((1, w_a)), full((w_a, 2 * w_a)), full((1, 2 * w_a)), full((1, w_a)),
            full((CONV_B, w_b)), full((1, w_b)), full((1, w_b)), full((1, w_b)),
        ],
        out_specs=[
            pl.BlockSpec((ts, nb, w_a + w_b), lambda i: (0, i, 0)),
            pl.BlockSpec((nb, w_a), lambda i: (i, 0)),
            pl.BlockSpec((CONV_A - 1, nb, w_a), lambda i: (0, i, 0)),
            pl.BlockSpec((CONV_B - 1, nb, w_b), lambda i: (0, i, 0)),
        ],
        out_shape=[
            jax.ShapeDtypeStruct((ts, n_seq, w_a + w_b), F32),
            jax.ShapeDtypeStruct((n_seq, w_a), F32),
            jax.ShapeDtypeStruct((CONV_A - 1, n_seq, w_a), F32),
            jax.ShapeDtypeStruct((CONV_B - 1, n_seq, w_b), F32),
        ],
        compiler_params=_cparams("arbitrary"),
        name="mix_sample_ab",
    )(p_t, h0, bufa_t, bufb_t,
      lw["conv_a_w"], lw["conv_a_b"], lw["wg"], lw["bg"], lw["lam"],
      lw["conv_b_w"], lw["conv_b_b"], lw["ln_g"], lw["ln_b"])


def _mix_sample_c_kernel(q_ref, k_ref, v_ref, g_ref, oab_ref, s_ref, cos_ref, sin_ref,
                         dmat_ref, xi_ref, zeta_ref, gcol_ref, rng_ref, tile_ref, fold_ref, *rest,
                         rows, ts, w_c, w_ab, layer, parts):
    if layer:
        prev_ref, mix_ref, sout_ref, qs_ref, kzs_ref, vs_ref, oi_ref = rest
        for l in range(layer):
            sout_ref[l] = prev_ref[l]
    else:
        mix_ref, sout_ref, qs_ref, kzs_ref, vs_ref, oi_ref = rest
    n_pair = w_c // LANES
    hd = LANES // 2
    lane = lax.broadcasted_iota(jnp.int32, (rows, LANES), 1)
    head0 = lane < hd
    first_half = (lane % hd) < (hd // 2)
    cos = cos_ref[...]
    sin = sin_ref[...]
    for p in range(n_pair):
        cs = slice(p * LANES, (p + 1) * LANES)
        q = _rope_pair(q_ref[:, cs], cos, sin, first_half)
        k = _rope_pair(k_ref[:, cs], cos, sin, first_half) * (1.0 / hd) ** 0.5
        v = v_ref[:, cs]
        oi_ref[:, cs] = _retention_intra_pair(q, k, v, dmat_ref[2 * p], dmat_ref[2 * p + 1], head0, parts)
        qs_ref[:, cs] = q
        kzs_ref[:, cs] = k * zeta_ref[:, cs]
        vs_ref[:, cs] = v
    grp = SUBLANES
    per_grp = grp // ts
    stack = NH_C * grp
    srow = lax.broadcasted_iota(jnp.int32, (stack, w_c), 0)
    slane = lax.broadcasted_iota(jnp.int32, (stack, w_c), 1)
    own_head = (srow // grp) == (slane // hd)
    tile_m = tile_ref[...]
    fold_m = fold_ref[...]
    for gi in range(rows // grp):
        rs = slice(gi * grp, (gi + 1) * grp)
        q8 = jnp.concatenate([qs_ref[rs, :]] * NH_C, axis=0)
        kz8 = jnp.concatenate([kzs_ref[rs, :]] * NH_C, axis=0)
        v8 = jnp.concatenate([vs_ref[rs, :]] * NH_C, axis=0)
        rest = jnp.where(own_head, v8, 0.0)
        vr = jnp.zeros((stack, hd), F32)
        for _ in range(3 if parts == 2 else 1):
            hi = rest.astype(BF16)
            vr = vr + jnp.dot(hi, fold_m, preferred_element_type=F32)
            rest = rest - hi.astype(F32)
        o_stack = jnp.zeros((stack, hd), F32)
        for s2 in range(per_grp):
            sq = gi * per_grp + s2
            own_seq = ((srow % grp) // ts) == s2
            sel = jnp.logical_and(own_head, own_seq)
            s_old = s_ref[0, sq].reshape(w_c, hd)
            o_stack = o_stack + _mm(jnp.where(sel, q8, 0.0), s_old, ROWS_BY_COLS, parts)
            upd = _mm(jnp.where(sel, kz8, 0.0), vr, COLS_BY_COLS, parts)
            sout_ref[layer, sq] = (gcol_ref[...] * s_old + upd).reshape(NH_C, hd, hd)
        p1 = o_stack.astype(BF16)
        r1 = o_stack - p1.astype(F32)
        p2 = r1.astype(BF16)
        p3 = (r1 - p2.astype(F32)).astype(BF16)
        spread = (jnp.dot(p1, tile_m, preferred_element_type=F32)
                  + jnp.dot(p2, tile_m, preferred_element_type=F32)
                  + jnp.dot(p3, tile_m, preferred_element_type=F32))
        spread = jnp.where(own_head, spread, 0.0).reshape(NH_C, grp, w_c)
        oi_ref[rs, :] = oi_ref[rs, :] + jnp.sum(spread, axis=0) * xi_ref[rs, :]
    mix_ref[:, 0:w_ab] = oab_ref[...]
    for p in range(n_pair):
        cs = slice(p * LANES, (p + 1) * LANES)
        on = _head_norm_pair(oi_ref[:, cs], head0, 1.0 / hd) * rng_ref[:, cs]
        mix_ref[:, w_ab + p * LANES:w_ab + (p + 1) * LANES] = _silu(g_ref[:, cs]) * on


def _mix_sample_c(proj, n_seq, ts, oab, state, layer, prev_states, lw, parts):
    w_a, w_b, w_c = lw["w_a"], lw["w_b"], lw["w_c"]
    hd = LANES // 2
    n_rows = n_seq * ts
    rows = 64 if n_rows % 64 == 0 else n_rows
    assert SUBLANES % ts == 0 and rows % SUBLANES == 0
    seq_per = rows // ts
    c0 = 2 * w_a + 2 * w_b
    assert c0 % w_c == 0
    cb = c0 // w_c
    log_g = jnp.log1p(-jnp.exp2(-5.0 - jnp.arange(NH_C, dtype=F32)))
    r = jnp.arange(rows)
    step = (r % ts).astype(F32)
    diff = step[:, None] - step[None, :]
    live = jnp.logical_and((r[:, None] // ts) == (r[None, :] // ts), diff >= 0)
    dmat = jnp.where(live[None], jnp.exp(jnp.where(live, diff, 0.0)[None] * log_g[:, None, None]), 0.0)
    xi = jnp.repeat(jnp.exp((step + 1.0)[None, :] * log_g[:, None]).T, hd, axis=1)
    zeta = jnp.repeat(jnp.exp((ts - 1.0 - step)[None, :] * log_g[:, None]).T, hd, axis=1)
    gcol = jnp.broadcast_to(jnp.repeat(jnp.exp(ts * log_g), hd)[:, None], (w_c, hd))
    pos = PAST_LEN + (r % ts).astype(jnp.int32)
    cos, sin = _rope_tables(pos)
    tile_m = jnp.tile(jnp.eye(hd, dtype=BF16), (1, NH_C))
    fold_m = jnp.tile(jnp.eye(hd, dtype=BF16), (NH_C, 1))
    full = lambda shape: _resident(shape, (0,) * len(shape))
    col = lambda j: pl.BlockSpec((rows, w_c), lambda i: (i, cb + j))
    states = lambda n_layers, first: pl.BlockSpec((n_layers, seq_per, NH_C, hd, hd), lambda i: (first, i, 0, 0, 0))
    kern = functools.partial(_mix_sample_c_kernel, rows=rows, ts=ts, w_c=w_c, w_ab=w_a + w_b, layer=layer,
                             parts=parts)
    in_specs = [
        col(0), col(1), col(2), col(3),
        pl.BlockSpec((rows, w_a + w_b), lambda i: (i, 0)),
        states(1, layer),
        full((rows, LANES)), full((rows, LANES)),
        full((NH_C, rows, rows)), full((rows, w_c)), full((rows, w_c)), full((w_c, hd)), full((1, w_c)),
        full((hd, w_c)), full((w_c, hd)),
    ]
    args = [proj, proj, proj, proj, oab, state, cos, sin, dmat, xi, zeta, gcol, lw["ret_g"], tile_m, fold_m]
    if layer:
        in_specs.append(states(layer, 0))
        args.append(prev_states)
    return pl.pallas_call(
        kern,
        grid=(n_rows // rows,),
        in_specs=in_specs,
        out_specs=[
            pl.BlockSpec((rows, w_a + w_b + w_c), lambda i: (i, 0)),
            states(layer + 1, 0),
        ],
        out_shape=[
            jax.ShapeDtypeStruct((n_rows, w_a + w_b + w_c), F32),
            jax.ShapeDtypeStruct((layer + 1,) + state.shape[1:], F32),
        ],
        scratch_shapes=[pltpu.VMEM((rows, w_c), F32)] * 4,
        compiler_params=_cparams("arbitrary"),
        name="mix_sample_c",
    )(*args)


def _out_proj_kernel(x_ref, mix_ref, w_ref, g_ref, wrh_ref, wrl_ref, br_ref, seen_ref,
                     xmid_ref, xn_ref, rt_ref, rg_ref, counts_ref, carry_ref, *, parts):
    @pl.when(pl.program_id(0) == 0)
    def _():
        carry_ref[...] = seen_ref[...]

    xm = x_ref[...] + _mm(mix_ref[...], w_ref[0], ROWS_BY_COLS, parts)
    xmid_ref[...] = xm
    _ffn_norm_route(xm, g_ref[...], wrh_ref[...], wrl_ref[...], br_ref[...], carry_ref,
                    xn_ref, rt_ref, rg_ref, counts_ref)


def _out_proj(x, mix, w_out, layer, seen, lw, parts):
    n, d = x.shape
    d_mix = mix.shape[1]
    tm = ROUTE_TILE
    assert n % tm == 0
    full = lambda shape: _resident(shape, (0,) * len(shape))
    rows = lambda width: pl.BlockSpec((tm, width), lambda i: (i, 0))
    return pl.pallas_call(
        functools.partial(_out_proj_kernel, parts=parts),
        grid=(n // tm,),
        in_specs=[rows(d), rows(d_mix), _resident((1, d_mix, d), (layer, 0, 0)),
                  full((1, d)), full((LANES, d)), full((LANES, d)), full((LANES, LANES)), full((LANES, LANES))],
        out_specs=[rows(d), rows(d), pl.BlockSpec((SUBLANES, tm), lambda i: (0, i)), rows(LANES),
                   pl.BlockSpec((LANES, LANES), lambda i: (0, 0))],
        out_shape=[
            jax.ShapeDtypeStruct((n, d), F32),
            jax.ShapeDtypeStruct((n, d), F32),
            jax.ShapeDtypeStruct((SUBLANES, n), jnp.int32),
            jax.ShapeDtypeStruct((n, LANES), F32),
            jax.ShapeDtypeStruct((LANES, LANES), F32),
        ],
        scratch_shapes=[pltpu.VMEM((LANES, LANES), F32)],
        compiler_params=_cparams("arbitrary"),
        name="out_proj_sample",
    )(x, mix, w_out, lw["g_ffn"], lw["wr_hi"], lw["wr_lo"], lw["b_route"], seen)


SLOT_TILE = 256


def _slot_plan(counts, n_tokens):
    n_slots = 2 * n_tokens + N_EXPERTS * SLOT_TILE
    n_slots = -(-n_slots // SLOT_TILE) * SLOT_TILE
    n_tiles = n_slots // SLOT_TILE
    padded = (counts + (SLOT_TILE - 1)) // SLOT_TILE * SLOT_TILE
    ends = jnp.cumsum(padded)
    offs = ends - padded
    total = ends[-1]
    tile_start = jnp.arange(n_tiles, dtype=jnp.int32) * SLOT_TILE
    probe = jnp.minimum(tile_start, total - 1)
    tile_expert = jnp.sum((ends[None, :] <= probe[:, None]).astype(jnp.int32), axis=1)
    tile_expert = jnp.minimum(tile_expert, N_EXPERTS - 1)
    n_valid = (total // SLOT_TILE).reshape(1)
    last_tile = jnp.where(padded > 0, ends - SLOT_TILE, 0)
    return n_slots, offs, tile_expert, n_valid, last_tile, (padded > 0).astype(jnp.int32)


def _dispatch_kernel(last_ref, used_ref, nv_ref, pos1_ref, pos2_ref, xnp_ref, xns_ref, xs_ref,
                     src_ref, zero_ref, zsem, sem, *, n_prompt_tiles):
    tm = xnp_ref.shape[0]
    n_tiles = xs_ref.shape[0] // SLOT_TILE
    i = pl.program_id(0)

    @pl.when(i == 0)
    def _():
        zero_ref[...] = jnp.zeros_like(zero_ref)
        tile = lambda row: pltpu.make_async_copy(
            zero_ref, xs_ref.at[pl.ds(pl.multiple_of(row, SLOT_TILE), SLOT_TILE), :], zsem)
        for e in range(N_EXPERTS):
            @pl.when(used_ref[e] != 0)
            def _():
                tile(last_ref[e]).start()
        lax.fori_loop(nv_ref[0], n_tiles, lambda j, c: (tile(j * SLOT_TILE).start(), c)[1], 0)
        for e in range(N_EXPERTS):
            @pl.when(used_ref[e] != 0)
            def _():
                tile(0).wait()
        lax.fori_loop(nv_ref[0], n_tiles, lambda j, c: (tile(0).wait(), c)[1], 0)

    src_ref[...] = jnp.where(i < n_prompt_tiles, xnp_ref[...], xns_ref[...])
    for t in range(tm):
        _row_copy(src_ref, t, xs_ref, pos1_ref[t], sem).start(priority=0)
        _row_copy(src_ref, t, xs_ref, pos2_ref[t], sem).start(priority=1)
    for _ in range(2 * tm):
        _row_copy(src_ref, 0, xs_ref, 0, sem).wait()


def _dispatch(xn_p, xn_s, pos1, pos2, n_slots, last_tile, used, n_valid):
    n_p, d = xn_p.shape
    n_s = xn_s.shape[0]
    tm = ROUTE_TILE
    tp = n_p // tm
    ts_last = n_s // tm - 1
    smem_rows = lambda: pl.BlockSpec((tm,), lambda i, *_: (i,), memory_space=pltpu.SMEM)
    grid_spec = pltpu.PrefetchScalarGridSpec(
        num_scalar_prefetch=3,
        grid=((n_p + n_s) // tm,),
        in_specs=[smem_rows(), smem_rows(),
                  pl.BlockSpec((tm, d), lambda i, *_: (jnp.minimum(i, tp - 1), 0)),
                  pl.BlockSpec((tm, d), lambda i, *_: (jnp.clip(i - tp, 0, ts_last), 0))],
        out_specs=pl.BlockSpec(memory_space=pl.ANY),
        scratch_shapes=[pltpu.VMEM((tm, d), F32), pltpu.VMEM((SLOT_TILE, d), F32),
                        pltpu.SemaphoreType.DMA, pltpu.SemaphoreType.DMA],
    )
    return pl.pallas_call(
        functools.partial(_dispatch_kernel, n_prompt_tiles=tp),
        grid_spec=grid_spec,
        out_shape=jax.ShapeDtypeStruct((n_slots, d), F32),
        compiler_params=_cparams("arbitrary"),
        name="moe_dispatch",
    )(last_tile, used, n_valid, pos1, pos2, xn_p, xn_s)


def _experts_kernel(te_ref, nv_ref, xs_ref, w1_ref, w3_ref, w2_ref, ys_ref, w1b_ref, w3b_ref, w2b_ref):
    j = pl.program_id(0)
    fresh = jnp.logical_or(j == 0, te_ref[j] != te_ref[jnp.maximum(j - 1, 0)])

    @pl.when(fresh)
    def _():
        w1b_ref[...] = w1_ref[0, 0].astype(BF16)
        w3b_ref[...] = w3_ref[0, 0].astype(BF16)
        w2b_ref[...] = w2_ref[0, 0].astype(BF16)

    @pl.when(j < nv_ref[0])
    def _():
        x = xs_ref[...].astype(BF16)
        h1 = jnp.dot(x, w1b_ref[...], preferred_element_type=F32)
        h3 = jnp.dot(x, w3b_ref[...], preferred_element_type=F32)
        h = (_silu(h1) * h3).astype(BF16)
        ys_ref[...] = jnp.dot(h, w2b_ref[...], preferred_element_type=F32)

    @pl.when(j >= nv_ref[0])
    def _():
        ys_ref[...] = jnp.zeros_like(ys_ref)


def _experts(xs, tile_expert, n_valid, w1, w3, w2, layer):
    n_slots, d = xs.shape
    d_e = w1.shape[3]
    tile = lambda j, te, nv: (jnp.minimum(j, nv[0] - 1), 0)
    expert = lambda j, te, nv: (layer, te[j], 0, 0)
    grid_spec = pltpu.PrefetchScalarGridSpec(
        num_scalar_prefetch=2,
        grid=(n_slots // SLOT_TILE,),
        in_specs=[
            pl.BlockSpec((SLOT_TILE, d), tile),
            pl.BlockSpec((1, 1, d, d_e), expert),
            pl.BlockSpec((1, 1, d, d_e), expert),
            pl.BlockSpec((1, 1, d_e, d), expert),
        ],
        out_specs=pl.BlockSpec((SLOT_TILE, d), lambda j, te, nv: (j, 0)),
        scratch_shapes=[pltpu.VMEM((d, d_e), BF16), pltpu.VMEM((d, d_e), BF16), pltpu.VMEM((d_e, d), BF16)],
    )
    return pl.pallas_call(
        _experts_kernel,
        grid_spec=grid_spec,
        out_shape=jax.ShapeDtypeStruct((n_slots, d), F32),
        compiler_params=_cparams("arbitrary"),
        name="moe_experts",
    )(tile_expert, n_valid, xs, w1, w3, w2)


def _combine_kernel(pos1_ref, pos2_ref, xmid_ref, rg_ref, gfin_ref, ys_ref, o_ref, y1_ref, y2_ref, sem,
                    *, final_norm):
    tm = xmid_ref.shape[0]
    for t in range(tm):
        _row_copy(ys_ref, pos1_ref[t], y1_ref, t, sem).start(priority=0)
        _row_copy(ys_ref, pos2_ref[t], y2_ref, t, sem).start(priority=1)
    for _ in range(2 * tm):
        _row_copy(ys_ref, 0, y1_ref, 0, sem).wait()
    rg = rg_ref[...]
    out = xmid_ref[...] + (rg[:, 0:1] * y1_ref[...] + rg[:, 1:2] * y2_ref[...])
    o_ref[...] = _rms(out, gfin_ref[...]) if final_norm else out


def _combine(xmid, rg, pos1, pos2, row0, ys, g_final, final_norm):
    n, d = xmid.shape
    tm = ROUTE_TILE
    t0 = row0 // tm
    smem_rows = lambda: pl.BlockSpec((tm,), lambda i: (t0 + i,), memory_space=pltpu.SMEM)
    kern = functools.partial(_combine_kernel, final_norm=final_norm)
    return pl.pallas_call(
        kern,
        grid=(n // tm,),
        in_specs=[smem_rows(), smem_rows(),
                  pl.BlockSpec((tm, d), lambda i: (i, 0)),
                  pl.BlockSpec((tm, LANES), lambda i: (i, 0)),
                  _resident((1, d), (0, 0)),
                  pl.BlockSpec(memory_space=pl.ANY)],
        out_specs=pl.BlockSpec((tm, d), lambda i: (i, 0)),
        out_shape=jax.ShapeDtypeStruct((n, d), F32),
        scratch_shapes=[pltpu.VMEM((tm, d), F32), pltpu.VMEM((tm, d), F32), pltpu.SemaphoreType.DMA],
        compiler_params=_cparams("arbitrary"),
        name="moe_combine",
    )(pos1, pos2, xmid, rg, g_final, ys)


def _moe_experts(prompt, sample, counts, w1, w3, w2, layer):
    xn_p, rt_p = prompt
    xn_s, rt_s = sample
    n = xn_p.shape[0] + xn_s.shape[0]
    n_slots, offs, tile_expert, n_valid, last_tile, used = _slot_plan(
        counts[:N_EXPERTS, 0].astype(jnp.int32), n)
    rt = jnp.concatenate([rt_p, rt_s], axis=1)
    experts = jnp.arange(N_EXPERTS, dtype=jnp.int32)[:, None]
    base = lambda e: jnp.sum(jnp.where(experts == e[None, :], offs[:, None], 0), axis=0)
    pos1 = base(rt[0]) + rt[2]
    pos2 = base(rt[1]) + rt[3]
    xs = _dispatch(xn_p, xn_s, pos1, pos2, n_slots, last_tile, used, n_valid)
    return _experts(xs, tile_expert, n_valid, w1, w3, w2, layer), pos1, pos2


def kernel(x_prompt, x_sample, state_lru_h, state_lru_conv, state_conv, state_ret, norm_mix_g, norm_ffn_g, final_norm_g, w_in, lru_conv_w, lru_conv_b, lru_wa, lru_ba, lru_wx, lru_bx, lru_lambda, cm_dw_w, cm_dw_b, cm_ln_g, cm_ln_b, ret_norm_g, w_out, moe_w_group, moe_b_group, moe_w_expert, moe_b_expert, moe_w1, moe_w3, moe_w2):
    bp, tp, d = x_prompt.shape
    bs, ts, _ = x_sample.shape
    depth = w_in.shape[0]
    w_a = lru_lambda.shape[1]
    w_b = cm_dw_b.shape[1]
    w_c = ret_norm_g.shape[1]
    n_p, n_s = bp * tp, bs * ts
    row = lambda v: v.reshape(1, -1)

    x_p, x_s = x_prompt.reshape(n_p, d), x_sample.reshape(n_s, d)
    outs = {k: [] for k in ("p_h", "p_ca", "p_cb", "p_s", "s_h", "s_ca", "s_cb")}
    s_states = None
    combine_p = None
    for l in range(depth):
        wr_hi, wr_lo, b_route = _route_weights(moe_w_group[l], moe_b_group[l], moe_w_expert[l], moe_b_expert[l])
        lw = dict(
            w_a=w_a, w_b=w_b, w_c=w_c, g_mix=row(norm_mix_g[l]), g_ffn=row(norm_ffn_g[l]),
            conv_a_w=lru_conv_w[l], conv_a_b=row(lru_conv_b[l]),
            wg=_blockdiag_gate_weights(lru_wa[l], lru_wx[l]),
            bg=row(jnp.concatenate([lru_ba[l], lru_bx[l]])), lam=row(lru_lambda[l]),
            conv_b_w=cm_dw_w[l], conv_b_b=row(cm_dw_b[l]), ln_g=row(cm_ln_g[l]), ln_b=row(cm_ln_b[l]),
            ret_g=row(ret_norm_g[l]), wr_hi=wr_hi, wr_lo=wr_lo, b_route=b_route,
        )
        parts = 1
        xmid_p, xn_p, rt_p, rg_p, seen, h_p, ca_p, cb_p, s_p = _prompt_layer(
            x_p, bp, tp, l, w_in, w_out, lw, parts, combine_p)
        outs["p_h"].append(h_p[:, 0, :])
        outs["p_ca"].append(ca_p[:, SUBLANES - (CONV_A - 1):, :])
        outs["p_cb"].append(cb_p[:, 4 * SUBLANES - (CONV_B - 1):, :])
        outs["p_s"].append(_unpair_state(s_p))
        proj_s = _in_proj(x_s, lw["g_mix"], w_in, l, parts)
        p_t = proj_s[:, :2 * w_a + 2 * w_b].reshape(bs, ts, -1).transpose(1, 0, 2)
        oab_t, h_s, ca_t, cb_t = _mix_sample_ab(
            p_t, state_lru_h, state_lru_conv[l].transpose(1, 0, 2), state_conv[l].transpose(1, 0, 2), l, lw,
            parts)
        oab = oab_t.transpose(1, 0, 2).reshape(n_s, w_a + w_b)
        mix_s, s_states = _mix_sample_c(proj_s, bs, ts, oab, state_ret, l, s_states, lw, parts)
        outs["s_h"].append(h_s)
        outs["s_ca"].append(ca_t.transpose(1, 0, 2))
        outs["s_cb"].append(cb_t.transpose(1, 0, 2))
        xmid_s, xn_s, rt_s, rg_s, counts = _out_proj(x_s, mix_s, w_out, l, seen, lw, parts)
        ys, pos1, pos2 = _moe_experts((xn_p, rt_p), (xn_s, rt_s), counts, moe_w1, moe_w3, moe_w2, l)
        last = l == depth - 1
        x_s = _combine(xmid_s, rg_s, pos1, pos2, n_p, ys, row(final_norm_g), last)
        if last:
            x_p = _combine(xmid_p, rg_p, pos1, pos2, 0, ys, row(final_norm_g), last)
        else:
            x_p, combine_p = xmid_p, (rg_p, pos1, pos2, ys)
    st = lambda k: jnp.stack(outs[k])
    return (x_p.reshape(bp, tp, d), x_s.reshape(bs, ts, d), st("p_h"), st("p_ca"), st("p_cb"), st("p_s"),
            st("s_h"), st("s_ca"), st("s_cb"), s_states)
```

```python
import functools
import math

import jax
import jax.numpy as jnp
from jax import lax
from jax.experimental import pallas as pl
from jax.experimental.pallas import tpu as pltpu

PAST_LEN = 16384
LRU_C = 8.0
CONV_A = 4
CONV_B = 31
NH_C = 8
RET_CHUNK = 128
ROPE_BASE = 10000.0
N_GROUPS = 4
EXPERTS_PER_GROUP = 8
N_EXPERTS = N_GROUPS * EXPERTS_PER_GROUP
EPS = 1e-6

LANES = 128
SUBLANES = 8
VMEM_LIMIT_BYTES = 56 * 1024 * 1024

F32 = jnp.float32
BF16 = jnp.bfloat16


def _cparams(*sem):
    return pltpu.CompilerParams(dimension_semantics=sem, vmem_limit_bytes=VMEM_LIMIT_BYTES)


def _row_tile(n):
    for t in (512, 256, 128, 64, 32, 16, 8):
        if n % t == 0:
            return t
    raise ValueError(f"row count {n} must be a multiple of 8")


def _resident(shape, index):
    return pl.BlockSpec(shape, lambda *_: index, pipeline_mode=pl.Buffered(1))


def _rms(x, g):
    ms = jnp.mean(x * x, axis=-1, keepdims=True)
    return x * lax.rsqrt(ms + EPS) * g


def _sigmoid(x):
    return jax.nn.sigmoid(x)


def _silu(x):
    return x * jax.nn.sigmoid(x)


def _layernorm(x, g, b):
    mu = jnp.mean(x, axis=-1, keepdims=True)
    xc = x - mu
    var = jnp.mean(xc * xc, axis=-1, keepdims=True)
    return xc * lax.rsqrt(var + EPS) * g + b


def _rope_pair(x, cos, sin_signed, first_half):
    partner = jnp.where(first_half, pltpu.roll(x, 96, 1), pltpu.roll(x, 32, 1))
    return x * cos + partner * sin_signed


def _head_norm_pair(o, head0, inv_d):
    zero = jnp.zeros_like(o)
    s0 = jnp.sum(jnp.where(head0, o, zero), axis=-1, keepdims=True)
    s1 = jnp.sum(jnp.where(head0, zero, o), axis=-1, keepdims=True)
    oc = o - jnp.where(head0, s0, s1) * inv_d
    q = oc * oc
    v0 = jnp.sum(jnp.where(head0, q, zero), axis=-1, keepdims=True)
    v1 = jnp.sum(jnp.where(head0, zero, q), axis=-1, keepdims=True)
    return oc * lax.rsqrt(jnp.where(head0, v0, v1) * inv_d + EPS)


def _mm(a, b, dims, parts):
    dot = lambda x, y: lax.dot_general(x, y, (dims, ((), ())), preferred_element_type=F32)
    ah, bh = a.astype(BF16), b.astype(BF16)
    if parts == 1:
        return dot(ah, bh)
    al = (a - ah.astype(F32)).astype(BF16)
    bl = (b - bh.astype(F32)).astype(BF16)
    return dot(ah, bh) + (dot(al, bh) + dot(ah, bl))


ROWS_BY_COLS = ((1,), (0,))
ROWS_BY_ROWS = ((1,), (1,))
COLS_BY_COLS = ((0,), (0,))


def _retention_intra_pair(q, k, v, dmat0, dmat1, head0, parts):
    outs = []
    for j, dm in ((0, dmat0), (1, dmat1)):
        keep = head0 if j == 0 else jnp.logical_not(head0)
        s = _mm(jnp.where(keep, q, 0.0), k, ROWS_BY_ROWS, parts)
        outs.append(_mm(s * dm, v, ROWS_BY_COLS, parts))
    return jnp.where(head0, outs[0], outs[1])


def _lru_gates(xa, wg, bg, lam, parts):
    w_a = xa.shape[-1]
    gates = _mm(xa, wg, ROWS_BY_COLS, parts) + bg
    r = _sigmoid(gates[:, :w_a])
    i = _sigmoid(gates[:, w_a:])
    z = -lam
    softplus = jnp.maximum(z, 0.0) + jnp.log1p(jnp.exp(-jnp.abs(z)))
    log_a = -LRU_C * r * softplus
    a = jnp.exp(log_a)
    bt = jnp.sqrt(-jnp.tanh(log_a) * (a * a + 1.0)) * (i * xa)
    return a, bt


ROUTE_TILE = 256
EXPERT_ROW0 = SUBLANES
assert EXPERTS_PER_GROUP == SUBLANES and N_GROUPS <= SUBLANES and EXPERT_ROW0 + N_EXPERTS <= LANES


def _split_bf16(x):
    hi = x.astype(BF16)
    return hi, (x - hi.astype(F32)).astype(BF16)


def _store_parts(ref, idx, val):
    hi = val.astype(BF16)
    ref[(0,) + idx] = hi
    if ref.shape[0] == 2:
        ref[(1,) + idx] = (val - hi.astype(F32)).astype(BF16)


def _dot_parts(a_ref, a_idx, w_ref, w_idx):
    dot = lambda i, j: jnp.dot(a_ref[(i,) + a_idx], w_ref[(j,) + w_idx], preferred_element_type=F32)
    if a_ref.shape[0] == 1:
        return dot(0, 0)
    return dot(0, 0) + (dot(1, 0) + dot(0, 1))


def _lane_tile(x, width):
    return x if width == x.shape[1] else jnp.concatenate([x] * (width // x.shape[1]), axis=1)


def _bf16_parts(w):
    hi = lax.reduce_precision(w, exponent_bits=8, mantissa_bits=7)
    return hi.astype(BF16), (w - hi).astype(BF16)


def _weight_parts(w, parts):
    if parts == 1:
        return w.astype(BF16)[None]
    return jnp.stack(_bf16_parts(w))


def _route_weights(w_group, b_group, w_expert, b_expert):
    d = w_group.shape[0]
    w = jnp.zeros((LANES, d), F32).at[:N_GROUPS].set(w_group.T).at[EXPERT_ROW0:EXPERT_ROW0 + N_EXPERTS].set(w_expert.T)
    b = jnp.zeros((LANES,), F32).at[:N_GROUPS].set(b_group).at[EXPERT_ROW0:EXPERT_ROW0 + N_EXPERTS].set(b_expert)
    return w, jnp.broadcast_to(b[:, None], (LANES, LANES))


def _ffn_norm_route(xm, g_ffn, w_route, b_route, carry_ref, xn_ref, rt_ref, rg_ref, counts_ref):
    xn = _rms(xm, g_ffn)
    xn_ref[...] = xn
    tm = xn.shape[0]
    xh, xl = _split_bf16(xn)
    wr_hi, wr_lo = _split_bf16(w_route)
    dot_t = lambda w, x: lax.dot_general(w, x, (((1,), (1,)), ((), ())), preferred_element_type=F32)
    lt = dot_t(wr_hi, xh) + (dot_t(wr_hi, xl) + dot_t(wr_lo, xh)) + _lane_tile(b_route, tm)
    row8 = lax.broadcasted_iota(jnp.int32, (SUBLANES, tm), 0)
    neg = jnp.full((SUBLANES, tm), -jnp.inf, F32)
    big = jnp.full((SUBLANES, tm), LANES, jnp.int32)
    first = lambda hit: jnp.min(jnp.where(hit, row8, big), axis=0, keepdims=True)
    gl = jnp.where(row8 < N_GROUPS, lt[0:SUBLANES, :], neg)
    gmax = jnp.max(gl, axis=0, keepdims=True)
    p_top = 1.0 / jnp.sum(jnp.exp(gl - gmax), axis=0, keepdims=True)
    g_sel = first(gl == gmax)
    slab = lambda g: lt[EXPERT_ROW0 + g * SUBLANES:EXPERT_ROW0 + (g + 1) * SUBLANES, :]
    el = slab(N_GROUPS - 1)
    for g in reversed(range(N_GROUPS - 1)):
        el = jnp.where(g_sel == g, slab(g), el)
    v1 = jnp.max(el, axis=0, keepdims=True)
    i1 = first(el == v1)
    el2 = jnp.where(row8 == i1, neg, el)
    v2 = jnp.max(el2, axis=0, keepdims=True)
    i2 = first(el2 == v2)
    ex = jnp.exp(v2 - v1)
    den = 1.0 + ex
    g1 = p_top * (1.0 / den)
    g2 = p_top * (ex / den)
    e1 = g_sel * EXPERTS_PER_GROUP + i1
    e2 = g_sel * EXPERTS_PER_GROUP + i2
    rowe = lax.broadcasted_iota(jnp.int32, (LANES, tm), 0)
    oh1 = rowe == e1
    oh2 = rowe == e2
    onehot = jnp.logical_or(oh1, oh2).astype(BF16)
    earlier = (lax.broadcasted_iota(jnp.int32, (tm, tm), 0)
               < lax.broadcasted_iota(jnp.int32, (tm, tm), 1)).astype(BF16)
    seen = _lane_tile(carry_ref[...], tm) + jnp.dot(onehot, earlier, preferred_element_type=F32)
    rank1 = jnp.sum(jnp.where(oh1, seen, 0.0), axis=0, keepdims=True).astype(jnp.int32)
    rank2 = jnp.sum(jnp.where(oh2, seen, 0.0), axis=0, keepdims=True).astype(jnp.int32)
    total = carry_ref[...] + jnp.sum(onehot.astype(F32), axis=1, keepdims=True)
    carry_ref[...] = total
    counts_ref[...] = total
    zi = jnp.zeros((SUBLANES, tm), jnp.int32)
    rt_ref[...] = jnp.where(row8 == 0, e1, jnp.where(row8 == 1, e2, jnp.where(
        row8 == 2, rank1, jnp.where(row8 == 3, rank2, zi))))
    rg_ref[...] = jnp.where(rowe == 0, g1, jnp.where(rowe == 1, g2, 0.0)).T


CONV_B_ROWS = 64


def _row_copy(src_ref, src_row, dst_ref, dst_row, sem):
    return pltpu.make_async_copy(src_ref.at[pl.ds(src_row, 1), :], dst_ref.at[pl.ds(dst_row, 1), :], sem)


def _prompt_layer_kernel(*refs, tc, w_a, w_b, w_c, chunk, fused_combine):
    if fused_combine:
        (p1c_ref, p2c_ref, p1n_ref, p2n_ref, rgp_ref, ys_ref), refs = refs[:6], refs[6:]
        (y_ref, ysem), refs = refs[-2:], refs[:-2]
    (x_ref, gmix_ref, winb_ref, cos_ref, sin_ref, h0_ref, bufa0_ref, bufb0_ref, s0_ref,
     caw_ref, cab_ref, wg_ref, bg_ref, lam_ref,
     cbw_ref, cbb_ref, lng_ref, lnb_ref, rng_ref,
     dmat_ref, xi_ref, zeta_ref, gmat_ref,
     woutb_ref, gffn_ref, wr_ref, br_ref,
     xmid_ref, xn_ref, rt_ref, rg_ref, counts_ref,
     hout_ref, bufa_out_ref, bufb_out_ref, sout_ref,
     xin_ref, u_ref, proj_ref, mix_ref,
     xpa_ref, xpb_ref, sh_ref, a_ref, b_ref, hseq_ref, hcar_ref, s_ref, carry_ref) = refs
    t = pl.program_id(1)
    n_t = pl.num_programs(1)
    pa = SUBLANES
    pb = 4 * SUBLANES
    parts = winb_ref.shape[0]
    everything = (slice(None), slice(None))

    if fused_combine:
        step = pl.program_id(0) * n_t + t
        slot = step % 2

        def gather(p1_ref, p2_ref, into):
            for r in range(tc):
                _row_copy(ys_ref, p1_ref[r], y_ref.at[into, 0], r, ysem.at[into]).start(priority=0)
                _row_copy(ys_ref, p2_ref[r], y_ref.at[into, 1], r, ysem.at[into]).start(priority=1)

        @pl.when(step == 0)
        def _():
            gather(p1c_ref, p2c_ref, 0)

        for _ in range(2 * tc):
            _row_copy(ys_ref, 0, y_ref.at[slot, 0], 0, ysem.at[slot]).wait()
        gates = rgp_ref[...]
        xin_ref[...] = x_ref[...] + (gates[:, 0:1] * y_ref[slot, 0] + gates[:, 1:2] * y_ref[slot, 1])
    else:
        xin_ref[...] = x_ref[...]

    @pl.when(jnp.logical_and(pl.program_id(0) == 0, t == 0))
    def _():
        carry_ref[...] = jnp.zeros_like(carry_ref)

    @pl.when(t == 0)
    def _():
        xpa_ref[0:pa, :] = bufa0_ref[0]
        xpb_ref[0:pb, :] = bufb0_ref[0]
        hcar_ref[...] = h0_ref[0]
        s_ref[...] = s0_ref[0]

    _store_parts(u_ref, everything, _rms(xin_ref[...], gmix_ref[...]))

    def in_proj(c0, c1):
        for c in range(c0, c1, 4 * LANES):
            cols = (slice(None), slice(c, min(c + 4 * LANES, c1)))
            proj_ref[cols] = _dot_parts(u_ref, everything, winb_ref, cols)

    in_proj(0, 2 * w_a)

    xpa_ref[pa:pa + tc, :] = proj_ref[:, w_a:2 * w_a]
    xa = cab_ref[...] + caw_ref[0:1, :] * xpa_ref[pa - 3:pa - 3 + tc, :]
    for j in range(1, CONV_A):
        xa = xa + caw_ref[j:j + 1, :] * xpa_ref[pa - 3 + j:pa - 3 + j + tc, :]
    a, bt = _lru_gates(xa, wg_ref[...], bg_ref[...], lam_ref[...], parts)
    a_ref[...] = a
    b_ref[...] = bt
    row8 = lax.broadcasted_iota(jnp.int32, (SUBLANES, w_a), 0)

    def scan_block(i, hprev):
        r0 = pl.multiple_of(i * SUBLANES, SUBLANES)
        ab = a_ref[pl.ds(r0, SUBLANES), :]
        bb = b_ref[pl.ds(r0, SUBLANES), :]
        for k in (1, 2, 4):
            a_sh = jnp.where(row8 >= k, pltpu.roll(ab, k, 0), 1.0)
            b_sh = jnp.where(row8 >= k, pltpu.roll(bb, k, 0), 0.0)
            bb = ab * b_sh + bb
            ab = ab * a_sh
        h = ab * hprev + bb
        hseq_ref[pl.ds(r0, SUBLANES), :] = h
        return jnp.broadcast_to(h[SUBLANES - 1:SUBLANES, :], (SUBLANES, w_a))

    hcar_ref[...] = lax.fori_loop(0, tc // SUBLANES, scan_block, hcar_ref[...])
    if fused_combine:
        gather(p1n_ref, p2n_ref, 1 - slot)
    in_proj(2 * w_a, proj_ref.shape[1])
    _store_parts(mix_ref, (slice(None), slice(0, w_a)), jax.nn.gelu(proj_ref[:, 0:w_a]) * hseq_ref[...])
    xpa_ref[0:pa, :] = xpa_ref[tc:tc + pa, :]

    b0 = 2 * w_a
    xpb_ref[pb:pb + tc, :] = proj_ref[:, b0:b0 + w_b] * _sigmoid(proj_ref[:, b0 + w_b:b0 + 2 * w_b])
    first = pb - (CONV_B - 1)
    for r in range(SUBLANES):
        span = tc + (CONV_B - 1 - r) // SUBLANES * SUBLANES
        sh_ref[r, 0:span, :] = xpb_ref[first + r:first + r + span, :]
    for blk in range(tc // CONV_B_ROWS):
        r0 = blk * CONV_B_ROWS
        acc = jnp.broadcast_to(cbb_ref[...], (CONV_B_ROWS, w_b))
        for j in range(CONV_B):
            r = j % SUBLANES
            acc = acc + cbw_ref[j:j + 1, :] * sh_ref[r, r0 + j - r:r0 + j - r + CONV_B_ROWS, :]
        out_b = _silu(_layernorm(acc, lng_ref[...], lnb_ref[...]))
        _store_parts(mix_ref, (slice(r0, r0 + CONV_B_ROWS), slice(w_a, w_a + w_b)), out_b)
    xpb_ref[0:pb, :] = xpb_ref[tc:tc + pb, :]

    c0 = 2 * w_a + 2 * w_b
    n_pair = w_c // LANES
    lane = lax.broadcasted_iota(jnp.int32, (chunk, LANES), 1)
    head0 = lane < (LANES // 2)
    first_half = (lane % (LANES // 2)) < (LANES // 4)
    rl = lax.broadcasted_iota(jnp.int32, (LANES, LANES), 0) < (LANES // 2)
    cl = lax.broadcasted_iota(jnp.int32, (LANES, LANES), 1) < (LANES // 2)
    same_head = rl == cl
    inv_d = 2.0 / LANES
    for c in range(tc // chunk):
        rows = slice(c * chunk, (c + 1) * chunk)
        cos = cos_ref[rows, :]
        sin = sin_ref[rows, :]
        for p in range(n_pair):
            cq = c0 + p * LANES
            q = _rope_pair(proj_ref[rows, cq:cq + LANES], cos, sin, first_half)
            k = _rope_pair(proj_ref[rows, cq + w_c:cq + w_c + LANES], cos, sin, first_half) * (2.0 / LANES) ** 0.5
            v = proj_ref[rows, cq + 2 * w_c:cq + 2 * w_c + LANES]
            g = proj_ref[rows, cq + 3 * w_c:cq + 3 * w_c + LANES]
            o = _retention_intra_pair(q, k, v, dmat_ref[2 * p], dmat_ref[2 * p + 1], head0, parts)
            s_old = s_ref[p]
            o = o + _mm(q, s_old, ROWS_BY_COLS, parts) * xi_ref[p]
            upd = _mm(k * zeta_ref[p], v, COLS_BY_COLS, parts)
            s_ref[p] = gmat_ref[p] * s_old + jnp.where(same_head, upd, 0.0)
            on = _head_norm_pair(o, head0, inv_d) * rng_ref[:, p * LANES:(p + 1) * LANES]
            _store_parts(mix_ref, (rows, slice(w_a + w_b + p * LANES, w_a + w_b + (p + 1) * LANES)), _silu(g) * on)

    xm = xin_ref[...] + _dot_parts(mix_ref, everything, woutb_ref, everything)
    xmid_ref[...] = xm
    _ffn_norm_route(xm, gffn_ref[...], wr_ref[...], br_ref[...], carry_ref,
                    xn_ref, rt_ref, rg_ref, counts_ref)

    @pl.when(t == n_t - 1)
    def _():
        hout_ref[0] = hcar_ref[...]
        bufa_out_ref[0] = xpa_ref[0:pa, :]
        bufb_out_ref[0] = xpb_ref[0:pb, :]
        sout_ref[0] = s_ref[...]

    if fused_combine:
        @pl.when(step == pl.num_programs(0) * n_t - 1)
        def _():
            for _ in range(2 * tc):
                _row_copy(ys_ref, 0, y_ref.at[1 - slot, 0], 0, ysem.at[1 - slot]).wait()


def _retention_tables(chunk):
    log_g = jnp.log1p(-jnp.exp2(-5.0 - jnp.arange(NH_C, dtype=F32)))
    idx = jnp.arange(chunk, dtype=F32)
    diff = idx[:, None] - idx[None, :]
    causal = diff >= 0
    dmat = jnp.where(causal[None], jnp.exp(jnp.where(causal, diff, 0.0)[None] * log_g[:, None, None]), 0.0)
    xi = jnp.exp((idx + 1.0)[None, :] * log_g[:, None])
    zeta = jnp.exp((chunk - 1.0 - idx)[None, :] * log_g[:, None])
    g_chunk = jnp.exp(chunk * log_g)
    return dmat, xi, zeta, g_chunk


def _pair_lanes(per_head):
    h, rows = per_head.shape
    half = LANES // 2
    x = jnp.broadcast_to(per_head.reshape(h // 2, 2, rows, 1), (h // 2, 2, rows, half))
    return x.transpose(0, 2, 1, 3).reshape(h // 2, rows, LANES)


def _rope_tables(pos):
    half = LANES // 4
    inv = ROPE_BASE ** (-jnp.arange(half, dtype=F32) / half)
    ang = pos.astype(F32)[:, None] * inv[None, :]
    cos, sin = jnp.cos(ang), jnp.sin(ang)
    return jnp.concatenate([cos] * 4, axis=1), jnp.concatenate([-sin, sin, -sin, sin], axis=1)


def _blockdiag_gate_weights(wa, wx):
    nh, hd, _ = wa.shape
    eye = jnp.eye(nh, dtype=wa.dtype)
    da = (eye[:, None, :, None] * wa[:, :, None, :]).reshape(nh * hd, nh * hd)
    dx = (eye[:, None, :, None] * wx[:, :, None, :]).reshape(nh * hd, nh * hd)
    return jnp.concatenate([da, dx], axis=1)


def _prompt_layer(x, n_seq, seq_len, layer, w_in, w_out, lw, parts, combine=None):
    w_a, w_b, w_c = lw["w_a"], lw["w_b"], lw["w_c"]
    n, d = x.shape
    d_in = w_in.shape[2]
    d_mix = w_a + w_b + w_c
    chunk = math.gcd(seq_len, RET_CHUNK)
    tc = ROUTE_TILE
    assert seq_len % tc == 0 and tc % chunk == 0 and tc % CONV_B_ROWS == 0
    n_t = seq_len // tc
    n_pair = w_c // LANES
    dmat, xi, zeta, g_chunk = _retention_tables(chunk)
    half = LANES // 2
    gm = jnp.zeros((n_pair, 2, half, 2, half), F32)
    gm = gm.at[:, 0, :, 0, :].set(g_chunk[0::2, None, None]).at[:, 1, :, 1, :].set(g_chunk[1::2, None, None])
    cos, sin = _rope_tables(jnp.arange(seq_len, dtype=jnp.int32))
    zeros = lambda *s: jnp.zeros(s, F32)
    full = lambda shape: _resident(shape, (0,) * len(shape))
    per_seq = lambda shape: pl.BlockSpec((1,) + shape, lambda b, t: (b,) + (0,) * len(shape))
    tile = lambda width: pl.BlockSpec((tc, width), lambda b, t: (b * n_t + t, 0))
    kern = functools.partial(_prompt_layer_kernel, tc=tc, w_a=w_a, w_b=w_b, w_c=w_c, chunk=chunk,
                             fused_combine=combine is not None)
    extra_in, extra_scratch, extra_args = [], [], []
    if combine is not None:
        rg_prev, pos1, pos2, ys = combine
        last = n // tc - 1
        cur = lambda: pl.BlockSpec((tc,), lambda b, t: (b * n_t + t,), memory_space=pltpu.SMEM)
        nxt = lambda: pl.BlockSpec((tc,), lambda b, t: (jnp.minimum(b * n_t + t + 1, last),),
                                   memory_space=pltpu.SMEM)
        extra_in = [cur(), cur(), nxt(), nxt(), tile(LANES), pl.BlockSpec(memory_space=pl.ANY)]
        extra_args = [pos1, pos2, pos1, pos2, rg_prev, ys]
        extra_scratch = [pltpu.VMEM((2, 2, tc, d), F32), pltpu.SemaphoreType.DMA((2,))]
    return pl.pallas_call(
        kern,
        grid=(n_seq, n_t),
        in_specs=extra_in + [
            tile(d), full((1, d)), full((parts, d, d_in)),
            pl.BlockSpec((tc, LANES), lambda b, t: (t, 0)),
            pl.BlockSpec((tc, LANES), lambda b, t: (t, 0)),
            per_seq((SUBLANES, w_a)), per_seq((SUBLANES, w_a)), per_seq((4 * SUBLANES, w_b)),
            per_seq((n_pair, LANES, LANES)),
            full((CONV_A, w_a)), full((1, w_a)), full((w_a, 2 * w_a)), full((1, 2 * w_a)), full((1, w_a)),
            full((CONV_B, w_b)), full((1, w_b)), full((1, w_b)), full((1, w_b)), full((1, w_c)),
            full((NH_C, chunk, chunk)), full((n_pair, chunk, LANES)), full((n_pair, chunk, LANES)),
            full((n_pair, LANES, LANES)),
            full((parts, d_mix, d)), full((1, d)),
            full((LANES, d)), full((LANES, LANES)),
        ],
        out_specs=[
            tile(d), tile(d),
            pl.BlockSpec((SUBLANES, tc), lambda b, t: (0, b * n_t + t)),
            tile(LANES),
            pl.BlockSpec((LANES, LANES), lambda b, t: (0, 0)),
            per_seq((SUBLANES, w_a)), per_seq((SUBLANES, w_a)), per_seq((4 * SUBLANES, w_b)),
            per_seq((n_pair, LANES, LANES)),
        ],
        out_shape=[
            jax.ShapeDtypeStruct((n, d), F32),
            jax.ShapeDtypeStruct((n, d), F32),
            jax.ShapeDtypeStruct((SUBLANES, n), jnp.int32),
            jax.ShapeDtypeStruct((n, LANES), F32),
            jax.ShapeDtypeStruct((LANES, LANES), F32),
            jax.ShapeDtypeStruct((n_seq, SUBLANES, w_a), F32),
            jax.ShapeDtypeStruct((n_seq, SUBLANES, w_a), F32),
            jax.ShapeDtypeStruct((n_seq, 4 * SUBLANES, w_b), F32),
            jax.ShapeDtypeStruct((n_seq, n_pair, LANES, LANES), F32),
        ],
        scratch_shapes=[
            pltpu.VMEM((tc, d), F32),
            pltpu.VMEM((parts, tc, d), BF16),
            pltpu.VMEM((tc, d_in), F32),
            pltpu.VMEM((parts, tc, d_mix), BF16),
            pltpu.VMEM((SUBLANES + tc, w_a), F32),
            pltpu.VMEM((4 * SUBLANES + tc, w_b), F32),
            pltpu.VMEM((SUBLANES, 3 * SUBLANES + tc, w_b), F32),
            pltpu.VMEM((tc, w_a), F32),
            pltpu.VMEM((tc, w_a), F32),
            pltpu.VMEM((tc, w_a), F32),
            pltpu.VMEM((SUBLANES, w_a), F32),
            pltpu.VMEM((n_pair, LANES, LANES), F32),
            pltpu.VMEM((LANES, LANES), F32),
        ] + extra_scratch,
        compiler_params=_cparams("arbitrary", "arbitrary"),
        name="prompt_layer",
    )(*extra_args, x, lw["g_mix"], _weight_parts(w_in[layer], parts), cos, sin,
      zeros(n_seq, SUBLANES, w_a), zeros(n_seq, SUBLANES, w_a), zeros(n_seq, 4 * SUBLANES, w_b),
      zeros(n_seq, n_pair, LANES, LANES),
      lw["conv_a_w"], lw["conv_a_b"], lw["wg"], lw["bg"], lw["lam"],
      lw["conv_b_w"], lw["conv_b_b"], lw["ln_g"], lw["ln_b"], lw["ret_g"],
      dmat, _pair_lanes(xi), _pair_lanes(zeta), gm.reshape(n_pair, LANES, LANES),
      _weight_parts(w_out[layer], parts), lw["g_ffn"], lw["w_route"], lw["b_route"])


def _unpair_state(s_pairs):
    b, n_pair = s_pairs.shape[:2]
    half = LANES // 2
    s = s_pairs.reshape(b, n_pair, 2, half, 2, half)
    return jnp.stack([s[:, :, 0, :, 0, :], s[:, :, 1, :, 1, :]], axis=2).reshape(b, 2 * n_pair, half, half)


def _in_proj_kernel(x_ref, g_ref, w_ref, o_ref, *, parts):
    o_ref[...] = _mm(_rms(x_ref[...], g_ref[...]), w_ref[0], ROWS_BY_COLS, parts)


def _in_proj(x, g, w_in, layer, parts):
    n, d = x.shape
    d_in = w_in.shape[2]
    tm = _row_tile(n)
    return pl.pallas_call(
        functools.partial(_in_proj_kernel, parts=parts),
        grid=(n // tm,),
        in_specs=[
            pl.BlockSpec((tm, d), lambda i: (i, 0)),
            _resident((1, d), (0, 0)),
            _resident((1, d, d_in), (layer, 0, 0)),
        ],
        out_specs=pl.BlockSpec((tm, d_in), lambda i: (i, 0)),
        out_shape=jax.ShapeDtypeStruct((n, d_in), F32),
        compiler_params=_cparams("arbitrary"),
        name="in_proj_sample",
    )(x, g, w_in)


def _mix_sample_ab_kernel(p_ref, h0_ref, bufa_ref, bufb_ref,
                          caw_ref, cab_ref, wg_ref, bg_ref, lam_ref,
                          cbw_ref, cbb_ref, lng_ref, lnb_ref,
                          o_ref, h_ref, bufa_out_ref, bufb_out_ref, *, ts, w_a, w_b, parts):
    xpa = [bufa_ref[j] for j in range(CONV_A - 1)] + [p_ref[t, :, w_a:2 * w_a] for t in range(ts)]
    xa = []
    for t in range(ts):
        acc = cab_ref[...] + caw_ref[0:1, :] * xpa[t]
        for j in range(1, CONV_A):
            acc = acc + caw_ref[j:j + 1, :] * xpa[t + j]
        xa.append(acc)
    nb = xa[0].shape[0]
    a, bt = _lru_gates(jnp.concatenate(xa, axis=0), wg_ref[...], bg_ref[...], lam_ref[...], parts)
    h = h0_ref[...]
    for t in range(ts):
        h = a[t * nb:(t + 1) * nb, :] * h + bt[t * nb:(t + 1) * nb, :]
        o_ref[t, :, 0:w_a] = jax.nn.gelu(p_ref[t, :, 0:w_a]) * h
    h_ref[...] = h
    for j in range(CONV_A - 1):
        bufa_out_ref[j] = xpa[ts + j]
    b0 = 2 * w_a
    hist = CONV_B - 1
    glu = [p_ref[t, :, b0:b0 + w_b] * _sigmoid(p_ref[t, :, b0 + w_b:b0 + 2 * w_b]) for t in range(ts)]
    xpb = lambda i: bufb_ref[i] if i < hist else glu[i - hist]
    for t in range(ts):
        acc = cbb_ref[...] + cbw_ref[0:1, :] * xpb(t)
        for j in range(1, CONV_B):
            acc = acc + cbw_ref[j:j + 1, :] * xpb(t + j)
        o_ref[t, :, w_a:w_a + w_b] = _silu(_layernorm(acc, lng_ref[...], lnb_ref[...]))
    for i in range(hist):
        bufb_out_ref[i] = xpb(ts + i)


def _mix_sample_ab(p_t, h0, bufa_t, bufb_t, layer, lw, parts):
    ts, n_seq, width = p_t.shape
    w_a, w_b = lw["w_a"], lw["w_b"]
    nb = 32 if n_seq % 32 == 0 else n_seq
    full = lambda shape: _resident(shape, (0,) * len(shape))
    kern = functools.partial(_mix_sample_ab_kernel, ts=ts, w_a=w_a, w_b=w_b, parts=parts)
    return pl.pallas_call(
        kern,
        grid=(n_seq // nb,),
        in_specs=[
            pl.BlockSpec((ts, nb, width), lambda i: (0, i, 0)),
            pl.BlockSpec((None, nb, w_a), lambda i: (layer, i, 0)),
            pl.BlockSpec((CONV_A - 1, nb, w_a), lambda i: (0, i, 0)),
            pl.BlockSpec((CONV_B - 1, nb, w_b), lambda i: (0, i, 0)),
            full((CONV_A, w_a)), full((1, w_a)), full((w_a, 2 * w_a)), full((1, 2 * w_a)), full((1, w_a)),
            full((CONV_B, w_b)), full((1, w_b)), full((1, w_b)), full((1, w_b)),
        ],
        out_specs=[
            pl.BlockSpec((ts, nb, w_a + w_b), lambda i: (0, i, 0)),
            pl.BlockSpec((nb, w_a), lambda i: (i, 0)),
            pl.BlockSpec((CONV_A - 1, nb, w_a), lambda i: (0, i, 0)),
            pl.BlockSpec((CONV_B - 1, nb, w_b), lambda i: (0, i, 0)),
        ],
        out_shape=[
            jax.ShapeDtypeStruct((ts, n_seq, w_a + w_b), F32),
            jax.ShapeDtypeStruct((n_seq, w_a), F32),
            jax.ShapeDtypeStruct((CONV_A - 1, n_seq, w_a), F32),
            jax.ShapeDtypeStruct((CONV_B - 1, n_seq, w_b), F32),
        ],
        compiler_params=_cparams("arbitrary"),
        name="mix_sample_ab",
    )(p_t, h0, bufa_t, bufb_t,
      lw["conv_a_w"], lw["conv_a_b"], lw["wg"], lw["bg"], lw["lam"],
      lw["conv_b_w"], lw["conv_b_b"], lw["ln_g"], lw["ln_b"])


def _mix_sample_c_kernel(q_ref, k_ref, v_ref, g_ref, oab_ref, s_ref, cos_ref, sin_ref,
                         dmat_ref, xi_ref, zeta_ref, gcol_ref, rng_ref, tile_ref, fold_ref, *rest,
                         rows, ts, w_c, w_ab, layer, parts):
    if layer:
        prev_ref, mix_ref, sout_ref, qs_ref, kzs_ref, vs_ref, oi_ref = rest
        for l in range(layer):
            sout_ref[l] = prev_ref[l]
    else:
        mix_ref, sout_ref, qs_ref, kzs_ref, vs_ref, oi_ref = rest
    n_pair = w_c // LANES
    hd = LANES // 2
    lane = lax.broadcasted_iota(jnp.int32, (rows, LANES), 1)
    head0 = lane < hd
    first_half = (lane % hd) < (hd // 2)
    cos = cos_ref[...]
    sin = sin_ref[...]
    for p in range(n_pair):
        cs = slice(p * LANES, (p + 1) * LANES)
        q = _rope_pair(q_ref[:, cs], cos, sin, first_half)
        k = _rope_pair(k_ref[:, cs], cos, sin, first_half) * (1.0 / hd) ** 0.5
        v = v_ref[:, cs]
        oi_ref[:, cs] = _retention_intra_pair(q, k, v, dmat_ref[2 * p], dmat_ref[2 * p + 1], head0, parts)
        qs_ref[:, cs] = q
        kzs_ref[:, cs] = k * zeta_ref[:, cs]
        vs_ref[:, cs] = v
    grp = SUBLANES
    per_grp = grp // ts
    stack = NH_C * grp
    srow = lax.broadcasted_iota(jnp.int32, (stack, w_c), 0)
    slane = lax.broadcasted_iota(jnp.int32, (stack, w_c), 1)
    own_head = (srow // grp) == (slane // hd)
    tile_m = tile_ref[...]
    fold_m = fold_ref[...]
    for gi in range(rows // grp):
        rs = slice(gi * grp, (gi + 1) * grp)
        q8 = jnp.concatenate([qs_ref[rs, :]] * NH_C, axis=0)
        kz8 = jnp.concatenate([kzs_ref[rs, :]] * NH_C, axis=0)
        v8 = jnp.concatenate([vs_ref[rs, :]] * NH_C, axis=0)
        rest = jnp.where(own_head, v8, 0.0)
        vr = jnp.zeros((stack, hd), F32)
        for _ in range(3 if parts == 2 else 1):
            hi = rest.astype(BF16)
            vr = vr + jnp.dot(hi, fold_m, preferred_element_type=F32)
            rest = rest - hi.astype(F32)
        o_stack = jnp.zeros((stack, hd), F32)
        for s2 in range(per_grp):
            sq = gi * per_grp + s2
            own_seq = ((srow % grp) // ts) == s2
            sel = jnp.logical_and(own_head, own_seq)
            s_old = s_ref[0, sq].reshape(w_c, hd)
            o_stack = o_stack + _mm(jnp.where(sel, q8, 0.0), s_old, ROWS_BY_COLS, parts)
            upd = _mm(jnp.where(sel, kz8, 0.0), vr, COLS_BY_COLS, parts)
            sout_ref[layer, sq] = (gcol_ref[...] * s_old + upd).reshape(NH_C, hd, hd)
        p1 = o_stack.astype(BF16)
        r1 = o_stack - p1.astype(F32)
        p2 = r1.astype(BF16)
        p3 = (r1 - p2.astype(F32)).astype(BF16)
        spread = (jnp.dot(p1, tile_m, preferred_element_type=F32)
                  + jnp.dot(p2, tile_m, preferred_element_type=F32)
                  + jnp.dot(p3, tile_m, preferred_element_type=F32))
        spread = jnp.where(own_head, spread, 0.0).reshape(NH_C, grp, w_c)
        oi_ref[rs, :] = oi_ref[rs, :] + jnp.sum(spread, axis=0) * xi_ref[rs, :]
    mix_ref[:, 0:w_ab] = oab_ref[...]
    for p in range(n_pair):
        cs = slice(p * LANES, (p + 1) * LANES)
        on = _head_norm_pair(oi_ref[:, cs], head0, 1.0 / hd) * rng_ref[:, cs]
        mix_ref[:, w_ab + p * LANES:w_ab + (p + 1) * LANES] = _silu(g_ref[:, cs]) * on


def _mix_sample_c(proj, n_seq, ts, oab, state, layer, prev_states, lw, parts):
    w_a, w_b, w_c = lw["w_a"], lw["w_b"], lw["w_c"]
    hd = LANES // 2
    n_rows = n_seq * ts
    rows = 64 if n_rows % 64 == 0 else n_rows
    assert SUBLANES % ts == 0 and rows % SUBLANES == 0
    seq_per = rows // ts
    c0 = 2 * w_a + 2 * w_b
    assert c0 % w_c == 0
    cb = c0 // w_c
    log_g = jnp.log1p(-jnp.exp2(-5.0 - jnp.arange(NH_C, dtype=F32)))
    r = jnp.arange(rows)
    step = (r % ts).astype(F32)
    diff = step[:, None] - step[None, :]
    live = jnp.logical_and((r[:, None] // ts) == (r[None, :] // ts), diff >= 0)
    dmat = jnp.where(live[None], jnp.exp(jnp.where(live, diff, 0.0)[None] * log_g[:, None, None]), 0.0)
    xi = jnp.repeat(jnp.exp((step + 1.0)[None, :] * log_g[:, None]).T, hd, axis=1)
    zeta = jnp.repeat(jnp.exp((ts - 1.0 - step)[None, :] * log_g[:, None]).T, hd, axis=1)
    gcol = jnp.broadcast_to(jnp.repeat(jnp.exp(ts * log_g), hd)[:, None], (w_c, hd))
    pos = PAST_LEN + (r % ts).astype(jnp.int32)
    cos, sin = _rope_tables(pos)
    tile_m = jnp.tile(jnp.eye(hd, dtype=BF16), (1, NH_C))
    fold_m = jnp.tile(jnp.eye(hd, dtype=BF16), (NH_C, 1))
    full = lambda shape: _resident(shape, (0,) * len(shape))
    col = lambda j: pl.BlockSpec((rows, w_c), lambda i: (i, cb + j))
    states = lambda n_layers, first: pl.BlockSpec((n_layers, seq_per, NH_C, hd, hd), lambda i: (first, i, 0, 0, 0))
    kern = functools.partial(_mix_sample_c_kernel, rows=rows, ts=ts, w_c=w_c, w_ab=w_a + w_b, layer=layer,
                             parts=parts)
    in_specs = [
        col(0), col(1), col(2), col(3),
        pl.BlockSpec((rows, w_a + w_b), lambda i: (i, 0)),
        states(1, layer),
        full((rows, LANES)), full((rows, LANES)),
        full((NH_C, rows, rows)), full((rows, w_c)), full((rows, w_c)), full((w_c, hd)), full((1, w_c)),
        full((hd, w_c)), full((w_c, hd)),
    ]
    args = [proj, proj, proj, proj, oab, state, cos, sin, dmat, xi, zeta, gcol, lw["ret_g"], tile_m, fold_m]
    if layer:
        in_specs.append(states(layer, 0))
        args.append(prev_states)
    return pl.pallas_call(
        kern,
        grid=(n_rows // rows,),
        in_specs=in_specs,
        out_specs=[
            pl.BlockSpec((rows, w_a + w_b + w_c), lambda i: (i, 0)),
            states(layer + 1, 0),
        ],
        out_shape=[
            jax.ShapeDtypeStruct((n_rows, w_a + w_b + w_c), F32),
            jax.ShapeDtypeStruct((layer + 1,) + state.shape[1:], F32),
        ],
        scratch_shapes=[pltpu.VMEM((rows, w_c), F32)] * 4,
        compiler_params=_cparams("arbitrary"),
        name="mix_sample_c",
    )(*args)


def _out_proj_kernel(x_ref, mix_ref, w_ref, g_ref, wr_ref, br_ref, seen_ref,
                     xmid_ref, xn_ref, rt_ref, rg_ref, counts_ref, carry_ref, *, parts):
    @pl.when(pl.program_id(0) == 0)
    def _():
        carry_ref[...] = seen_ref[...]

    xm = x_ref[...] + _mm(mix_ref[...], w_ref[0], ROWS_BY_COLS, parts)
    xmid_ref[...] = xm
    _ffn_norm_route(xm, g_ref[...], wr_ref[...], br_ref[...], carry_ref,
                    xn_ref, rt_ref, rg_ref, counts_ref)


def _out_proj(x, mix, w_out, layer, seen, lw, parts):
    n, d = x.shape
    d_mix = mix.shape[1]
    tm = ROUTE_TILE
    assert n % tm == 0
    full = lambda shape: _resident(shape, (0,) * len(shape))
    rows = lambda width: pl.BlockSpec((tm, width), lambda i: (i, 0))
    return pl.pallas_call(
        functools.partial(_out_proj_kernel, parts=parts),
        grid=(n // tm,),
        in_specs=[rows(d), rows(d_mix), _resident((1, d_mix, d), (layer, 0, 0)),
                  full((1, d)), full((LANES, d)), full((LANES, LANES)), full((LANES, LANES))],
        out_specs=[rows(d), rows(d), pl.BlockSpec((SUBLANES, tm), lambda i: (0, i)), rows(LANES),
                   pl.BlockSpec((LANES, LANES), lambda i: (0, 0))],
        out_shape=[
            jax.ShapeDtypeStruct((n, d), F32),
            jax.ShapeDtypeStruct((n, d), F32),
            jax.ShapeDtypeStruct((SUBLANES, n), jnp.int32),
            jax.ShapeDtypeStruct((n, LANES), F32),
            jax.ShapeDtypeStruct((LANES, LANES), F32),
        ],
        scratch_shapes=[pltpu.VMEM((LANES, LANES), F32)],
        compiler_params=_cparams("arbitrary"),
        name="out_proj_sample",
    )(x, mix, w_out, lw["g_ffn"], lw["w_route"], lw["b_route"], seen)


SLOT_TILE = 256


def _slot_plan(counts, n_tokens):
    n_slots = 2 * n_tokens + N_EXPERTS * SLOT_TILE
    n_slots = -(-n_slots // SLOT_TILE) * SLOT_TILE
    n_tiles = n_slots // SLOT_TILE
    padded = (counts + (SLOT_TILE - 1)) // SLOT_TILE * SLOT_TILE
    ends = jnp.cumsum(padded)
    offs = ends - padded
    total = ends[-1]
    tile_start = jnp.arange(n_tiles, dtype=jnp.int32) * SLOT_TILE
    probe = jnp.minimum(tile_start, total - 1)
    tile_expert = jnp.sum((ends[None, :] <= probe[:, None]).astype(jnp.int32), axis=1)
    tile_expert = jnp.minimum(tile_expert, N_EXPERTS - 1)
    n_valid = (total // SLOT_TILE).reshape(1)
    last_tile = jnp.where(padded > 0, ends - SLOT_TILE, 0)
    return n_slots, offs, tile_expert, n_valid, last_tile, (padded > 0).astype(jnp.int32)


def _dispatch_kernel(last_ref, used_ref, nv_ref, pos1_ref, pos2_ref, xnp_ref, xns_ref, xs_ref,
                     src_ref, zero_ref, zsem, sem, *, n_prompt_tiles):
    tm = xnp_ref.shape[0]
    n_tiles = xs_ref.shape[0] // SLOT_TILE
    i = pl.program_id(0)

    @pl.when(i == 0)
    def _():
        zero_ref[...] = jnp.zeros_like(zero_ref)
        tile = lambda row: pltpu.make_async_copy(
            zero_ref, xs_ref.at[pl.ds(pl.multiple_of(row, SLOT_TILE), SLOT_TILE), :], zsem)
        for e in range(N_EXPERTS):
            @pl.when(used_ref[e] != 0)
            def _():
                tile(last_ref[e]).start()
        lax.fori_loop(nv_ref[0], n_tiles, lambda j, c: (tile(j * SLOT_TILE).start(), c)[1], 0)
        for e in range(N_EXPERTS):
            @pl.when(used_ref[e] != 0)
            def _():
                tile(0).wait()
        lax.fori_loop(nv_ref[0], n_tiles, lambda j, c: (tile(0).wait(), c)[1], 0)

    src_ref[...] = jnp.where(i < n_prompt_tiles, xnp_ref[...], xns_ref[...])
    for t in range(tm):
        _row_copy(src_ref, t, xs_ref, pos1_ref[t], sem).start(priority=0)
        _row_copy(src_ref, t, xs_ref, pos2_ref[t], sem).start(priority=1)
    for _ in range(2 * tm):
        _row_copy(src_ref, 0, xs_ref, 0, sem).wait()


def _dispatch(xn_p, xn_s, pos1, pos2, n_slots, last_tile, used, n_valid):
    n_p, d = xn_p.shape
    n_s = xn_s.shape[0]
    tm = ROUTE_TILE
    tp = n_p // tm
    ts_last = n_s // tm - 1
    smem_rows = lambda: pl.BlockSpec((tm,), lambda i, *_: (i,), memory_space=pltpu.SMEM)
    grid_spec = pltpu.PrefetchScalarGridSpec(
        num_scalar_prefetch=3,
        grid=((n_p + n_s) // tm,),
        in_specs=[smem_rows(), smem_rows(),
                  pl.BlockSpec((tm, d), lambda i, *_: (jnp.minimum(i, tp - 1), 0)),
                  pl.BlockSpec((tm, d), lambda i, *_: (jnp.clip(i - tp, 0, ts_last), 0))],
        out_specs=pl.BlockSpec(memory_space=pl.ANY),
        scratch_shapes=[pltpu.VMEM((tm, d), F32), pltpu.VMEM((SLOT_TILE, d), F32),
                        pltpu.SemaphoreType.DMA, pltpu.SemaphoreType.DMA],
    )
    return pl.pallas_call(
        functools.partial(_dispatch_kernel, n_prompt_tiles=tp),
        grid_spec=grid_spec,
        out_shape=jax.ShapeDtypeStruct((n_slots, d), F32),
        compiler_params=_cparams("arbitrary"),
        name="moe_dispatch",
    )(last_tile, used, n_valid, pos1, pos2, xn_p, xn_s)


def _experts_kernel(te_ref, nv_ref, xs_ref, w1_ref, w3_ref, w2_ref, ys_ref, w1b_ref, w3b_ref, w2b_ref):
    j = pl.program_id(0)
    fresh = jnp.logical_or(j == 0, te_ref[j] != te_ref[jnp.maximum(j - 1, 0)])

    @pl.when(fresh)
    def _():
        w1b_ref[...] = w1_ref[0, 0].astype(BF16)
        w3b_ref[...] = w3_ref[0, 0].astype(BF16)
        w2b_ref[...] = w2_ref[0, 0].astype(BF16)

    @pl.when(j < nv_ref[0])
    def _():
        x = xs_ref[...].astype(BF16)
        h1 = jnp.dot(x, w1b_ref[...], preferred_element_type=F32)
        h3 = jnp.dot(x, w3b_ref[...], preferred_element_type=F32)
        h = (_silu(h1) * h3).astype(BF16)
        ys_ref[...] = jnp.dot(h, w2b_ref[...], preferred_element_type=F32)

    @pl.when(j >= nv_ref[0])
    def _():
        ys_ref[...] = jnp.zeros_like(ys_ref)


def _experts(xs, tile_expert, n_valid, w1, w3, w2, layer):
    n_slots, d = xs.shape
    d_e = w1.shape[3]
    tile = lambda j, te, nv: (jnp.minimum(j, nv[0] - 1), 0)
    expert = lambda j, te, nv: (layer, te[j], 0, 0)
    grid_spec = pltpu.PrefetchScalarGridSpec(
        num_scalar_prefetch=2,
        grid=(n_slots // SLOT_TILE,),
        in_specs=[
            pl.BlockSpec((SLOT_TILE, d), tile),
            pl.BlockSpec((1, 1, d, d_e), expert),
            pl.BlockSpec((1, 1, d, d_e), expert),
            pl.BlockSpec((1, 1, d_e, d), expert),
        ],
        out_specs=pl.BlockSpec((SLOT_TILE, d), lambda j, te, nv: (j, 0)),
        scratch_shapes=[pltpu.VMEM((d, d_e), BF16), pltpu.VMEM((d, d_e), BF16), pltpu.VMEM((d_e, d), BF16)],
    )
    return pl.pallas_call(
        _experts_kernel,
        grid_spec=grid_spec,
        out_shape=jax.ShapeDtypeStruct((n_slots, d), F32),
        compiler_params=_cparams("arbitrary"),
        name="moe_experts",
    )(tile_expert, n_valid, xs, w1, w3, w2)


def _combine_kernel(pos1_ref, pos2_ref, xmid_ref, rg_ref, gfin_ref, ys_ref, o_ref, y1_ref, y2_ref, sem,
                    *, final_norm):
    tm = xmid_ref.shape[0]
    for t in range(tm):
        _row_copy(ys_ref, pos1_ref[t], y1_ref, t, sem).start(priority=0)
        _row_copy(ys_ref, pos2_ref[t], y2_ref, t, sem).start(priority=1)
    for _ in range(2 * tm):
        _row_copy(ys_ref, 0, y1_ref, 0, sem).wait()
    rg = rg_ref[...]
    out = xmid_ref[...] + (rg[:, 0:1] * y1_ref[...] + rg[:, 1:2] * y2_ref[...])
    o_ref[...] = _rms(out, gfin_ref[...]) if final_norm else out


def _combine(xmid, rg, pos1, pos2, row0, ys, g_final, final_norm):
    n, d = xmid.shape
    tm = ROUTE_TILE
    t0 = row0 // tm
    smem_rows = lambda: pl.BlockSpec((tm,), lambda i: (t0 + i,), memory_space=pltpu.SMEM)
    kern = functools.partial(_combine_kernel, final_norm=final_norm)
    return pl.pallas_call(
        kern,
        grid=(n // tm,),
        in_specs=[smem_rows(), smem_rows(),
                  pl.BlockSpec((tm, d), lambda i: (i, 0)),
                  pl.BlockSpec((tm, LANES), lambda i: (i, 0)),
                  _resident((1, d), (0, 0)),
                  pl.BlockSpec(memory_space=pl.ANY)],
        out_specs=pl.BlockSpec((tm, d), lambda i: (i, 0)),
        out_shape=jax.ShapeDtypeStruct((n, d), F32),
        scratch_shapes=[pltpu.VMEM((tm, d), F32), pltpu.VMEM((tm, d), F32), pltpu.SemaphoreType.DMA],
        compiler_params=_cparams("arbitrary"),
        name="moe_combine",
    )(pos1, pos2, xmid, rg, g_final, ys)


def _moe_experts(prompt, sample, counts, w1, w3, w2, layer):
    xn_p, rt_p = prompt
    xn_s, rt_s = sample
    n = xn_p.shape[0] + xn_s.shape[0]
    n_slots, offs, tile_expert, n_valid, last_tile, used = _slot_plan(
        counts[:N_EXPERTS, 0].astype(jnp.int32), n)
    rt = jnp.concatenate([rt_p, rt_s], axis=1)
    experts = jnp.arange(N_EXPERTS, dtype=jnp.int32)[:, None]
    base = lambda e: jnp.sum(jnp.where(experts == e[None, :], offs[:, None], 0), axis=0)
    pos1 = base(rt[0]) + rt[2]
    pos2 = base(rt[1]) + rt[3]
    xs = _dispatch(xn_p, xn_s, pos1, pos2, n_slots, last_tile, used, n_valid)
    return _experts(xs, tile_expert, n_valid, w1, w3, w2, layer), pos1, pos2


def kernel(x_prompt, x_sample, state_lru_h, state_lru_conv, state_conv, state_ret, norm_mix_g, norm_ffn_g, final_norm_g, w_in, lru_conv_w, lru_conv_b, lru_wa, lru_ba, lru_wx, lru_bx, lru_lambda, cm_dw_w, cm_dw_b, cm_ln_g, cm_ln_b, ret_norm_g, w_out, moe_w_group, moe_b_group, moe_w_expert, moe_b_expert, moe_w1, moe_w3, moe_w2):
    bp, tp, d = x_prompt.shape
    bs, ts, _ = x_sample.shape
    depth = w_in.shape[0]
    w_a = lru_lambda.shape[1]
    w_b = cm_dw_b.shape[1]
    w_c = ret_norm_g.shape[1]
    n_p, n_s = bp * tp, bs * ts
    row = lambda v: v.reshape(1, -1)

    x_p, x_s = x_prompt.reshape(n_p, d), x_sample.reshape(n_s, d)
    outs = {k: [] for k in ("p_h", "p_ca", "p_cb", "p_s", "s_h", "s_ca", "s_cb")}
    s_states = None
    combine_p = None
    for l in range(depth):
        w_route, b_route = _route_weights(moe_w_group[l], moe_b_group[l], moe_w_expert[l], moe_b_expert[l])
        lw = dict(
            w_a=w_a, w_b=w_b, w_c=w_c, g_mix=row(norm_mix_g[l]), g_ffn=row(norm_ffn_g[l]),
            conv_a_w=lru_conv_w[l], conv_a_b=row(lru_conv_b[l]),
            wg=_blockdiag_gate_weights(lru_wa[l], lru_wx[l]),
            bg=row(jnp.concatenate([lru_ba[l], lru_bx[l]])), lam=row(lru_lambda[l]),
            conv_b_w=cm_dw_w[l], conv_b_b=row(cm_dw_b[l]), ln_g=row(cm_ln_g[l]), ln_b=row(cm_ln_b[l]),
            ret_g=row(ret_norm_g[l]), w_route=w_route, b_route=b_route,
        )
        parts = 1
        xmid_p, xn_p, rt_p, rg_p, seen, h_p, ca_p, cb_p, s_p = _prompt_layer(
            x_p, bp, tp, l, w_in, w_out, lw, parts, combine_p)
        outs["p_h"].append(h_p[:, 0, :])
        outs["p_ca"].append(ca_p[:, SUBLANES - (CONV_A - 1):, :])
        outs["p_cb"].append(cb_p[:, 4 * SUBLANES - (CONV_B - 1):, :])
        outs["p_s"].append(_unpair_state(s_p))
        proj_s = _in_proj(x_s, lw["g_mix"], w_in, l, parts)
        p_t = proj_s[:, :2 * w_a + 2 * w_b].reshape(bs, ts, -1).transpose(1, 0, 2)
        oab_t, h_s, ca_t, cb_t = _mix_sample_ab(
            p_t, state_lru_h, state_lru_conv[l].transpose(1, 0, 2), state_conv[l].transpose(1, 0, 2), l, lw,
            parts)
        oab = oab_t.transpose(1, 0, 2).reshape(n_s, w_a + w_b)
        mix_s, s_states = _mix_sample_c(proj_s, bs, ts, oab, state_ret, l, s_states, lw, parts)
        outs["s_h"].append(h_s)
        outs["s_ca"].append(ca_t.transpose(1, 0, 2))
        outs["s_cb"].append(cb_t.transpose(1, 0, 2))
        xmid_s, xn_s, rt_s, rg_s, counts = _out_proj(x_s, mix_s, w_out, l, seen, lw, parts)
        ys, pos1, pos2 = _moe_experts((xn_p, rt_p), (xn_s, rt_s), counts, moe_w1, moe_w3, moe_w2, l)
        last = l == depth - 1
        x_s = _combine(xmid_s, rg_s, pos1, pos2, n_p, ys, row(final_norm_g), last)
        if last:
            x_p = _combine(xmid_p, rg_p, pos1, pos2, 0, ys, row(final_norm_g), last)
        else:
            x_p, combine_p = xmid_p, (rg_p, pos1, pos2, ys)
    st = lambda k: jnp.stack(outs[k])
    return (x_p.reshape(bp, tp, d), x_s.reshape(bs, ts, d), st("p_h"), st("p_ca"), st("p_cb"), st("p_s"),
            st("s_h"), st("s_ca"), st("s_cb"), s_states)
```

```python
import functools
import math

import jax
import jax.numpy as jnp
from jax import lax
from jax.experimental import pallas as pl
from jax.experimental.pallas import tpu as pltpu

PAST_LEN = 16384
LRU_C = 8.0
CONV_A = 4
CONV_B = 31
NH_C = 8
RET_CHUNK = 128
ROPE_BASE = 10000.0
N_GROUPS = 4
EXPERTS_PER_GROUP = 8
N_EXPERTS = N_GROUPS * EXPERTS_PER_GROUP
EPS = 1e-6

LANES = 128
SUBLANES = 8
VMEM_LIMIT_BYTES = 56 * 1024 * 1024

F32 = jnp.float32
BF16 = jnp.bfloat16


def _cparams(*sem):
    return pltpu.CompilerParams(dimension_semantics=sem, vmem_limit_bytes=VMEM_LIMIT_BYTES)


def _row_tile(n):
    for t in (512, 256, 128, 64, 32, 16, 8):
        if n % t == 0:
            return t
    raise ValueError(f"row count {n} must be a multiple of 8")


def _resident(shape, index):
    return pl.BlockSpec(shape, lambda *_: index, pipeline_mode=pl.Buffered(1))


def _rms(x, g):
    ms = jnp.mean(x * x, axis=-1, keepdims=True)
    return x * lax.rsqrt(ms + EPS) * g


def _sigmoid(x):
    return jax.nn.sigmoid(x)


def _silu(x):
    return x * jax.nn.sigmoid(x)


def _layernorm(x, g, b):
    mu = jnp.mean(x, axis=-1, keepdims=True)
    xc = x - mu
    var = jnp.mean(xc * xc, axis=-1, keepdims=True)
    return xc * lax.rsqrt(var + EPS) * g + b


def _rope_pair(x, cos, sin_signed, first_half):
    partner = jnp.where(first_half, pltpu.roll(x, 96, 1), pltpu.roll(x, 32, 1))
    return x * cos + partner * sin_signed


def _head_norm_pair(o, head0, inv_d):
    zero = jnp.zeros_like(o)
    s0 = jnp.sum(jnp.where(head0, o, zero), axis=-1, keepdims=True)
    s1 = jnp.sum(jnp.where(head0, zero, o), axis=-1, keepdims=True)
    oc = o - jnp.where(head0, s0, s1) * inv_d
    q = oc * oc
    v0 = jnp.sum(jnp.where(head0, q, zero), axis=-1, keepdims=True)
    v1 = jnp.sum(jnp.where(head0, zero, q), axis=-1, keepdims=True)
    return oc * lax.rsqrt(jnp.where(head0, v0, v1) * inv_d + EPS)


def _mm(a, b, dims, parts):
    dot = lambda x, y: lax.dot_general(x, y, (dims, ((), ())), preferred_element_type=F32)
    ah, bh = a.astype(BF16), b.astype(BF16)
    if parts == 1:
        return dot(ah, bh)
    al = (a - ah.astype(F32)).astype(BF16)
    bl = (b - bh.astype(F32)).astype(BF16)
    return dot(ah, bh) + (dot(al, bh) + dot(ah, bl))


ROWS_BY_COLS = ((1,), (0,))
ROWS_BY_ROWS = ((1,), (1,))
COLS_BY_COLS = ((0,), (0,))


def _retention_intra_pair(q, k, v, dmat0, dmat1, head0, parts):
    outs = []
    for j, dm in ((0, dmat0), (1, dmat1)):
        keep = head0 if j == 0 else jnp.logical_not(head0)
        s = _mm(jnp.where(keep, q, 0.0), k, ROWS_BY_ROWS, parts)
        outs.append(_mm(s * dm, v, ROWS_BY_COLS, parts))
    return jnp.where(head0, outs[0], outs[1])


def _lru_gates(xa, wg, bg, lam, parts):
    w_a = xa.shape[-1]
    gates = _mm(xa, wg, ROWS_BY_COLS, parts) + bg
    r = _sigmoid(gates[:, :w_a])
    i = _sigmoid(gates[:, w_a:])
    z = -lam
    softplus = jnp.maximum(z, 0.0) + jnp.log1p(jnp.exp(-jnp.abs(z)))
    log_a = -LRU_C * r * softplus
    a = jnp.exp(log_a)
    bt = jnp.sqrt(-jnp.tanh(log_a) * (a * a + 1.0)) * (i * xa)
    return a, bt


ROUTE_TILE = 512
EXPERT_ROW0 = SUBLANES
assert EXPERTS_PER_GROUP == SUBLANES and N_GROUPS <= SUBLANES and EXPERT_ROW0 + N_EXPERTS <= LANES


def _split_bf16(x):
    hi = x.astype(BF16)
    return hi, (x - hi.astype(F32)).astype(BF16)


def _store_parts(ref, idx, val):
    hi = val.astype(BF16)
    ref[(0,) + idx] = hi
    if ref.shape[0] == 2:
        ref[(1,) + idx] = (val - hi.astype(F32)).astype(BF16)


def _dot_parts(a_ref, a_idx, w_ref, w_idx):
    dot = lambda i, j: jnp.dot(a_ref[(i,) + a_idx], w_ref[(j,) + w_idx], preferred_element_type=F32)
    if a_ref.shape[0] == 1:
        return dot(0, 0)
    return dot(0, 0) + (dot(1, 0) + dot(0, 1))


def _lane_tile(x, width):
    return x if width == x.shape[1] else jnp.concatenate([x] * (width // x.shape[1]), axis=1)


def _bf16_parts(w):
    hi = lax.reduce_precision(w, exponent_bits=8, mantissa_bits=7)
    return hi.astype(BF16), (w - hi).astype(BF16)


def _weight_parts(w, parts):
    if parts == 1:
        return w.astype(BF16)[None]
    return jnp.stack(_bf16_parts(w))


def _route_weights(w_group, b_group, w_expert, b_expert):
    d = w_group.shape[0]
    w = jnp.zeros((LANES, d), F32).at[:N_GROUPS].set(w_group.T).at[EXPERT_ROW0:EXPERT_ROW0 + N_EXPERTS].set(w_expert.T)
    b = jnp.zeros((LANES,), F32).at[:N_GROUPS].set(b_group).at[EXPERT_ROW0:EXPERT_ROW0 + N_EXPERTS].set(b_expert)
    return w, jnp.broadcast_to(b[:, None], (LANES, LANES))


def _ffn_norm_route(xm, g_ffn, w_route, b_route, carry_ref, xn_ref, rt_ref, rg_ref, counts_ref):
    xn = _rms(xm, g_ffn)
    xn_ref[...] = xn
    tm = xn.shape[0]
    xh, xl = _split_bf16(xn)
    wr_hi, wr_lo = _split_bf16(w_route)
    dot_t = lambda w, x: lax.dot_general(w, x, (((1,), (1,)), ((), ())), preferred_element_type=F32)
    lt = dot_t(wr_hi, xh) + (dot_t(wr_hi, xl) + dot_t(wr_lo, xh)) + _lane_tile(b_route, tm)
    row8 = lax.broadcasted_iota(jnp.int32, (SUBLANES, tm), 0)
    neg = jnp.full((SUBLANES, tm), -jnp.inf, F32)
    big = jnp.full((SUBLANES, tm), LANES, jnp.int32)
    first = lambda hit: jnp.min(jnp.where(hit, row8, big), axis=0, keepdims=True)
    gl = jnp.where(row8 < N_GROUPS, lt[0:SUBLANES, :], neg)
    gmax = jnp.max(gl, axis=0, keepdims=True)
    p_top = 1.0 / jnp.sum(jnp.exp(gl - gmax), axis=0, keepdims=True)
    g_sel = first(gl == gmax)
    slab = lambda g: lt[EXPERT_ROW0 + g * SUBLANES:EXPERT_ROW0 + (g + 1) * SUBLANES, :]
    el = slab(N_GROUPS - 1)
    for g in reversed(range(N_GROUPS - 1)):
        el = jnp.where(g_sel == g, slab(g), el)
    v1 = jnp.max(el, axis=0, keepdims=True)
    i1 = first(el == v1)
    el2 = jnp.where(row8 == i1, neg, el)
    v2 = jnp.max(el2, axis=0, keepdims=True)
    i2 = first(el2 == v2)
    ex = jnp.exp(v2 - v1)
    den = 1.0 + ex
    g1 = p_top * (1.0 / den)
    g2 = p_top * (ex / den)
    e1 = g_sel * EXPERTS_PER_GROUP + i1
    e2 = g_sel * EXPERTS_PER_GROUP + i2
    rowe = lax.broadcasted_iota(jnp.int32, (LANES, tm), 0)
    oh1 = rowe == e1
    oh2 = rowe == e2
    onehot = jnp.logical_or(oh1, oh2).astype(BF16)
    earlier = (lax.broadcasted_iota(jnp.int32, (tm, tm), 0)
               < lax.broadcasted_iota(jnp.int32, (tm, tm), 1)).astype(BF16)
    seen = _lane_tile(carry_ref[...], tm) + jnp.dot(onehot, earlier, preferred_element_type=F32)
    rank1 = jnp.sum(jnp.where(oh1, seen, 0.0), axis=0, keepdims=True).astype(jnp.int32)
    rank2 = jnp.sum(jnp.where(oh2, seen, 0.0), axis=0, keepdims=True).astype(jnp.int32)
    total = carry_ref[...] + jnp.sum(onehot.astype(F32), axis=1, keepdims=True)
    carry_ref[...] = total
    counts_ref[...] = total
    zi = jnp.zeros((SUBLANES, tm), jnp.int32)
    rt_ref[...] = jnp.where(row8 == 0, e1, jnp.where(row8 == 1, e2, jnp.where(
        row8 == 2, rank1, jnp.where(row8 == 3, rank2, zi))))
    rg_ref[...] = jnp.where(rowe == 0, g1, jnp.where(rowe == 1, g2, 0.0)).T


CONV_B_ROWS = 64


def _row_copy(src_ref, src_row, dst_ref, dst_row, sem):
    return pltpu.make_async_copy(src_ref.at[pl.ds(src_row, 1), :], dst_ref.at[pl.ds(dst_row, 1), :], sem)


def _prompt_layer_kernel(*refs, tc, w_a, w_b, w_c, chunk, fused_combine):
    if fused_combine:
        (p1c_ref, p2c_ref, p1n_ref, p2n_ref, rgp_ref, ys_ref), refs = refs[:6], refs[6:]
        (y_ref, ysem), refs = refs[-2:], refs[:-2]
    (x_ref, gmix_ref, winb_ref, cos_ref, sin_ref, h0_ref, bufa0_ref, bufb0_ref, s0_ref,
     caw_ref, cab_ref, wg_ref, bg_ref, lam_ref,
     cbw_ref, cbb_ref, lng_ref, lnb_ref, rng_ref,
     dmat_ref, xi_ref, zeta_ref, gmat_ref,
     woutb_ref, gffn_ref, wr_ref, br_ref,
     xmid_ref, xn_ref, rt_ref, rg_ref, counts_ref,
     hout_ref, bufa_out_ref, bufb_out_ref, sout_ref,
     xin_ref, u_ref, proj_ref, mix_ref,
     xpa_ref, xpb_ref, sh_ref, a_ref, b_ref, hseq_ref, hcar_ref, s_ref, carry_ref) = refs
    t = pl.program_id(1)
    n_t = pl.num_programs(1)
    pa = SUBLANES
    pb = 4 * SUBLANES
    parts = winb_ref.shape[0]
    everything = (slice(None), slice(None))

    if fused_combine:
        step = pl.program_id(0) * n_t + t
        slot = step % 2

        def gather(p1_ref, p2_ref, into):
            for r in range(tc):
                _row_copy(ys_ref, p1_ref[r], y_ref.at[into, 0], r, ysem.at[into]).start(priority=0)
                _row_copy(ys_ref, p2_ref[r], y_ref.at[into, 1], r, ysem.at[into]).start(priority=1)

        @pl.when(step == 0)
        def _():
            gather(p1c_ref, p2c_ref, 0)

        for _ in range(2 * tc):
            _row_copy(ys_ref, 0, y_ref.at[slot, 0], 0, ysem.at[slot]).wait()
        gates = rgp_ref[...]
        xin_ref[...] = x_ref[...] + (gates[:, 0:1] * y_ref[slot, 0] + gates[:, 1:2] * y_ref[slot, 1])
    else:
        xin_ref[...] = x_ref[...]

    @pl.when(jnp.logical_and(pl.program_id(0) == 0, t == 0))
    def _():
        carry_ref[...] = jnp.zeros_like(carry_ref)

    @pl.when(t == 0)
    def _():
        xpa_ref[0:pa, :] = bufa0_ref[0]
        xpb_ref[0:pb, :] = bufb0_ref[0]
        hcar_ref[...] = h0_ref[0]
        s_ref[...] = s0_ref[0]

    _store_parts(u_ref, everything, _rms(xin_ref[...], gmix_ref[...]))

    def in_proj(c0, c1):
        for c in range(c0, c1, 4 * LANES):
            cols = (slice(None), slice(c, min(c + 4 * LANES, c1)))
            proj_ref[cols] = _dot_parts(u_ref, everything, winb_ref, cols)

    in_proj(0, 2 * w_a)

    xpa_ref[pa:pa + tc, :] = proj_ref[:, w_a:2 * w_a]
    xa = cab_ref[...] + caw_ref[0:1, :] * xpa_ref[pa - 3:pa - 3 + tc, :]
    for j in range(1, CONV_A):
        xa = xa + caw_ref[j:j + 1, :] * xpa_ref[pa - 3 + j:pa - 3 + j + tc, :]
    a, bt = _lru_gates(xa, wg_ref[...], bg_ref[...], lam_ref[...], parts)
    a_ref[...] = a
    b_ref[...] = bt
    row8 = lax.broadcasted_iota(jnp.int32, (SUBLANES, w_a), 0)

    def scan_block(i, hprev):
        r0 = pl.multiple_of(i * SUBLANES, SUBLANES)
        ab = a_ref[pl.ds(r0, SUBLANES), :]
        bb = b_ref[pl.ds(r0, SUBLANES), :]
        for k in (1, 2, 4):
            a_sh = jnp.where(row8 >= k, pltpu.roll(ab, k, 0), 1.0)
            b_sh = jnp.where(row8 >= k, pltpu.roll(bb, k, 0), 0.0)
            bb = ab * b_sh + bb
            ab = ab * a_sh
        h = ab * hprev + bb
        hseq_ref[pl.ds(r0, SUBLANES), :] = h
        return jnp.broadcast_to(h[SUBLANES - 1:SUBLANES, :], (SUBLANES, w_a))

    hcar_ref[...] = lax.fori_loop(0, tc // SUBLANES, scan_block, hcar_ref[...])
    if fused_combine:
        gather(p1n_ref, p2n_ref, 1 - slot)
    in_proj(2 * w_a, proj_ref.shape[1])
    _store_parts(mix_ref, (slice(None), slice(0, w_a)), jax.nn.gelu(proj_ref[:, 0:w_a]) * hseq_ref[...])
    xpa_ref[0:pa, :] = xpa_ref[tc:tc + pa, :]

    b0 = 2 * w_a
    xpb_ref[pb:pb + tc, :] = proj_ref[:, b0:b0 + w_b] * _sigmoid(proj_ref[:, b0 + w_b:b0 + 2 * w_b])
    first = pb - (CONV_B - 1)
    for r in range(SUBLANES):
        span = tc + (CONV_B - 1 - r) // SUBLANES * SUBLANES
        sh_ref[r, 0:span, :] = xpb_ref[first + r:first + r + span, :]
    for blk in range(tc // CONV_B_ROWS):
        r0 = blk * CONV_B_ROWS
        acc = jnp.broadcast_to(cbb_ref[...], (CONV_B_ROWS, w_b))
        for j in range(CONV_B):
            r = j % SUBLANES
            acc = acc + cbw_ref[j:j + 1, :] * sh_ref[r, r0 + j - r:r0 + j - r + CONV_B_ROWS, :]
        out_b = _silu(_layernorm(acc, lng_ref[...], lnb_ref[...]))
        _store_parts(mix_ref, (slice(r0, r0 + CONV_B_ROWS), slice(w_a, w_a + w_b)), out_b)
    xpb_ref[0:pb, :] = xpb_ref[tc:tc + pb, :]

    c0 = 2 * w_a + 2 * w_b
    n_pair = w_c // LANES
    lane = lax.broadcasted_iota(jnp.int32, (chunk, LANES), 1)
    head0 = lane < (LANES // 2)
    first_half = (lane % (LANES // 2)) < (LANES // 4)
    rl = lax.broadcasted_iota(jnp.int32, (LANES, LANES), 0) < (LANES // 2)
    cl = lax.broadcasted_iota(jnp.int32, (LANES, LANES), 1) < (LANES // 2)
    same_head = rl == cl
    inv_d = 2.0 / LANES
    for c in range(tc // chunk):
        rows = slice(c * chunk, (c + 1) * chunk)
        cos = cos_ref[rows, :]
        sin = sin_ref[rows, :]
        for p in range(n_pair):
            cq = c0 + p * LANES
            q = _rope_pair(proj_ref[rows, cq:cq + LANES], cos, sin, first_half)
            k = _rope_pair(proj_ref[rows, cq + w_c:cq + w_c + LANES], cos, sin, first_half) * (2.0 / LANES) ** 0.5
            v = proj_ref[rows, cq + 2 * w_c:cq + 2 * w_c + LANES]
            g = proj_ref[rows, cq + 3 * w_c:cq + 3 * w_c + LANES]
            o = _retention_intra_pair(q, k, v, dmat_ref[2 * p], dmat_ref[2 * p + 1], head0, parts)
            s_old = s_ref[p]
            o = o + _mm(q, s_old, ROWS_BY_COLS, parts) * xi_ref[p]
            upd = _mm(k * zeta_ref[p], v, COLS_BY_COLS, parts)
            s_ref[p] = gmat_ref[p] * s_old + jnp.where(same_head, upd, 0.0)
            on = _head_norm_pair(o, head0, inv_d) * rng_ref[:, p * LANES:(p + 1) * LANES]
            _store_parts(mix_ref, (rows, slice(w_a + w_b + p * LANES, w_a + w_b + (p + 1) * LANES)), _silu(g) * on)

    xm = xin_ref[...] + _dot_parts(mix_ref, everything, woutb_ref, everything)
    xmid_ref[...] = xm
    _ffn_norm_route(xm, gffn_ref[...], wr_ref[...], br_ref[...], carry_ref,
                    xn_ref, rt_ref, rg_ref, counts_ref)

    @pl.when(t == n_t - 1)
    def _():
        hout_ref[0] = hcar_ref[...]
        bufa_out_ref[0] = xpa_ref[0:pa, :]
        bufb_out_ref[0] = xpb_ref[0:pb, :]
        sout_ref[0] = s_ref[...]

    if fused_combine:
        @pl.when(step == pl.num_programs(0) * n_t - 1)
        def _():
            for _ in range(2 * tc):
                _row_copy(ys_ref, 0, y_ref.at[1 - slot, 0], 0, ysem.at[1 - slot]).wait()


def _retention_tables(chunk):
    log_g = jnp.log1p(-jnp.exp2(-5.0 - jnp.arange(NH_C, dtype=F32)))
    idx = jnp.arange(chunk, dtype=F32)
    diff = idx[:, None] - idx[None, :]
    causal = diff >= 0
    dmat = jnp.where(causal[None], jnp.exp(jnp.where(causal, diff, 0.0)[None] * log_g[:, None, None]), 0.0)
    xi = jnp.exp((idx + 1.0)[None, :] * log_g[:, None])
    zeta = jnp.exp((chunk - 1.0 - idx)[None, :] * log_g[:, None])
    g_chunk = jnp.exp(chunk * log_g)
    return dmat, xi, zeta, g_chunk


def _pair_lanes(per_head):
    h, rows = per_head.shape
    half = LANES // 2
    x = jnp.broadcast_to(per_head.reshape(h // 2, 2, rows, 1), (h // 2, 2, rows, half))
    return x.transpose(0, 2, 1, 3).reshape(h // 2, rows, LANES)


def _rope_tables(pos):
    half = LANES // 4
    inv = ROPE_BASE ** (-jnp.arange(half, dtype=F32) / half)
    ang = pos.astype(F32)[:, None] * inv[None, :]
    cos, sin = jnp.cos(ang), jnp.sin(ang)
    return jnp.concatenate([cos] * 4, axis=1), jnp.concatenate([-sin, sin, -sin, sin], axis=1)


def _blockdiag_gate_weights(wa, wx):
    nh, hd, _ = wa.shape
    eye = jnp.eye(nh, dtype=wa.dtype)
    da = (eye[:, None, :, None] * wa[:, :, None, :]).reshape(nh * hd, nh * hd)
    dx = (eye[:, None, :, None] * wx[:, :, None, :]).reshape(nh * hd, nh * hd)
    return jnp.concatenate([da, dx], axis=1)


def _prompt_layer(x, n_seq, seq_len, layer, w_in, w_out, lw, parts, combine=None):
    w_a, w_b, w_c = lw["w_a"], lw["w_b"], lw["w_c"]
    n, d = x.shape
    d_in = w_in.shape[2]
    d_mix = w_a + w_b + w_c
    chunk = math.gcd(seq_len, RET_CHUNK)
    tc = ROUTE_TILE
    assert seq_len % tc == 0 and tc % chunk == 0 and tc % CONV_B_ROWS == 0
    n_t = seq_len // tc
    n_pair = w_c // LANES
    dmat, xi, zeta, g_chunk = _retention_tables(chunk)
    half = LANES // 2
    gm = jnp.zeros((n_pair, 2, half, 2, half), F32)
    gm = gm.at[:, 0, :, 0, :].set(g_chunk[0::2, None, None]).at[:, 1, :, 1, :].set(g_chunk[1::2, None, None])
    cos, sin = _rope_tables(jnp.arange(seq_len, dtype=jnp.int32))
    zeros = lambda *s: jnp.zeros(s, F32)
    full = lambda shape: _resident(shape, (0,) * len(shape))
    per_seq = lambda shape: pl.BlockSpec((1,) + shape, lambda b, t: (b,) + (0,) * len(shape))
    tile = lambda width: pl.BlockSpec((tc, width), lambda b, t: (b * n_t + t, 0))
    kern = functools.partial(_prompt_layer_kernel, tc=tc, w_a=w_a, w_b=w_b, w_c=w_c, chunk=chunk,
                             fused_combine=combine is not None)
    extra_in, extra_scratch, extra_args = [], [], []
    if combine is not None:
        rg_prev, pos1, pos2, ys = combine
        last = n // tc - 1
        cur = lambda: pl.BlockSpec((tc,), lambda b, t: (b * n_t + t,), memory_space=pltpu.SMEM)
        nxt = lambda: pl.BlockSpec((tc,), lambda b, t: (jnp.minimum(b * n_t + t + 1, last),),
                                   memory_space=pltpu.SMEM)
        extra_in = [cur(), cur(), nxt(), nxt(), tile(LANES), pl.BlockSpec(memory_space=pl.ANY)]
        extra_args = [pos1, pos2, pos1, pos2, rg_prev, ys]
        extra_scratch = [pltpu.VMEM((2, 2, tc, d), F32), pltpu.SemaphoreType.DMA((2,))]
    return pl.pallas_call(
        kern,
        grid=(n_seq, n_t),
        in_specs=extra_in + [
            tile(d), full((1, d)), full((parts, d, d_in)),
            pl.BlockSpec((tc, LANES), lambda b, t: (t, 0)),
            pl.BlockSpec((tc, LANES), lambda b, t: (t, 0)),
            per_seq((SUBLANES, w_a)), per_seq((SUBLANES, w_a)), per_seq((4 * SUBLANES, w_b)),
            per_seq((n_pair, LANES, LANES)),
            full((CONV_A, w_a)), full((1, w_a)), full((w_a, 2 * w_a)), full((1, 2 * w_a)), full((1, w_a)),
            full((CONV_B, w_b)), full((1, w_b)), full((1, w_b)), full((1, w_b)), full((1, w_c)),
            full((NH_C, chunk, chunk)), full((n_pair, chunk, LANES)), full((n_pair, chunk, LANES)),
            full((n_pair, LANES, LANES)),
            full((parts, d_mix, d)), full((1, d)),
            full((LANES, d)), full((LANES, LANES)),
        ],
        out_specs=[
            tile(d), tile(d),
            pl.BlockSpec((SUBLANES, tc), lambda b, t: (0, b * n_t + t)),
            tile(LANES),
            pl.BlockSpec((LANES, LANES), lambda b, t: (0, 0)),
            per_seq((SUBLANES, w_a)), per_seq((SUBLANES, w_a)), per_seq((4 * SUBLANES, w_b)),
            per_seq((n_pair, LANES, LANES)),
        ],
        out_shape=[
            jax.ShapeDtypeStruct((n, d), F32),
            jax.ShapeDtypeStruct((n, d), F32),
            jax.ShapeDtypeStruct((SUBLANES, n), jnp.int32),
            jax.ShapeDtypeStruct((n, LANES), F32),
            jax.ShapeDtypeStruct((LANES, LANES), F32),
            jax.ShapeDtypeStruct((n_seq, SUBLANES, w_a), F32),
            jax.ShapeDtypeStruct((n_seq, SUBLANES, w_a), F32),
            jax.ShapeDtypeStruct((n_seq, 4 * SUBLANES, w_b), F32),
            jax.ShapeDtypeStruct((n_seq, n_pair, LANES, LANES), F32),
        ],
        scratch_shapes=[
            pltpu.VMEM((tc, d), F32),
            pltpu.VMEM((parts, tc, d), BF16),
            pltpu.VMEM((tc, d_in), F32),
            pltpu.VMEM((parts, tc, d_mix), BF16),
            pltpu.VMEM((SUBLANES + tc, w_a), F32),
            pltpu.VMEM((4 * SUBLANES + tc, w_b), F32),
            pltpu.VMEM((SUBLANES, 3 * SUBLANES + tc, w_b), F32),
            pltpu.VMEM((tc, w_a), F32),
            pltpu.VMEM((tc, w_a), F32),
            pltpu.VMEM((tc, w_a), F32),
            pltpu.VMEM((SUBLANES, w_a), F32),
            pltpu.VMEM((n_pair, LANES, LANES), F32),
            pltpu.VMEM((LANES, LANES), F32),
        ] + extra_scratch,
        compiler_params=_cparams("arbitrary", "arbitrary"),
        name="prompt_layer",
    )(*extra_args, x, lw["g_mix"], _weight_parts(w_in[layer], parts), cos, sin,
      zeros(n_seq, SUBLANES, w_a), zeros(n_seq, SUBLANES, w_a), zeros(n_seq, 4 * SUBLANES, w_b),
      zeros(n_seq, n_pair, LANES, LANES),
      lw["conv_a_w"], lw["conv_a_b"], lw["wg"], lw["bg"], lw["lam"],
      lw["conv_b_w"], lw["conv_b_b"], lw["ln_g"], lw["ln_b"], lw["ret_g"],
      dmat, _pair_lanes(xi), _pair_lanes(zeta), gm.reshape(n_pair, LANES, LANES),
      _weight_parts(w_out[layer], parts), lw["g_ffn"], lw["w_route"], lw["b_route"])


def _unpair_state(s_pairs):
    b, n_pair = s_pairs.shape[:2]
    half = LANES // 2
    s = s_pairs.reshape(b, n_pair, 2, half, 2, half)
    return jnp.stack([s[:, :, 0, :, 0, :], s[:, :, 1, :, 1, :]], axis=2).reshape(b, 2 * n_pair, half, half)


def _in_proj_kernel(x_ref, g_ref, w_ref, o_ref, *, parts):
    o_ref[...] = _mm(_rms(x_ref[...], g_ref[...]), w_ref[0], ROWS_BY_COLS, parts)


def _in_proj(x, g, w_in, layer, parts):
    n, d = x.shape
    d_in = w_in.shape[2]
    tm = _row_tile(n)
    return pl.pallas_call(
        functools.partial(_in_proj_kernel, parts=parts),
        grid=(n // tm,),
        in_specs=[
            pl.BlockSpec((tm, d), lambda i: (i, 0)),
            _resident((1, d), (0, 0)),
            _resident((1, d, d_in), (layer, 0, 0)),
        ],
        out_specs=pl.BlockSpec((tm, d_in), lambda i: (i, 0)),
        out_shape=jax.ShapeDtypeStruct((n, d_in), F32),
        compiler_params=_cparams("arbitrary"),
        name="in_proj_sample",
    )(x, g, w_in)


def _mix_sample_ab_kernel(p_ref, h0_ref, bufa_ref, bufb_ref,
                          caw_ref, cab_ref, wg_ref, bg_ref, lam_ref,
                          cbw_ref, cbb_ref, lng_ref, lnb_ref,
                          o_ref, h_ref, bufa_out_ref, bufb_out_ref, *, ts, w_a, w_b, parts):
    xpa = [bufa_ref[j] for j in range(CONV_A - 1)] + [p_ref[t, :, w_a:2 * w_a] for t in range(ts)]
    xa = []
    for t in range(ts):
        acc = cab_ref[...] + caw_ref[0:1, :] * xpa[t]
        for j in range(1, CONV_A):
            acc = acc + caw_ref[j:j + 1, :] * xpa[t + j]
        xa.append(acc)
    nb = xa[0].shape[0]
    a, bt = _lru_gates(jnp.concatenate(xa, axis=0), wg_ref[...], bg_ref[...], lam_ref[...], parts)
    h = h0_ref[...]
    for t in range(ts):
        h = a[t * nb:(t + 1) * nb, :] * h + bt[t * nb:(t + 1) * nb, :]
        o_ref[t, :, 0:w_a] = jax.nn.gelu(p_ref[t, :, 0:w_a]) * h
    h_ref[...] = h
    for j in range(CONV_A - 1):
        bufa_out_ref[j] = xpa[ts + j]
    b0 = 2 * w_a
    hist = CONV_B - 1
    glu = [p_ref[t, :, b0:b0 + w_b] * _sigmoid(p_ref[t, :, b0 + w_b:b0 + 2 * w_b]) for t in range(ts)]
    xpb = lambda i: bufb_ref[i] if i < hist else glu[i - hist]
    for t in range(ts):
        acc = cbb_ref[...] + cbw_ref[0:1, :] * xpb(t)
        for j in range(1, CONV_B):
            acc = acc + cbw_ref[j:j + 1, :] * xpb(t + j)
        o_ref[t, :, w_a:w_a + w_b] = _silu(_layernorm(acc, lng_ref[...], lnb_ref[...]))
    for i in range(hist):
        bufb_out_ref[i] = xpb(ts + i)


def _mix_sample_ab(p_t, h0, bufa_t, bufb_t, layer, lw, parts):
    ts, n_seq, width = p_t.shape
    w_a, w_b = lw["w_a"], lw["w_b"]
    nb = 32 if n_seq % 32 == 0 else n_seq
    full = lambda shape: _resident(shape, (0,) * len(shape))
    kern = functools.partial(_mix_sample_ab_kernel, ts=ts, w_a=w_a, w_b=w_b, parts=parts)
    return pl.pallas_call(
        kern,
        grid=(n_seq // nb,),
        in_specs=[
            pl.BlockSpec((ts, nb, width), lambda i: (0, i, 0)),
            pl.BlockSpec((None, nb, w_a), lambda i: (layer, i, 0)),
            pl.BlockSpec((CONV_A - 1, nb, w_a), lambda i: (0, i, 0)),
            pl.BlockSpec((CONV_B - 1, nb, w_b), lambda i: (0, i, 0)),
            full((CONV_A, w_a)), full((1, w_a)), full((w_a, 2 * w_a)), full((1, 2 * w_a)), full((1, w_a)),
            full((CONV_B, w_b)), full((1, w_b)), full((1, w_b)), full((1, w_b)),
        ],
        out_specs=[
            pl.BlockSpec((ts, nb, w_a + w_b), lambda i: (0, i, 0)),
            pl.BlockSpec((nb, w_a), lambda i: (i, 0)),
            pl.BlockSpec((CONV_A - 1, nb, w_a), lambda i: (0, i, 0)),
            pl.BlockSpec((CONV_B - 1, nb, w_b), lambda i: (0, i, 0)),
        ],
        out_shape=[
            jax.ShapeDtypeStruct((ts, n_seq, w_a + w_b), F32),
            jax.ShapeDtypeStruct((n_seq, w_a), F32),
            jax.ShapeDtypeStruct((CONV_A - 1, n_seq, w_a), F32),
            jax.ShapeDtypeStruct((CONV_B - 1, n_seq, w_b), F32),
        ],
        compiler_params=_cparams("arbitrary"),
        name="mix_sample_ab",
    )(p_t, h0, bufa_t, bufb_t,
      lw["conv_a_w"], lw["conv_a_b"], lw["wg"], lw["bg"], lw["lam"],
      lw["conv_b_w"], lw["conv_b_b"], lw["ln_g"], lw["ln_b"])


def _mix_sample_c_kernel(q_ref, k_ref, v_ref, g_ref, oab_ref, s_ref, cos_ref, sin_ref,
                         dmat_ref, xi_ref, zeta_ref, gcol_ref, rng_ref, tile_ref, fold_ref, *rest,
                         rows, ts, w_c, w_ab, layer, parts):
    if layer:
        prev_ref, mix_ref, sout_ref, qs_ref, kzs_ref, vs_ref, oi_ref = rest
        for l in range(layer):
            sout_ref[l] = prev_ref[l]
    else:
        mix_ref, sout_ref, qs_ref, kzs_ref, vs_ref, oi_ref = rest
    n_pair = w_c // LANES
    hd = LANES // 2
    lane = lax.broadcasted_iota(jnp.int32, (rows, LANES), 1)
    head0 = lane < hd
    first_half = (lane % hd) < (hd // 2)
    cos = cos_ref[...]
    sin = sin_ref[...]
    for p in range(n_pair):
        cs = slice(p * LANES, (p + 1) * LANES)
        q = _rope_pair(q_ref[:, cs], cos, sin, first_half)
        k = _rope_pair(k_ref[:, cs], cos, sin, first_half) * (1.0 / hd) ** 0.5
        v = v_ref[:, cs]
        oi_ref[:, cs] = _retention_intra_pair(q, k, v, dmat_ref[2 * p], dmat_ref[2 * p + 1], head0, parts)
        qs_ref[:, cs] = q
        kzs_ref[:, cs] = k * zeta_ref[:, cs]
        vs_ref[:, cs] = v
    grp = SUBLANES
    per_grp = grp // ts
    stack = NH_C * grp
    srow = lax.broadcasted_iota(jnp.int32, (stack, w_c), 0)
    slane = lax.broadcasted_iota(jnp.int32, (stack, w_c), 1)
    own_head = (srow // grp) == (slane // hd)
    tile_m = tile_ref[...]
    fold_m = fold_ref[...]
    for gi in range(rows // grp):
        rs = slice(gi * grp, (gi + 1) * grp)
        q8 = jnp.concatenate([qs_ref[rs, :]] * NH_C, axis=0)
        kz8 = jnp.concatenate([kzs_ref[rs, :]] * NH_C, axis=0)
        v8 = jnp.concatenate([vs_ref[rs, :]] * NH_C, axis=0)
        rest = jnp.where(own_head, v8, 0.0)
        vr = jnp.zeros((stack, hd), F32)
        for _ in range(3 if parts == 2 else 1):
            hi = rest.astype(BF16)
            vr = vr + jnp.dot(hi, fold_m, preferred_element_type=F32)
            rest = rest - hi.astype(F32)
        o_stack = jnp.zeros((stack, hd), F32)
        for s2 in range(per_grp):
            sq = gi * per_grp + s2
            own_seq = ((srow % grp) // ts) == s2
            sel = jnp.logical_and(own_head, own_seq)
            s_old = s_ref[0, sq].reshape(w_c, hd)
            o_stack = o_stack + _mm(jnp.where(sel, q8, 0.0), s_old, ROWS_BY_COLS, parts)
            upd = _mm(jnp.where(sel, kz8, 0.0), vr, COLS_BY_COLS, parts)
            sout_ref[layer, sq] = (gcol_ref[...] * s_old + upd).reshape(NH_C, hd, hd)
        p1 = o_stack.astype(BF16)
        r1 = o_stack - p1.astype(F32)
        p2 = r1.astype(BF16)
        p3 = (r1 - p2.astype(F32)).astype(BF16)
        spread = (jnp.dot(p1, tile_m, preferred_element_type=F32)
                  + jnp.dot(p2, tile_m, preferred_element_type=F32)
                  + jnp.dot(p3, tile_m, preferred_element_type=F32))
        spread = jnp.where(own_head, spread, 0.0).reshape(NH_C, grp, w_c)
        oi_ref[rs, :] = oi_ref[rs, :] + jnp.sum(spread, axis=0) * xi_ref[rs, :]
    mix_ref[:, 0:w_ab] = oab_ref[...]
    for p in range(n_pair):
        cs = slice(p * LANES, (p + 1) * LANES)
        on = _head_norm_pair(oi_ref[:, cs], head0, 1.0 / hd) * rng_ref[:, cs]
        mix_ref[:, w_ab + p * LANES:w_ab + (p + 1) * LANES] = _silu(g_ref[:, cs]) * on


def _mix_sample_c(proj, n_seq, ts, oab, state, layer, prev_states, lw, parts):
    w_a, w_b, w_c = lw["w_a"], lw["w_b"], lw["w_c"]
    hd = LANES // 2
    n_rows = n_seq * ts
    rows = 64 if n_rows % 64 == 0 else n_rows
    assert SUBLANES % ts == 0 and rows % SUBLANES == 0
    seq_per = rows // ts
    c0 = 2 * w_a + 2 * w_b
    assert c0 % w_c == 0
    cb = c0 // w_c
    log_g = jnp.log1p(-jnp.exp2(-5.0 - jnp.arange(NH_C, dtype=F32)))
    r = jnp.arange(rows)
    step = (r % ts).astype(F32)
    diff = step[:, None] - step[None, :]
    live = jnp.logical_and((r[:, None] // ts) == (r[None, :] // ts), diff >= 0)
    dmat = jnp.where(live[None], jnp.exp(jnp.where(live, diff, 0.0)[None] * log_g[:, None, None]), 0.0)
    xi = jnp.repeat(jnp.exp((step + 1.0)[None, :] * log_g[:, None]).T, hd, axis=1)
    zeta = jnp.repeat(jnp.exp((ts - 1.0 - step)[None, :] * log_g[:, None]).T, hd, axis=1)
    gcol = jnp.broadcast_to(jnp.repeat(jnp.exp(ts * log_g), hd)[:, None], (w_c, hd))
    pos = PAST_LEN + (r % ts).astype(jnp.int32)
    cos, sin = _rope_tables(pos)
    tile_m = jnp.tile(jnp.eye(hd, dtype=BF16), (1, NH_C))
    fold_m = jnp.tile(jnp.eye(hd, dtype=BF16), (NH_C, 1))
    full = lambda shape: _resident(shape, (0,) * len(shape))
    col = lambda j: pl.BlockSpec((rows, w_c), lambda i: (i, cb + j))
    states = lambda n_layers, first: pl.BlockSpec((n_layers, seq_per, NH_C, hd, hd), lambda i: (first, i, 0, 0, 0))
    kern = functools.partial(_mix_sample_c_kernel, rows=rows, ts=ts, w_c=w_c, w_ab=w_a + w_b, layer=layer,
                             parts=parts)
    in_specs = [
        col(0), col(1), col(2), col(3),
        pl.BlockSpec((rows, w_a + w_b), lambda i: (i, 0)),
        states(1, layer),
        full((rows, LANES)), full((rows, LANES)),
        full((NH_C, rows, rows)), full((rows, w_c)), full((rows, w_c)), full((w_c, hd)), full((1, w_c)),
        full((hd, w_c)), full((w_c, hd)),
    ]
    args = [proj, proj, proj, proj, oab, state, cos, sin, dmat, xi, zeta, gcol, lw["ret_g"], tile_m, fold_m]
    if layer:
        in_specs.append(states(layer, 0))
        args.append(prev_states)
    return pl.pallas_call(
        kern,
        grid=(n_rows // rows,),
        in_specs=in_specs,
        out_specs=[
            pl.BlockSpec((rows, w_a + w_b + w_c), lambda i: (i, 0)),
            states(layer + 1, 0),
        ],
        out_shape=[
            jax.ShapeDtypeStruct((n_rows, w_a + w_b + w_c), F32),
            jax.ShapeDtypeStruct((layer + 1,) + state.shape[1:], F32),
        ],
        scratch_shapes=[pltpu.VMEM((rows, w_c), F32)] * 4,
        compiler_params=_cparams("arbitrary"),
        name="mix_sample_c",
    )(*args)


def _out_proj_kernel(x_ref, mix_ref, w_ref, g_ref, wr_ref, br_ref, seen_ref,
                     xmid_ref, xn_ref, rt_ref, rg_ref, counts_ref, carry_ref, *, parts):
    @pl.when(pl.program_id(0) == 0)
    def _():
        carry_ref[...] = seen_ref[...]

    xm = x_ref[...] + _mm(mix_ref[...], w_ref[0], ROWS_BY_COLS, parts)
    xmid_ref[...] = xm
    _ffn_norm_route(xm, g_ref[...], wr_ref[...], br_ref[...], carry_ref,
                    xn_ref, rt_ref, rg_ref, counts_ref)


def _out_proj(x, mix, w_out, layer, seen, lw, parts):
    n, d = x.shape
    d_mix = mix.shape[1]
    tm = ROUTE_TILE
    assert n % tm == 0
    full = lambda shape: _resident(shape, (0,) * len(shape))
    rows = lambda width: pl.BlockSpec((tm, width), lambda i: (i, 0))
    return pl.pallas_call(
        functools.partial(_out_proj_kernel, parts=parts),
        grid=(n // tm,),
        in_specs=[rows(d), rows(d_mix), _resident((1, d_mix, d), (layer, 0, 0)),
                  full((1, d)), full((LANES, d)), full((LANES, LANES)), full((LANES, LANES))],
        out_specs=[rows(d), rows(d), pl.BlockSpec((SUBLANES, tm), lambda i: (0, i)), rows(LANES),
                   pl.BlockSpec((LANES, LANES), lambda i: (0, 0))],
        out_shape=[
            jax.ShapeDtypeStruct((n, d), F32),
            jax.ShapeDtypeStruct((n, d), F32),
            jax.ShapeDtypeStruct((SUBLANES, n), jnp.int32),
            jax.ShapeDtypeStruct((n, LANES), F32),
            jax.ShapeDtypeStruct((LANES, LANES), F32),
        ],
        scratch_shapes=[pltpu.VMEM((LANES, LANES), F32)],
        compiler_params=_cparams("arbitrary"),
        name="out_proj_sample",
    )(x, mix, w_out, lw["g_ffn"], lw["w_route"], lw["b_route"], seen)


SLOT_TILE = 256


def _slot_plan(counts, n_tokens):
    n_slots = 2 * n_tokens + N_EXPERTS * SLOT_TILE
    n_slots = -(-n_slots // SLOT_TILE) * SLOT_TILE
    n_tiles = n_slots // SLOT_TILE
    padded = (counts + (SLOT_TILE - 1)) // SLOT_TILE * SLOT_TILE
    ends = jnp.cumsum(padded)
    offs = ends - padded
    total = ends[-1]
    tile_start = jnp.arange(n_tiles, dtype=jnp.int32) * SLOT_TILE
    probe = jnp.minimum(tile_start, total - 1)
    tile_expert = jnp.sum((ends[None, :] <= probe[:, None]).astype(jnp.int32), axis=1)
    tile_expert = jnp.minimum(tile_expert, N_EXPERTS - 1)
    n_valid = (total // SLOT_TILE).reshape(1)
    last_tile = jnp.where(padded > 0, ends - SLOT_TILE, 0)
    return n_slots, offs, tile_expert, n_valid, last_tile, (padded > 0).astype(jnp.int32)


def _dispatch_kernel(last_ref, used_ref, nv_ref, pos1_ref, pos2_ref, xnp_ref, xns_ref, xs_ref,
                     src_ref, zero_ref, zsem, sem, *, n_prompt_tiles):
    tm = xnp_ref.shape[0]
    n_tiles = xs_ref.shape[0] // SLOT_TILE
    i = pl.program_id(0)

    @pl.when(i == 0)
    def _():
        zero_ref[...] = jnp.zeros_like(zero_ref)
        tile = lambda row: pltpu.make_async_copy(
            zero_ref, xs_ref.at[pl.ds(pl.multiple_of(row, SLOT_TILE), SLOT_TILE), :], zsem)
        for e in range(N_EXPERTS):
            @pl.when(used_ref[e] != 0)
            def _():
                tile(last_ref[e]).start()
        lax.fori_loop(nv_ref[0], n_tiles, lambda j, c: (tile(j * SLOT_TILE).start(), c)[1], 0)
        for e in range(N_EXPERTS):
            @pl.when(used_ref[e] != 0)
            def _():
                tile(0).wait()
        lax.fori_loop(nv_ref[0], n_tiles, lambda j, c: (tile(0).wait(), c)[1], 0)

    src_ref[...] = jnp.where(i < n_prompt_tiles, xnp_ref[...], xns_ref[...])
    for t in range(tm):
        _row_copy(src_ref, t, xs_ref, pos1_ref[t], sem).start(priority=0)
        _row_copy(src_ref, t, xs_ref, pos2_ref[t], sem).start(priority=1)
    for _ in range(2 * tm):
        _row_copy(src_ref, 0, xs_ref, 0, sem).wait()


def _dispatch(xn_p, xn_s, pos1, pos2, n_slots, last_tile, used, n_valid):
    n_p, d = xn_p.shape
    n_s = xn_s.shape[0]
    tm = ROUTE_TILE
    tp = n_p // tm
    ts_last = n_s // tm - 1
    smem_rows = lambda: pl.BlockSpec((tm,), lambda i, *_: (i,), memory_space=pltpu.SMEM)
    grid_spec = pltpu.PrefetchScalarGridSpec(
        num_scalar_prefetch=3,
        grid=((n_p + n_s) // tm,),
        in_specs=[smem_rows(), smem_rows(),
                  pl.BlockSpec((tm, d), lambda i, *_: (jnp.minimum(i, tp - 1), 0)),
                  pl.BlockSpec((tm, d), lambda i, *_: (jnp.clip(i - tp, 0, ts_last), 0))],
        out_specs=pl.BlockSpec(memory_space=pl.ANY),
        scratch_shapes=[pltpu.VMEM((tm, d), F32), pltpu.VMEM((SLOT_TILE, d), F32),
                        pltpu.SemaphoreType.DMA, pltpu.SemaphoreType.DMA],
    )
    return pl.pallas_call(
        functools.partial(_dispatch_kernel, n_prompt_tiles=tp),
        grid_spec=grid_spec,
        out_shape=jax.ShapeDtypeStruct((n_slots, d), F32),
        compiler_params=_cparams("arbitrary"),
        name="moe_dispatch",
    )(last_tile, used, n_valid, pos1, pos2, xn_p, xn_s)


def _experts_kernel(te_ref, nv_ref, xs_ref, w1_ref, w3_ref, w2_ref, ys_ref, w1b_ref, w3b_ref, w2b_ref):
    j = pl.program_id(0)
    fresh = jnp.logical_or(j == 0, te_ref[j] != te_ref[jnp.maximum(j - 1, 0)])

    @pl.when(fresh)
    def _():
        w1b_ref[...] = w1_ref[0, 0].astype(BF16)
        w3b_ref[...] = w3_ref[0, 0].astype(BF16)
        w2b_ref[...] = w2_ref[0, 0].astype(BF16)

    @pl.when(j < nv_ref[0])
    def _():
        x = xs_ref[...].astype(BF16)
        h1 = jnp.dot(x, w1b_ref[...], preferred_element_type=F32)
        h3 = jnp.dot(x, w3b_ref[...], preferred_element_type=F32)
        h = (_silu(h1) * h3).astype(BF16)
        ys_ref[...] = jnp.dot(h, w2b_ref[...], preferred_element_type=F32)

    @pl.when(j >= nv_ref[0])
    def _():
        ys_ref[...] = jnp.zeros_like(ys_ref)


def _experts(xs, tile_expert, n_valid, w1, w3, w2, layer):
    n_slots, d = xs.shape
    d_e = w1.shape[3]
    tile = lambda j, te, nv: (jnp.minimum(j, nv[0] - 1), 0)
    expert = lambda j, te, nv: (layer, te[j], 0, 0)
    grid_spec = pltpu.PrefetchScalarGridSpec(
        num_scalar_prefetch=2,
        grid=(n_slots // SLOT_TILE,),
        in_specs=[
            pl.BlockSpec((SLOT_TILE, d), tile),
            pl.BlockSpec((1, 1, d, d_e), expert),
            pl.BlockSpec((1, 1, d, d_e), expert),
            pl.BlockSpec((1, 1, d_e, d), expert),
        ],
        out_specs=pl.BlockSpec((SLOT_TILE, d), lambda j, te, nv: (j, 0)),
        scratch_shapes=[pltpu.VMEM((d, d_e), BF16), pltpu.VMEM((d, d_e), BF16), pltpu.VMEM((d_e, d), BF16)],
    )
    return pl.pallas_call(
        _experts_kernel,
        grid_spec=grid_spec,
        out_shape=jax.ShapeDtypeStruct((n_slots, d), F32),
        compiler_params=_cparams("arbitrary"),
        name="moe_experts",
    )(tile_expert, n_valid, xs, w1, w3, w2)


def _combine_kernel(pos1_ref, pos2_ref, xmid_ref, rg_ref, gfin_ref, ys_ref, o_ref, y1_ref, y2_ref, sem,
                    *, final_norm):
    tm = xmid_ref.shape[0]
    for t in range(tm):
        _row_copy(ys_ref, pos1_ref[t], y1_ref, t, sem).start(priority=0)
        _row_copy(ys_ref, pos2_ref[t], y2_ref, t, sem).start(priority=1)
    for _ in range(2 * tm):
        _row_copy(ys_ref, 0, y1_ref, 0, sem).wait()
    rg = rg_ref[...]
    out = xmid_ref[...] + (rg[:, 0:1] * y1_ref[...] + rg[:, 1:2] * y2_ref[...])
    o_ref[...] = _rms(out, gfin_ref[...]) if final_norm else out


def _combine(xmid, rg, pos1, pos2, row0, ys, g_final, final_norm):
    n, d = xmid.shape
    tm = ROUTE_TILE
    t0 = row0 // tm
    smem_rows = lambda: pl.BlockSpec((tm,), lambda i: (t0 + i,), memory_space=pltpu.SMEM)
    kern = functools.partial(_combine_kernel, final_norm=final_norm)
    return pl.pallas_call(
        kern,
        grid=(n // tm,),
        in_specs=[smem_rows(), smem_rows(),
                  pl.BlockSpec((tm, d), lambda i: (i, 0)),
                  pl.BlockSpec((tm, LANES), lambda i: (i, 0)),
                  _resident((1, d), (0, 0)),
                  pl.BlockSpec(memory_space=pl.ANY)],
        out_specs=pl.BlockSpec((tm, d), lambda i: (i, 0)),
        out_shape=jax.ShapeDtypeStruct((n, d), F32),
        scratch_shapes=[pltpu.VMEM((tm, d), F32), pltpu.VMEM((tm, d), F32), pltpu.SemaphoreType.DMA],
        compiler_params=_cparams("arbitrary"),
        name="moe_combine",
    )(pos1, pos2, xmid, rg, g_final, ys)


def _moe_experts(prompt, sample, counts, w1, w3, w2, layer):
    xn_p, rt_p = prompt
    xn_s, rt_s = sample
    n = xn_p.shape[0] + xn_s.shape[0]
    n_slots, offs, tile_expert, n_valid, last_tile, used = _slot_plan(
        counts[:N_EXPERTS, 0].astype(jnp.int32), n)
    rt = jnp.concatenate([rt_p, rt_s], axis=1)
    experts = jnp.arange(N_EXPERTS, dtype=jnp.int32)[:, None]
    base = lambda e: jnp.sum(jnp.where(experts == e[None, :], offs[:, None], 0), axis=0)
    pos1 = base(rt[0]) + rt[2]
    pos2 = base(rt[1]) + rt[3]
    xs = _dispatch(xn_p, xn_s, pos1, pos2, n_slots, last_tile, used, n_valid)
    return _experts(xs, tile_expert, n_valid, w1, w3, w2, layer), pos1, pos2


def kernel(x_prompt, x_sample, state_lru_h, state_lru_conv, state_conv, state_ret, norm_mix_g, norm_ffn_g, final_norm_g, w_in, lru_conv_w, lru_conv_b, lru_wa, lru_ba, lru_wx, lru_bx, lru_lambda, cm_dw_w, cm_dw_b, cm_ln_g, cm_ln_b, ret_norm_g, w_out, moe_w_group, moe_b_group, moe_w_expert, moe_b_expert, moe_w1, moe_w3, moe_w2):
    bp, tp, d = x_prompt.shape
    bs, ts, _ = x_sample.shape
    depth = w_in.shape[0]
    w_a = lru_lambda.shape[1]
    w_b = cm_dw_b.shape[1]
    w_c = ret_norm_g.shape[1]
    n_p, n_s = bp * tp, bs * ts
    row = lambda v: v.reshape(1, -1)

    x_p, x_s = x_prompt.reshape(n_p, d), x_sample.reshape(n_s, d)
    outs = {k: [] for k in ("p_h", "p_ca", "p_cb", "p_s", "s_h", "s_ca", "s_cb")}
    s_states = None
    combine_p = None
    for l in range(depth):
        w_route, b_route = _route_weights(moe_w_group[l], moe_b_group[l], moe_w_expert[l], moe_b_expert[l])
        lw = dict(
            w_a=w_a, w_b=w_b, w_c=w_c, g_mix=row(norm_mix_g[l]), g_ffn=row(norm_ffn_g[l]),
            conv_a_w=lru_conv_w[l], conv_a_b=row(lru_conv_b[l]),
            wg=_blockdiag_gate_weights(lru_wa[l], lru_wx[l]),
            bg=row(jnp.concatenate([lru_ba[l], lru_bx[l]])), lam=row(lru_lambda[l]),
            conv_b_w=cm_dw_w[l], conv_b_b=row(cm_dw_b[l]), ln_g=row(cm_ln_g[l]), ln_b=row(cm_ln_b[l]),
            ret_g=row(ret_norm_g[l]), w_route=w_route, b_route=b_route,
        )
        parts = 1
        xmid_p, xn_p, rt_p, rg_p, seen, h_p, ca_p, cb_p, s_p = _prompt_layer(
            x_p, bp, tp, l, w_in, w_out, lw, parts, combine_p)
        outs["p_h"].append(h_p[:, 0, :])
        outs["p_ca"].append(ca_p[:, SUBLANES - (CONV_A - 1):, :])
        outs["p_cb"].append(cb_p[:, 4 * SUBLANES - (CONV_B - 1):, :])
        outs["p_s"].append(_unpair_state(s_p))
        proj_s = _in_proj(x_s, lw["g_mix"], w_in, l, parts)
        p_t = proj_s[:, :2 * w_a + 2 * w_b].reshape(bs, ts, -1).transpose(1, 0, 2)
        oab_t, h_s, ca_t, cb_t = _mix_sample_ab(
            p_t, state_lru_h, state_lru_conv[l].transpose(1, 0, 2), state_conv[l].transpose(1, 0, 2), l, lw,
            parts)
        oab = oab_t.transpose(1, 0, 2).reshape(n_s, w_a + w_b)
        mix_s, s_states = _mix_sample_c(proj_s, bs, ts, oab, state_ret, l, s_states, lw, parts)
        outs["s_h"].append(h_s)
        outs["s_ca"].append(ca_t.transpose(1, 0, 2))
        outs["s_cb"].append(cb_t.transpose(1, 0, 2))
        xmid_s, xn_s, rt_s, rg_s, counts = _out_proj(x_s, mix_s, w_out, l, seen, lw, parts)
        ys, pos1, pos2 = _moe_experts((xn_p, rt_p), (xn_s, rt_s), counts, moe_w1, moe_w3, moe_w2, l)
        last = l == depth - 1
        x_s = _combine(xmid_s, rg_s, pos1, pos2, n_p, ys, row(final_norm_g), last)
        if last:
            x_p = _combine(xmid_p, rg_p, pos1, pos2, 0, ys, row(final_norm_g), last)
        else:
            x_p, combine_p = xmid_p, (rg_p, pos1, pos2, ys)
    st = lambda k: jnp.stack(outs[k])
    return (x_p.reshape(bp, tp, d), x_s.reshape(bs, ts, d), st("p_h"), st("p_ca"), st("p_cb"), st("p_s"),
            st("s_h"), st("s_ca"), st("s_cb"), s_states)
```

```python
import functools
import math

import jax
import jax.numpy as jnp
from jax import lax
from jax.experimental import pallas as pl
from jax.experimental.pallas import tpu as pltpu

PAST_LEN = 16384
LRU_C = 8.0
CONV_A = 4
CONV_B = 31
NH_C = 8
RET_CHUNK = 128
ROPE_BASE = 10000.0
N_GROUPS = 4
EXPERTS_PER_GROUP = 8
N_EXPERTS = N_GROUPS * EXPERTS_PER_GROUP
EPS = 1e-6

LANES = 128
SUBLANES = 8
VMEM_LIMIT_BYTES = 56 * 1024 * 1024

F32 = jnp.float32
BF16 = jnp.bfloat16


def _cparams(*sem):
    return pltpu.CompilerParams(dimension_semantics=sem, vmem_limit_bytes=VMEM_LIMIT_BYTES)


def _row_tile(n):
    for t in (512, 256, 128, 64, 32, 16, 8):
        if n % t == 0:
            return t
    raise ValueError(f"row count {n} must be a multiple of 8")


def _resident(shape, index):
    return pl.BlockSpec(shape, lambda *_: index, pipeline_mode=pl.Buffered(1))


def _rms(x, g):
    ms = jnp.mean(x * x, axis=-1, keepdims=True)
    return x * lax.rsqrt(ms + EPS) * g


def _sigmoid(x):
    return jax.nn.sigmoid(x)


def _silu(x):
    return x * jax.nn.sigmoid(x)


def _layernorm(x, g, b):
    mu = jnp.mean(x, axis=-1, keepdims=True)
    xc = x - mu
    var = jnp.mean(xc * xc, axis=-1, keepdims=True)
    return xc * lax.rsqrt(var + EPS) * g + b


def _rope_pair(x, cos, sin_signed, first_half):
    partner = jnp.where(first_half, pltpu.roll(x, 96, 1), pltpu.roll(x, 32, 1))
    return x * cos + partner * sin_signed


def _head_norm_pair(o, head0, inv_d):
    zero = jnp.zeros_like(o)
    s0 = jnp.sum(jnp.where(head0, o, zero), axis=-1, keepdims=True)
    s1 = jnp.sum(jnp.where(head0, zero, o), axis=-1, keepdims=True)
    oc = o - jnp.where(head0, s0, s1) * inv_d
    q = oc * oc
    v0 = jnp.sum(jnp.where(head0, q, zero), axis=-1, keepdims=True)
    v1 = jnp.sum(jnp.where(head0, zero, q), axis=-1, keepdims=True)
    return oc * lax.rsqrt(jnp.where(head0, v0, v1) * inv_d + EPS)


def _mm(a, b, dims, parts):
    dot = lambda x, y: lax.dot_general(x, y, (dims, ((), ())), preferred_element_type=F32)
    ah, bh = a.astype(BF16), b.astype(BF16)
    if parts == 1:
        return dot(ah, bh)
    al = (a - ah.astype(F32)).astype(BF16)
    bl = (b - bh.astype(F32)).astype(BF16)
    return dot(ah, bh) + (dot(al, bh) + dot(ah, bl))


ROWS_BY_COLS = ((1,), (0,))
ROWS_BY_ROWS = ((1,), (1,))
COLS_BY_COLS = ((0,), (0,))


def _retention_intra_pair(q, k, v, dmat0, dmat1, head0, parts):
    outs = []
    for j, dm in ((0, dmat0), (1, dmat1)):
        keep = head0 if j == 0 else jnp.logical_not(head0)
        s = _mm(jnp.where(keep, q, 0.0), k, ROWS_BY_ROWS, parts)
        outs.append(_mm(s * dm, v, ROWS_BY_COLS, parts))
    return jnp.where(head0, outs[0], outs[1])


def _lru_gates(xa, wg, bg, lam, parts):
    w_a = xa.shape[-1]
    gates = _mm(xa, wg, ROWS_BY_COLS, parts) + bg
    r = _sigmoid(gates[:, :w_a])
    i = _sigmoid(gates[:, w_a:])
    z = -lam
    softplus = jnp.maximum(z, 0.0) + jnp.log1p(jnp.exp(-jnp.abs(z)))
    log_a = -LRU_C * r * softplus
    a = jnp.exp(log_a)
    bt = jnp.sqrt(-jnp.tanh(log_a) * (a * a + 1.0)) * (i * xa)
    return a, bt


ROUTE_TILE = 512
EXPERT_ROW0 = SUBLANES
assert EXPERTS_PER_GROUP == SUBLANES and N_GROUPS <= SUBLANES and EXPERT_ROW0 + N_EXPERTS <= LANES


def _split_bf16(x):
    hi = x.astype(BF16)
    return hi, (x - hi.astype(F32)).astype(BF16)


def _store_parts(ref, idx, val):
    hi = val.astype(BF16)
    ref[(0,) + idx] = hi
    if ref.shape[0] == 2:
        ref[(1,) + idx] = (val - hi.astype(F32)).astype(BF16)


def _dot_parts(a_ref, a_idx, w_ref, w_idx):
    dot = lambda i, j: jnp.dot(a_ref[(i,) + a_idx], w_ref[(j,) + w_idx], preferred_element_type=F32)
    if a_ref.shape[0] == 1:
        return dot(0, 0)
    return dot(0, 0) + (dot(1, 0) + dot(0, 1))


def _lane_tile(x, width):
    return x if width == x.shape[1] else jnp.concatenate([x] * (width // x.shape[1]), axis=1)


def _bf16_parts(w):
    hi = lax.reduce_precision(w, exponent_bits=8, mantissa_bits=7)
    return hi.astype(BF16), (w - hi).astype(BF16)


def _weight_parts(w, parts):
    if parts == 1:
        return w.astype(BF16)[None]
    return jnp.stack(_bf16_parts(w))


def _route_weights(w_group, b_group, w_expert, b_expert):
    d = w_group.shape[0]
    w = jnp.zeros((LANES, d), F32).at[:N_GROUPS].set(w_group.T).at[EXPERT_ROW0:EXPERT_ROW0 + N_EXPERTS].set(w_expert.T)
    b = jnp.zeros((LANES,), F32).at[:N_GROUPS].set(b_group).at[EXPERT_ROW0:EXPERT_ROW0 + N_EXPERTS].set(b_expert)
    return w, jnp.broadcast_to(b[:, None], (LANES, LANES))


def _ffn_norm_route(xm, g_ffn, w_route, b_route, carry_ref, xn_ref, rt_ref, rg_ref, counts_ref):
    xn = _rms(xm, g_ffn)
    xn_ref[...] = xn
    tm = xn.shape[0]
    xh, xl = _split_bf16(xn)
    wr_hi, wr_lo = _split_bf16(w_route)
    dot_t = lambda w, x: lax.dot_general(w, x, (((1,), (1,)), ((), ())), preferred_element_type=F32)
    lt = dot_t(wr_hi, xh) + (dot_t(wr_hi, xl) + dot_t(wr_lo, xh)) + _lane_tile(b_route, tm)
    row8 = lax.broadcasted_iota(jnp.int32, (SUBLANES, tm), 0)
    neg = jnp.full((SUBLANES, tm), -jnp.inf, F32)
    big = jnp.full((SUBLANES, tm), LANES, jnp.int32)
    first = lambda hit: jnp.min(jnp.where(hit, row8, big), axis=0, keepdims=True)
    gl = jnp.where(row8 < N_GROUPS, lt[0:SUBLANES, :], neg)
    gmax = jnp.max(gl, axis=0, keepdims=True)
    p_top = 1.0 / jnp.sum(jnp.exp(gl - gmax), axis=0, keepdims=True)
    g_sel = first(gl == gmax)
    slab = lambda g: lt[EXPERT_ROW0 + g * SUBLANES:EXPERT_ROW0 + (g + 1) * SUBLANES, :]
    el = slab(N_GROUPS - 1)
    for g in reversed(range(N_GROUPS - 1)):
        el = jnp.where(g_sel == g, slab(g), el)
    v1 = jnp.max(el, axis=0, keepdims=True)
    i1 = first(el == v1)
    el2 = jnp.where(row8 == i1, neg, el)
    v2 = jnp.max(el2, axis=0, keepdims=True)
    i2 = first(el2 == v2)
    ex = jnp.exp(v2 - v1)
    den = 1.0 + ex
    g1 = p_top * (1.0 / den)
    g2 = p_top * (ex / den)
    e1 = g_sel * EXPERTS_PER_GROUP + i1
    e2 = g_sel * EXPERTS_PER_GROUP + i2
    rowe = lax.broadcasted_iota(jnp.int32, (LANES, tm), 0)
    oh1 = rowe == e1
    oh2 = rowe == e2
    onehot = jnp.logical_or(oh1, oh2).astype(BF16)
    earlier = (lax.broadcasted_iota(jnp.int32, (tm, tm), 0)
               < lax.broadcasted_iota(jnp.int32, (tm, tm), 1)).astype(BF16)
    seen = _lane_tile(carry_ref[...], tm) + jnp.dot(onehot, earlier, preferred_element_type=F32)
    rank1 = jnp.sum(jnp.where(oh1, seen, 0.0), axis=0, keepdims=True).astype(jnp.int32)
    rank2 = jnp.sum(jnp.where(oh2, seen, 0.0), axis=0, keepdims=True).astype(jnp.int32)
    total = carry_ref[...] + jnp.sum(onehot.astype(F32), axis=1, keepdims=True)
    carry_ref[...] = total
    counts_ref[...] = total
    zi = jnp.zeros((SUBLANES, tm), jnp.int32)
    rt_ref[...] = jnp.where(row8 == 0, e1, jnp.where(row8 == 1, e2, jnp.where(
        row8 == 2, rank1, jnp.where(row8 == 3, rank2, zi))))
    rg_ref[...] = jnp.where(rowe == 0, g1, jnp.where(rowe == 1, g2, 0.0)).T


CONV_B_ROWS = 64


def _row_copy(src_ref, src_row, dst_ref, dst_row, sem):
    return pltpu.make_async_copy(src_ref.at[pl.ds(src_row, 1), :], dst_ref.at[pl.ds(dst_row, 1), :], sem)


def _prompt_layer_kernel(*refs, tc, w_a, w_b, w_c, chunk, fused_combine):
    if fused_combine:
        (p1c_ref, p2c_ref, p1n_ref, p2n_ref, rgp_ref, ys_ref), refs = refs[:6], refs[6:]
        (y_ref, ysem), refs = refs[-2:], refs[:-2]
    (x_ref, gmix_ref, winb_ref, cos_ref, sin_ref, h0_ref, bufa0_ref, bufb0_ref, s0_ref,
     caw_ref, cab_ref, wg_ref, bg_ref, lam_ref,
     cbw_ref, cbb_ref, lng_ref, lnb_ref, rng_ref,
     dmat_ref, xi_ref, zeta_ref, gmat_ref,
     woutb_ref, gffn_ref, wr_ref, br_ref,
     xmid_ref, xn_ref, rt_ref, rg_ref, counts_ref,
     hout_ref, bufa_out_ref, bufb_out_ref, sout_ref,
     xin_ref, u_ref, proj_ref, mix_ref,
     xpa_ref, xpb_ref, sh_ref, a_ref, b_ref, hseq_ref, hcar_ref, s_ref, carry_ref) = refs
    t = pl.program_id(1)
    n_t = pl.num_programs(1)
    pa = SUBLANES
    pb = 4 * SUBLANES
    parts = winb_ref.shape[0]
    everything = (slice(None), slice(None))

    if fused_combine:
        step = pl.program_id(0) * n_t + t
        slot = step % 2

        def gather(p1_ref, p2_ref, into):
            for r in range(tc):
                _row_copy(ys_ref, p1_ref[r], y_ref.at[into, 0], r, ysem.at[into]).start(priority=0)
                _row_copy(ys_ref, p2_ref[r], y_ref.at[into, 1], r, ysem.at[into]).start(priority=1)

        @pl.when(step == 0)
        def _():
            gather(p1c_ref, p2c_ref, 0)

        for _ in range(2 * tc):
            _row_copy(ys_ref, 0, y_ref.at[slot, 0], 0, ysem.at[slot]).wait()
        gates = rgp_ref[...]
        xin_ref[...] = x_ref[...] + (gates[:, 0:1] * y_ref[slot, 0] + gates[:, 1:2] * y_ref[slot, 1])
    else:
        xin_ref[...] = x_ref[...]

    @pl.when(jnp.logical_and(pl.program_id(0) == 0, t == 0))
    def _():
        carry_ref[...] = jnp.zeros_like(carry_ref)

    @pl.when(t == 0)
    def _():
        xpa_ref[0:pa, :] = bufa0_ref[0]
        xpb_ref[0:pb, :] = bufb0_ref[0]
        hcar_ref[...] = h0_ref[0]
        s_ref[...] = s0_ref[0]

    _store_parts(u_ref, everything, _rms(xin_ref[...], gmix_ref[...]))

    def in_proj(c0, c1):
        for c in range(c0, c1, 4 * LANES):
            cols = (slice(None), slice(c, min(c + 4 * LANES, c1)))
            proj_ref[cols] = _dot_parts(u_ref, everything, winb_ref, cols)

    in_proj(0, 2 * w_a)

    xpa_ref[pa:pa + tc, :] = proj_ref[:, w_a:2 * w_a]
    xa = cab_ref[...] + caw_ref[0:1, :] * xpa_ref[pa - 3:pa - 3 + tc, :]
    for j in range(1, CONV_A):
        xa = xa + caw_ref[j:j + 1, :] * xpa_ref[pa - 3 + j:pa - 3 + j + tc, :]
    a, bt = _lru_gates(xa, wg_ref[...], bg_ref[...], lam_ref[...], parts)
    a_ref[...] = a
    b_ref[...] = bt
    row8 = lax.broadcasted_iota(jnp.int32, (SUBLANES, w_a), 0)

    def scan_block(i, hprev):
        r0 = pl.multiple_of(i * SUBLANES, SUBLANES)
        ab = a_ref[pl.ds(r0, SUBLANES), :]
        bb = b_ref[pl.ds(r0, SUBLANES), :]
        for k in (1, 2, 4):
            a_sh = jnp.where(row8 >= k, pltpu.roll(ab, k, 0), 1.0)
            b_sh = jnp.where(row8 >= k, pltpu.roll(bb, k, 0), 0.0)
            bb = ab * b_sh + bb
            ab = ab * a_sh
        h = ab * hprev + bb
        hseq_ref[pl.ds(r0, SUBLANES), :] = h
        return jnp.broadcast_to(h[SUBLANES - 1:SUBLANES, :], (SUBLANES, w_a))

    hcar_ref[...] = lax.fori_loop(0, tc // SUBLANES, scan_block, hcar_ref[...])
    if fused_combine:
        gather(p1n_ref, p2n_ref, 1 - slot)
    in_proj(2 * w_a, proj_ref.shape[1])
    _store_parts(mix_ref, (slice(None), slice(0, w_a)), jax.nn.gelu(proj_ref[:, 0:w_a]) * hseq_ref[...])
    xpa_ref[0:pa, :] = xpa_ref[tc:tc + pa, :]

    b0 = 2 * w_a
    xpb_ref[pb:pb + tc, :] = proj_ref[:, b0:b0 + w_b] * _sigmoid(proj_ref[:, b0 + w_b:b0 + 2 * w_b])
    first = pb - (CONV_B - 1)
    for r in range(SUBLANES):
        span = tc + (CONV_B - 1 - r) // SUBLANES * SUBLANES
        sh_ref[r, 0:span, :] = xpb_ref[first + r:first + r + span, :]
    for blk in range(tc // CONV_B_ROWS):
        r0 = blk * CONV_B_ROWS
        acc = jnp.broadcast_to(cbb_ref[...], (CONV_B_ROWS, w_b))
        for j in range(CONV_B):
            r = j % SUBLANES
            acc = acc + cbw_ref[j:j + 1, :] * sh_ref[r, r0 + j - r:r0 + j - r + CONV_B_ROWS, :]
        out_b = _silu(_layernorm(acc, lng_ref[...], lnb_ref[...]))
        _store_parts(mix_ref, (slice(r0, r0 + CONV_B_ROWS), slice(w_a, w_a + w_b)), out_b)
    xpb_ref[0:pb, :] = xpb_ref[tc:tc + pb, :]

    c0 = 2 * w_a + 2 * w_b
    n_pair = w_c // LANES
    lane = lax.broadcasted_iota(jnp.int32, (chunk, LANES), 1)
    head0 = lane < (LANES // 2)
    first_half = (lane % (LANES // 2)) < (LANES // 4)
    rl = lax.broadcasted_iota(jnp.int32, (LANES, LANES), 0) < (LANES // 2)
    cl = lax.broadcasted_iota(jnp.int32, (LANES, LANES), 1) < (LANES // 2)
    same_head = rl == cl
    inv_d = 2.0 / LANES
    for c in range(tc // chunk):
        rows = slice(c * chunk, (c + 1) * chunk)
        cos = cos_ref[rows, :]
        sin = sin_ref[rows, :]
        for p in range(n_pair):
            cq = c0 + p * LANES
            q = _rope_pair(proj_ref[rows, cq:cq + LANES], cos, sin, first_half)
            k = _rope_pair(proj_ref[rows, cq + w_c:cq + w_c + LANES], cos, sin, first_half) * (2.0 / LANES) ** 0.5
            v = proj_ref[rows, cq + 2 * w_c:cq + 2 * w_c + LANES]
            g = proj_ref[rows, cq + 3 * w_c:cq + 3 * w_c + LANES]
            o = _retention_intra_pair(q, k, v, dmat_ref[2 * p], dmat_ref[2 * p + 1], head0, parts)
            s_old = s_ref[p]
            o = o + _mm(q, s_old, ROWS_BY_COLS, parts) * xi_ref[p]
            upd = _mm(k * zeta_ref[p], v, COLS_BY_COLS, parts)
            s_ref[p] = gmat_ref[p] * s_old + jnp.where(same_head, upd, 0.0)
            on = _head_norm_pair(o, head0, inv_d) * rng_ref[:, p * LANES:(p + 1) * LANES]
            _store_parts(mix_ref, (rows, slice(w_a + w_b + p * LANES, w_a + w_b + (p + 1) * LANES)), _silu(g) * on)

    xm = xin_ref[...] + _dot_parts(mix_ref, everything, woutb_ref, everything)
    xmid_ref[...] = xm
    _ffn_norm_route(xm, gffn_ref[...], wr_ref[...], br_ref[...], carry_ref,
                    xn_ref, rt_ref, rg_ref, counts_ref)

    @pl.when(t == n_t - 1)
    def _():
        hout_ref[0] = hcar_ref[...]
        bufa_out_ref[0] = xpa_ref[0:pa, :]
        bufb_out_ref[0] = xpb_ref[0:pb, :]
        sout_ref[0] = s_ref[...]

    if fused_combine:
        @pl.when(step == pl.num_programs(0) * n_t - 1)
        def _():
            for _ in range(2 * tc):
                _row_copy(ys_ref, 0, y_ref.at[1 - slot, 0], 0, ysem.at[1 - slot]).wait()


def _retention_tables(chunk):
    log_g = jnp.log1p(-jnp.exp2(-5.0 - jnp.arange(NH_C, dtype=F32)))
    idx = jnp.arange(chunk, dtype=F32)
    diff = idx[:, None] - idx[None, :]
    causal = diff >= 0
    dmat = jnp.where(causal[None], jnp.exp(jnp.where(causal, diff, 0.0)[None] * log_g[:, None, None]), 0.0)
    xi = jnp.exp((idx + 1.0)[None, :] * log_g[:, None])
    zeta = jnp.exp((chunk - 1.0 - idx)[None, :] * log_g[:, None])
    g_chunk = jnp.exp(chunk * log_g)
    return dmat, xi, zeta, g_chunk


def _pair_lanes(per_head):
    h, rows = per_head.shape
    half = LANES // 2
    x = jnp.broadcast_to(per_head.reshape(h // 2, 2, rows, 1), (h // 2, 2, rows, half))
    return x.transpose(0, 2, 1, 3).reshape(h // 2, rows, LANES)


def _rope_tables(pos):
    half = LANES // 4
    inv = ROPE_BASE ** (-jnp.arange(half, dtype=F32) / half)
    ang = pos.astype(F32)[:, None] * inv[None, :]
    cos, sin = jnp.cos(ang), jnp.sin(ang)
    return jnp.concatenate([cos] * 4, axis=1), jnp.concatenate([-sin, sin, -sin, sin], axis=1)


def _blockdiag_gate_weights(wa, wx):
    nh, hd, _ = wa.shape
    eye = jnp.eye(nh, dtype=wa.dtype)
    da = (eye[:, None, :, None] * wa[:, :, None, :]).reshape(nh * hd, nh * hd)
    dx = (eye[:, None, :, None] * wx[:, :, None, :]).reshape(nh * hd, nh * hd)
    return jnp.concatenate([da, dx], axis=1)


def _prompt_layer(x, n_seq, seq_len, layer, w_in, w_out, lw, parts, combine=None):
    w_a, w_b, w_c = lw["w_a"], lw["w_b"], lw["w_c"]
    n, d = x.shape
    d_in = w_in.shape[2]
    d_mix = w_a + w_b + w_c
    chunk = math.gcd(seq_len, RET_CHUNK)
    tc = ROUTE_TILE
    assert seq_len % tc == 0 and tc % chunk == 0 and tc % CONV_B_ROWS == 0
    n_t = seq_len // tc
    n_pair = w_c // LANES
    dmat, xi, zeta, g_chunk = _retention_tables(chunk)
    half = LANES // 2
    gm = jnp.zeros((n_pair, 2, half, 2, half), F32)
    gm = gm.at[:, 0, :, 0, :].set(g_chunk[0::2, None, None]).at[:, 1, :, 1, :].set(g_chunk[1::2, None, None])
    cos, sin = _rope_tables(jnp.arange(seq_len, dtype=jnp.int32))
    zeros = lambda *s: jnp.zeros(s, F32)
    full = lambda shape: _resident(shape, (0,) * len(shape))
    per_seq = lambda shape: pl.BlockSpec((1,) + shape, lambda b, t: (b,) + (0,) * len(shape))
    tile = lambda width: pl.BlockSpec((tc, width), lambda b, t: (b * n_t + t, 0))
    kern = functools.partial(_prompt_layer_kernel, tc=tc, w_a=w_a, w_b=w_b, w_c=w_c, chunk=chunk,
                             fused_combine=combine is not None)
    extra_in, extra_scratch, extra_args = [], [], []
    if combine is not None:
        rg_prev, pos1, pos2, ys = combine
        last = n // tc - 1
        cur = lambda: pl.BlockSpec((tc,), lambda b, t: (b * n_t + t,), memory_space=pltpu.SMEM)
        nxt = lambda: pl.BlockSpec((tc,), lambda b, t: (jnp.minimum(b * n_t + t + 1, last),),
                                   memory_space=pltpu.SMEM)
        extra_in = [cur(), cur(), nxt(), nxt(), tile(LANES), pl.BlockSpec(memory_space=pl.ANY)]
        extra_args = [pos1, pos2, pos1, pos2, rg_prev, ys]
        extra_scratch = [pltpu.VMEM((2, 2, tc, d), F32), pltpu.SemaphoreType.DMA((2,))]
    return pl.pallas_call(
        kern,
        grid=(n_seq, n_t),
        in_specs=extra_in + [
            tile(d), full((1, d)), full((parts, d, d_in)),
            pl.BlockSpec((tc, LANES), lambda b, t: (t, 0)),
            pl.BlockSpec((tc, LANES), lambda b, t: (t, 0)),
            per_seq((SUBLANES, w_a)), per_seq((SUBLANES, w_a)), per_seq((4 * SUBLANES, w_b)),
            per_seq((n_pair, LANES, LANES)),
            full((CONV_A, w_a)), full((1, w_a)), full((w_a, 2 * w_a)), full((1, 2 * w_a)), full((1, w_a)),
            full((CONV_B, w_b)), full((1, w_b)), full((1, w_b)), full((1, w_b)), full((1, w_c)),
            full((NH_C, chunk, chunk)), full((n_pair, chunk, LANES)), full((n_pair, chunk, LANES)),
            full((n_pair, LANES, LANES)),
            full((parts, d_mix, d)), full((1, d)),
            full((LANES, d)), full((LANES, LANES)),
        ],
        out_specs=[
            tile(d), tile(d),
            pl.BlockSpec((SUBLANES, tc), lambda b, t: (0, b * n_t + t)),
            tile(LANES),
            pl.BlockSpec((LANES, LANES), lambda b, t: (0, 0)),
            per_seq((SUBLANES, w_a)), per_seq((SUBLANES, w_a)), per_seq((4 * SUBLANES, w_b)),
            per_seq((n_pair, LANES, LANES)),
        ],
        out_shape=[
            jax.ShapeDtypeStruct((n, d), F32),
            jax.ShapeDtypeStruct((n, d), F32),
            jax.ShapeDtypeStruct((SUBLANES, n), jnp.int32),
            jax.ShapeDtypeStruct((n, LANES), F32),
            jax.ShapeDtypeStruct((LANES, LANES), F32),
            jax.ShapeDtypeStruct((n_seq, SUBLANES, w_a), F32),
            jax.ShapeDtypeStruct((n_seq, SUBLANES, w_a), F32),
            jax.ShapeDtypeStruct((n_seq, 4 * SUBLANES, w_b), F32),
            jax.ShapeDtypeStruct((n_seq, n_pair, LANES, LANES), F32),
        ],
        scratch_shapes=[
            pltpu.VMEM((tc, d), F32),
            pltpu.VMEM((parts, tc, d), BF16),
            pltpu.VMEM((tc, d_in), F32),
            pltpu.VMEM((parts, tc, d_mix), BF16),
            pltpu.VMEM((SUBLANES + tc, w_a), F32),
            pltpu.VMEM((4 * SUBLANES + tc, w_b), F32),
            pltpu.VMEM((SUBLANES, 3 * SUBLANES + tc, w_b), F32),
            pltpu.VMEM((tc, w_a), F32),
            pltpu.VMEM((tc, w_a), F32),
            pltpu.VMEM((tc, w_a), F32),
            pltpu.VMEM((SUBLANES, w_a), F32),
            pltpu.VMEM((n_pair, LANES, LANES), F32),
            pltpu.VMEM((LANES, LANES), F32),
        ] + extra_scratch,
        compiler_params=_cparams("arbitrary", "arbitrary"),
        name="prompt_layer",
    )(*extra_args, x, lw["g_mix"], _weight_parts(w_in[layer], parts), cos, sin,
      zeros(n_seq, SUBLANES, w_a), zeros(n_seq, SUBLANES, w_a), zeros(n_seq, 4 * SUBLANES, w_b),
      zeros(n_seq, n_pair, LANES, LANES),
      lw["conv_a_w"], lw["conv_a_b"], lw["wg"], lw["bg"], lw["lam"],
      lw["conv_b_w"], lw["conv_b_b"], lw["ln_g"], lw["ln_b"], lw["ret_g"],
      dmat, _pair_lanes(xi), _pair_lanes(zeta), gm.reshape(n_pair, LANES, LANES),
      _weight_parts(w_out[layer], parts), lw["g_ffn"], lw["w_route"], lw["b_route"])


def _unpair_state(s_pairs):
    b, n_pair = s_pairs.shape[:2]
    half = LANES // 2
    s = s_pairs.reshape(b, n_pair, 2, half, 2, half)
    return jnp.stack([s[:, :, 0, :, 0, :], s[:, :, 1, :, 1, :]], axis=2).reshape(b, 2 * n_pair, half, half)


def _in_proj_kernel(x_ref, g_ref, w_ref, o_ref, *, parts):
    o_ref[...] = _mm(_rms(x_ref[...], g_ref[...]), w_ref[0], ROWS_BY_COLS, parts)


def _in_proj(x, g, w_in, layer, parts):
    n, d = x.shape
    d_in = w_in.shape[2]
    tm = _row_tile(n)
    return pl.pallas_call(
        functools.partial(_in_proj_kernel, parts=parts),
        grid=(n // tm,),
        in_specs=[
            pl.BlockSpec((tm, d), lambda i: (i, 0)),
            _resident((1, d), (0, 0)),
            _resident((1, d, d_in), (layer, 0, 0)),
        ],
        out_specs=pl.BlockSpec((tm, d_in), lambda i: (i, 0)),
        out_shape=jax.ShapeDtypeStruct((n, d_in), F32),
        compiler_params=_cparams("arbitrary"),
        name="in_proj_sample",
    )(x, g, w_in)


def _mix_sample_ab_kernel(p_ref, h0_ref, bufa_ref, bufb_ref,
                          caw_ref, cab_ref, wg_ref, bg_ref, lam_ref,
                          cbw_ref, cbb_ref, lng_ref, lnb_ref,
                          o_ref, h_ref, bufa_out_ref, bufb_out_ref, *, ts, w_a, w_b, parts):
    xpa = [bufa_ref[j] for j in range(CONV_A - 1)] + [p_ref[t, :, w_a:2 * w_a] for t in range(ts)]
    xa = []
    for t in range(ts):
        acc = cab_ref[...] + caw_ref[0:1, :] * xpa[t]
        for j in range(1, CONV_A):
            acc = acc + caw_ref[j:j + 1, :] * xpa[t + j]
        xa.append(acc)
    nb = xa[0].shape[0]
    a, bt = _lru_gates(jnp.concatenate(xa, axis=0), wg_ref[...], bg_ref[...], lam_ref[...], parts)
    h = h0_ref[...]
    for t in range(ts):
        h = a[t * nb:(t + 1) * nb, :] * h + bt[t * nb:(t + 1) * nb, :]
        o_ref[t, :, 0:w_a] = jax.nn.gelu(p_ref[t, :, 0:w_a]) * h
    h_ref[...] = h
    for j in range(CONV_A - 1):
        bufa_out_ref[j] = xpa[ts + j]
    b0 = 2 * w_a
    hist = CONV_B - 1
    glu = [p_ref[t, :, b0:b0 + w_b] * _sigmoid(p_ref[t, :, b0 + w_b:b0 + 2 * w_b]) for t in range(ts)]
    xpb = lambda i: bufb_ref[i] if i < hist else glu[i - hist]
    for t in range(ts):
        acc = cbb_ref[...] + cbw_ref[0:1, :] * xpb(t)
        for j in range(1, CONV_B):
            acc = acc + cbw_ref[j:j + 1, :] * xpb(t + j)
        o_ref[t, :, w_a:w_a + w_b] = _silu(_layernorm(acc, lng_ref[...], lnb_ref[...]))
    for i in range(hist):
        bufb_out_ref[i] = xpb(ts + i)


def _mix_sample_ab(p_t, h0, bufa_t, bufb_t, layer, lw, parts):
    ts, n_seq, width = p_t.shape
    w_a, w_b = lw["w_a"], lw["w_b"]
    nb = 32 if n_seq % 32 == 0 else n_seq
    full = lambda shape: _resident(shape, (0,) * len(shape))
    kern = functools.partial(_mix_sample_ab_kernel, ts=ts, w_a=w_a, w_b=w_b, parts=parts)
    return pl.pallas_call(
        kern,
        grid=(n_seq // nb,),
        in_specs=[
            pl.BlockSpec((ts, nb, width), lambda i: (0, i, 0)),
            pl.BlockSpec((None, nb, w_a), lambda i: (layer, i, 0)),
            pl.BlockSpec((CONV_A - 1, nb, w_a), lambda i: (0, i, 0)),
            pl.BlockSpec((CONV_B - 1, nb, w_b), lambda i: (0, i, 0)),
            full((CONV_A, w_a)), full((1, w_a)), full((w_a, 2 * w_a)), full((1, 2 * w_a)), full((1, w_a)),
            full((CONV_B, w_b)), full((1, w_b)), full((1, w_b)), full((1, w_b)),
        ],
        out_specs=[
            pl.BlockSpec((ts, nb, w_a + w_b), lambda i: (0, i, 0)),
            pl.BlockSpec((nb, w_a), lambda i: (i, 0)),
            pl.BlockSpec((CONV_A - 1, nb, w_a), lambda i: (0, i, 0)),
            pl.BlockSpec((CONV_B - 1, nb, w_b), lambda i: (0, i, 0)),
        ],
        out_shape=[
            jax.ShapeDtypeStruct((ts, n_seq, w_a + w_b), F32),
            jax.ShapeDtypeStruct((n_seq, w_a), F32),
            jax.ShapeDtypeStruct((CONV_A - 1, n_seq, w_a), F32),
            jax.ShapeDtypeStruct((CONV_B - 1, n_seq, w_b), F32),
        ],
        compiler_params=_cparams("arbitrary"),
        name="mix_sample_ab",
    )(p_t, h0, bufa_t, bufb_t,
      lw["conv_a_w"], lw["conv_a_b"], lw["wg"], lw["bg"], lw["lam"],
      lw["conv_b_w"], lw["conv_b_b"], lw["ln_g"], lw["ln_b"])


def _mix_sample_c_kernel(q_ref, k_ref, v_ref, g_ref, oab_ref, s_ref, cos_ref, sin_ref,
                         dmat_ref, xi_ref, zeta_ref, gcol_ref, rng_ref, tile_ref, fold_ref, *rest,
                         rows, ts, w_c, w_ab, layer, parts):
    if layer:
        prev_ref, mix_ref, sout_ref, qs_ref, kzs_ref, vs_ref, oi_ref = rest
        for l in range(layer):
            sout_ref[l] = prev_ref[l]
    else:
        mix_ref, sout_ref, qs_ref, kzs_ref, vs_ref, oi_ref = rest
    n_pair = w_c // LANES
    hd = LANES // 2
    lane = lax.broadcasted_iota(jnp.int32, (rows, LANES), 1)
    head0 = lane < hd
    first_half = (lane % hd) < (hd // 2)
    cos = cos_ref[...]
    sin = sin_ref[...]
    for p in range(n_pair):
        cs = slice(p * LANES, (p + 1) * LANES)
        q = _rope_pair(q_ref[:, cs], cos, sin, first_half)
        k = _rope_pair(k_ref[:, cs], cos, sin, first_half) * (1.0 / hd) ** 0.5
        v = v_ref[:, cs]
        oi_ref[:, cs] = _retention_intra_pair(q, k, v, dmat_ref[2 * p], dmat_ref[2 * p + 1], head0, parts)
        qs_ref[:, cs] = q
        kzs_ref[:, cs] = k * zeta_ref[:, cs]
        vs_ref[:, cs] = v
    grp = SUBLANES
    per_grp = grp // ts
    stack = NH_C * grp
    srow = lax.broadcasted_iota(jnp.int32, (stack, w_c), 0)
    slane = lax.broadcasted_iota(jnp.int32, (stack, w_c), 1)
    own_head = (srow // grp) == (slane // hd)
    tile_m = tile_ref[...]
    fold_m = fold_ref[...]
    for gi in range(rows // grp):
        rs = slice(gi * grp, (gi + 1) * grp)
        q8 = jnp.concatenate([qs_ref[rs, :]] * NH_C, axis=0)
        kz8 = jnp.concatenate([kzs_ref[rs, :]] * NH_C, axis=0)
        v8 = jnp.concatenate([vs_ref[rs, :]] * NH_C, axis=0)
        rest = jnp.where(own_head, v8, 0.0)
        vr = jnp.zeros((stack, hd), F32)
        for _ in range(3 if parts == 2 else 1):
            hi = rest.astype(BF16)
            vr = vr + jnp.dot(hi, fold_m, preferred_element_type=F32)
            rest = rest - hi.astype(F32)
        o_stack = jnp.zeros((stack, hd), F32)
        for s2 in range(per_grp):
            sq = gi * per_grp + s2
            own_seq = ((srow % grp) // ts) == s2
            sel = jnp.logical_and(own_head, own_seq)
            s_old = s_ref[0, sq].reshape(w_c, hd)
            o_stack = o_stack + _mm(jnp.where(sel, q8, 0.0), s_old, ROWS_BY_COLS, parts)
            upd = _mm(jnp.where(sel, kz8, 0.0), vr, COLS_BY_COLS, parts)
            sout_ref[layer, sq] = (gcol_ref[...] * s_old + upd).reshape(NH_C, hd, hd)
        p1 = o_stack.astype(BF16)
        r1 = o_stack - p1.astype(F32)
        p2 = r1.astype(BF16)
        p3 = (r1 - p2.astype(F32)).astype(BF16)
        spread = (jnp.dot(p1, tile_m, preferred_element_type=F32)
                  + jnp.dot(p2, tile_m, preferred_element_type=F32)
                  + jnp.dot(p3, tile_m, preferred_element_type=F32))
        spread = jnp.where(own_head, spread, 0.0).reshape(NH_C, grp, w_c)
        oi_ref[rs, :] = oi_ref[rs, :] + jnp.sum(spread, axis=0) * xi_ref[rs, :]
    mix_ref[:, 0:w_ab] = oab_ref[...]
    for p in range(n_pair):
        cs = slice(p * LANES, (p + 1) * LANES)
        on = _head_norm_pair(oi_ref[:, cs], head0, 1.0 / hd) * rng_ref[:, cs]
        mix_ref[:, w_ab + p * LANES:w_ab + (p + 1) * LANES] = _silu(g_ref[:, cs]) * on


def _mix_sample_c(proj, n_seq, ts, oab, state, layer, prev_states, lw, parts):
    w_a, w_b, w_c = lw["w_a"], lw["w_b"], lw["w_c"]
    hd = LANES // 2
    n_rows = n_seq * ts
    rows = 64 if n_rows % 64 == 0 else n_rows
    assert SUBLANES % ts == 0 and rows % SUBLANES == 0
    seq_per = rows // ts
    c0 = 2 * w_a + 2 * w_b
    assert c0 % w_c == 0
    cb = c0 // w_c
    log_g = jnp.log1p(-jnp.exp2(-5.0 - jnp.arange(NH_C, dtype=F32)))
    r = jnp.arange(rows)
    step = (r % ts).astype(F32)
    diff = step[:, None] - step[None, :]
    live = jnp.logical_and((r[:, None] // ts) == (r[None, :] // ts), diff >= 0)
    dmat = jnp.where(live[None], jnp.exp(jnp.where(live, diff, 0.0)[None] * log_g[:, None, None]), 0.0)
    xi = jnp.repeat(jnp.exp((step + 1.0)[None, :] * log_g[:, None]).T, hd, axis=1)
    zeta = jnp.repeat(jnp.exp((ts - 1.0 - step)[None, :] * log_g[:, None]).T, hd, axis=1)
    gcol = jnp.broadcast_to(jnp.repeat(jnp.exp(ts * log_g), hd)[:, None], (w_c, hd))
    pos = PAST_LEN + (r % ts).astype(jnp.int32)
    cos, sin = _rope_tables(pos)
    tile_m = jnp.tile(jnp.eye(hd, dtype=BF16), (1, NH_C))
    fold_m = jnp.tile(jnp.eye(hd, dtype=BF16), (NH_C, 1))
    full = lambda shape: _resident(shape, (0,) * len(shape))
    col = lambda j: pl.BlockSpec((rows, w_c), lambda i: (i, cb + j))
    states = lambda n_layers, first: pl.BlockSpec((n_layers, seq_per, NH_C, hd, hd), lambda i: (first, i, 0, 0, 0))
    kern = functools.partial(_mix_sample_c_kernel, rows=rows, ts=ts, w_c=w_c, w_ab=w_a + w_b, layer=layer,
                             parts=parts)
    in_specs = [
        col(0), col(1), col(2), col(3),
        pl.BlockSpec((rows, w_a + w_b), lambda i: (i, 0)),
        states(1, layer),
        full((rows, LANES)), full((rows, LANES)),
        full((NH_C, rows, rows)), full((rows, w_c)), full((rows, w_c)), full((w_c, hd)), full((1, w_c)),
        full((hd, w_c)), full((w_c, hd)),
    ]
    args = [proj, proj, proj, proj, oab, state, cos, sin, dmat, xi, zeta, gcol, lw["ret_g"], tile_m, fold_m]
    if layer:
        in_specs.append(states(layer, 0))
        args.append(prev_states)
    return pl.pallas_call(
        kern,
        grid=(n_rows // rows,),
        in_specs=in_specs,
        out_specs=[
            pl.BlockSpec((rows, w_a + w_b + w_c), lambda i: (i, 0)),
            states(layer + 1, 0),
        ],
        out_shape=[
            jax.ShapeDtypeStruct((n_rows, w_a + w_b + w_c), F32),
            jax.ShapeDtypeStruct((layer + 1,) + state.shape[1:], F32),
        ],
        scratch_shapes=[pltpu.VMEM((rows, w_c), F32)] * 4,
        compiler_params=_cparams("arbitrary"),
        name="mix_sample_c",
    )(*args)


def _out_proj_kernel(x_ref, mix_ref, w_ref, g_ref, wr_ref, br_ref, seen_ref,
                     xmid_ref, xn_ref, rt_ref, rg_ref, counts_ref, carry_ref, *, parts):
    @pl.when(pl.program_id(0) == 0)
    def _():
        carry_ref[...] = seen_ref[...]

    xm = x_ref[...] + _mm(mix_ref[...], w_ref[0], ROWS_BY_COLS, parts)
    xmid_ref[...] = xm
    _ffn_norm_route(xm, g_ref[...], wr_ref[...], br_ref[...], carry_ref,
                    xn_ref, rt_ref, rg_ref, counts_ref)


def _out_proj(x, mix, w_out, layer, seen, lw, parts):
    n, d = x.shape
    d_mix = mix.shape[1]
    tm = ROUTE_TILE
    assert n % tm == 0
    full = lambda shape: _resident(shape, (0,) * len(shape))
    rows = lambda width: pl.BlockSpec((tm, width), lambda i: (i, 0))
    return pl.pallas_call(
        functools.partial(_out_proj_kernel, parts=parts),
        grid=(n // tm,),
        in_specs=[rows(d), rows(d_mix), _resident((1, d_mix, d), (layer, 0, 0)),
                  full((1, d)), full((LANES, d)), full((LANES, LANES)), full((LANES, LANES))],
        out_specs=[rows(d), rows(d), pl.BlockSpec((SUBLANES, tm), lambda i: (0, i)), rows(LANES),
                   pl.BlockSpec((LANES, LANES), lambda i: (0, 0))],
        out_shape=[
            jax.ShapeDtypeStruct((n, d), F32),
            jax.ShapeDtypeStruct((n, d), F32),
            jax.ShapeDtypeStruct((SUBLANES, n), jnp.int32),
            jax.ShapeDtypeStruct((n, LANES), F32),
            jax.ShapeDtypeStruct((LANES, LANES), F32),
        ],
        scratch_shapes=[pltpu.VMEM((LANES, LANES), F32)],
        compiler_params=_cparams("arbitrary"),
        name="out_proj_sample",
    )(x, mix, w_out, lw["g_ffn"], lw["w_route"], lw["b_route"], seen)


SLOT_TILE = 512


def _slot_plan(counts, n_tokens):
    n_slots = 2 * n_tokens + N_EXPERTS * SLOT_TILE
    n_slots = -(-n_slots // SLOT_TILE) * SLOT_TILE
    n_tiles = n_slots // SLOT_TILE
    padded = (counts + (SLOT_TILE - 1)) // SLOT_TILE * SLOT_TILE
    ends = jnp.cumsum(padded)
    offs = ends - padded
    total = ends[-1]
    tile_start = jnp.arange(n_tiles, dtype=jnp.int32) * SLOT_TILE
    probe = jnp.minimum(tile_start, total - 1)
    tile_expert = jnp.sum((ends[None, :] <= probe[:, None]).astype(jnp.int32), axis=1)
    tile_expert = jnp.minimum(tile_expert, N_EXPERTS - 1)
    n_valid = (total // SLOT_TILE).reshape(1)
    last_tile = jnp.where(padded > 0, ends - SLOT_TILE, 0)
    return n_slots, offs, tile_expert, n_valid, last_tile, (padded > 0).astype(jnp.int32)


def _dispatch_kernel(last_ref, used_ref, nv_ref, pos1_ref, pos2_ref, xnp_ref, xns_ref, xs_ref,
                     src_ref, zero_ref, zsem, sem, *, n_prompt_tiles):
    tm = xnp_ref.shape[0]
    n_tiles = xs_ref.shape[0] // SLOT_TILE
    i = pl.program_id(0)

    @pl.when(i == 0)
    def _():
        zero_ref[...] = jnp.zeros_like(zero_ref)
        tile = lambda row: pltpu.make_async_copy(
            zero_ref, xs_ref.at[pl.ds(pl.multiple_of(row, SLOT_TILE), SLOT_TILE), :], zsem)
        for e in range(N_EXPERTS):
            @pl.when(used_ref[e] != 0)
            def _():
                tile(last_ref[e]).start()
        lax.fori_loop(nv_ref[0], n_tiles, lambda j, c: (tile(j * SLOT_TILE).start(), c)[1], 0)
        for e in range(N_EXPERTS):
            @pl.when(used_ref[e] != 0)
            def _():
                tile(0).wait()
        lax.fori_loop(nv_ref[0], n_tiles, lambda j, c: (tile(0).wait(), c)[1], 0)

    src_ref[...] = jnp.where(i < n_prompt_tiles, xnp_ref[...], xns_ref[...])
    for t in range(tm):
        _row_copy(src_ref, t, xs_ref, pos1_ref[t], sem).start(priority=0)
        _row_copy(src_ref, t, xs_ref, pos2_ref[t], sem).start(priority=1)
    for _ in range(2 * tm):
        _row_copy(src_ref, 0, xs_ref, 0, sem).wait()


def _dispatch(xn_p, xn_s, pos1, pos2, n_slots, last_tile, used, n_valid):
    n_p, d = xn_p.shape
    n_s = xn_s.shape[0]
    tm = ROUTE_TILE
    tp = n_p // tm
    ts_last = n_s // tm - 1
    smem_rows = lambda: pl.BlockSpec((tm,), lambda i, *_: (i,), memory_space=pltpu.SMEM)
    grid_spec = pltpu.PrefetchScalarGridSpec(
        num_scalar_prefetch=3,
        grid=((n_p + n_s) // tm,),
        in_specs=[smem_rows(), smem_rows(),
                  pl.BlockSpec((tm, d), lambda i, *_: (jnp.minimum(i, tp - 1), 0)),
                  pl.BlockSpec((tm, d), lambda i, *_: (jnp.clip(i - tp, 0, ts_last), 0))],
        out_specs=pl.BlockSpec(memory_space=pl.ANY),
        scratch_shapes=[pltpu.VMEM((tm, d), F32), pltpu.VMEM((SLOT_TILE, d), F32),
                        pltpu.SemaphoreType.DMA, pltpu.SemaphoreType.DMA],
    )
    return pl.pallas_call(
        functools.partial(_dispatch_kernel, n_prompt_tiles=tp),
        grid_spec=grid_spec,
        out_shape=jax.ShapeDtypeStruct((n_slots, d), F32),
        compiler_params=_cparams("arbitrary"),
        name="moe_dispatch",
    )(last_tile, used, n_valid, pos1, pos2, xn_p, xn_s)


def _experts_kernel(te_ref, nv_ref, xs_ref, w1_ref, w3_ref, w2_ref, ys_ref, w1b_ref, w3b_ref, w2b_ref):
    j = pl.program_id(0)
    fresh = jnp.logical_or(j == 0, te_ref[j] != te_ref[jnp.maximum(j - 1, 0)])

    @pl.when(fresh)
    def _():
        w1b_ref[...] = w1_ref[0, 0].astype(BF16)
        w3b_ref[...] = w3_ref[0, 0].astype(BF16)
        w2b_ref[...] = w2_ref[0, 0].astype(BF16)

    @pl.when(j < nv_ref[0])
    def _():
        x = xs_ref[...].astype(BF16)
        h1 = jnp.dot(x, w1b_ref[...], preferred_element_type=F32)
        h3 = jnp.dot(x, w3b_ref[...], preferred_element_type=F32)
        h = (_silu(h1) * h3).astype(BF16)
        ys_ref[...] = jnp.dot(h, w2b_ref[...], preferred_element_type=F32)

    @pl.when(j >= nv_ref[0])
    def _():
        ys_ref[...] = jnp.zeros_like(ys_ref)


def _experts(xs, tile_expert, n_valid, w1, w3, w2, layer):
    n_slots, d = xs.shape
    d_e = w1.shape[3]
    tile = lambda j, te, nv: (jnp.minimum(j, nv[0] - 1), 0)
    expert = lambda j, te, nv: (layer, te[j], 0, 0)
    grid_spec = pltpu.PrefetchScalarGridSpec(
        num_scalar_prefetch=2,
        grid=(n_slots // SLOT_TILE,),
        in_specs=[
            pl.BlockSpec((SLOT_TILE, d), tile),
            pl.BlockSpec((1, 1, d, d_e), expert),
            pl.BlockSpec((1, 1, d, d_e), expert),
            pl.BlockSpec((1, 1, d_e, d), expert),
        ],
        out_specs=pl.BlockSpec((SLOT_TILE, d), lambda j, te, nv: (j, 0)),
        scratch_shapes=[pltpu.VMEM((d, d_e), BF16), pltpu.VMEM((d, d_e), BF16), pltpu.VMEM((d_e, d), BF16)],
    )
    return pl.pallas_call(
        _experts_kernel,
        grid_spec=grid_spec,
        out_shape=jax.ShapeDtypeStruct((n_slots, d), F32),
        compiler_params=_cparams("arbitrary"),
        name="moe_experts",
    )(tile_expert, n_valid, xs, w1, w3, w2)


def _combine_kernel(pos1_ref, pos2_ref, xmid_ref, rg_ref, gfin_ref, ys_ref, o_ref, y1_ref, y2_ref, sem,
                    *, final_norm):
    tm = xmid_ref.shape[0]
    for t in range(tm):
        _row_copy(ys_ref, pos1_ref[t], y1_ref, t, sem).start(priority=0)
        _row_copy(ys_ref, pos2_ref[t], y2_ref, t, sem).start(priority=1)
    for _ in range(2 * tm):
        _row_copy(ys_ref, 0, y1_ref, 0, sem).wait()
    rg = rg_ref[...]
    out = xmid_ref[...] + (rg[:, 0:1] * y1_ref[...] + rg[:, 1:2] * y2_ref[...])
    o_ref[...] = _rms(out, gfin_ref[...]) if final_norm else out


def _combine(xmid, rg, pos1, pos2, row0, ys, g_final, final_norm):
    n, d = xmid.shape
    tm = ROUTE_TILE
    t0 = row0 // tm
    smem_rows = lambda: pl.BlockSpec((tm,), lambda i: (t0 + i,), memory_space=pltpu.SMEM)
    kern = functools.partial(_combine_kernel, final_norm=final_norm)
    return pl.pallas_call(
        kern,
        grid=(n // tm,),
        in_specs=[smem_rows(), smem_rows(),
                  pl.BlockSpec((tm, d), lambda i: (i, 0)),
                  pl.BlockSpec((tm, LANES), lambda i: (i, 0)),
                  _resident((1, d), (0, 0)),
                  pl.BlockSpec(memory_space=pl.ANY)],
        out_specs=pl.BlockSpec((tm, d), lambda i: (i, 0)),
        out_shape=jax.ShapeDtypeStruct((n, d), F32),
        scratch_shapes=[pltpu.VMEM((tm, d), F32), pltpu.VMEM((tm, d), F32), pltpu.SemaphoreType.DMA],
        compiler_params=_cparams("arbitrary"),
        name="moe_combine",
    )(pos1, pos2, xmid, rg, g_final, ys)


def _moe_experts(prompt, sample, counts, w1, w3, w2, layer):
    xn_p, rt_p = prompt
    xn_s, rt_s = sample
    n = xn_p.shape[0] + xn_s.shape[0]
    n_slots, offs, tile_expert, n_valid, last_tile, used = _slot_plan(
        counts[:N_EXPERTS, 0].astype(jnp.int32), n)
    rt = jnp.concatenate([rt_p, rt_s], axis=1)
    experts = jnp.arange(N_EXPERTS, dtype=jnp.int32)[:, None]
    base = lambda e: jnp.sum(jnp.where(experts == e[None, :], offs[:, None], 0), axis=0)
    pos1 = base(rt[0]) + rt[2]
    pos2 = base(rt[1]) + rt[3]
    xs = _dispatch(xn_p, xn_s, pos1, pos2, n_slots, last_tile, used, n_valid)
    return _experts(xs, tile_expert, n_valid, w1, w3, w2, layer), pos1, pos2


def kernel(x_prompt, x_sample, state_lru_h, state_lru_conv, state_conv, state_ret, norm_mix_g, norm_ffn_g, final_norm_g, w_in, lru_conv_w, lru_conv_b, lru_wa, lru_ba, lru_wx, lru_bx, lru_lambda, cm_dw_w, cm_dw_b, cm_ln_g, cm_ln_b, ret_norm_g, w_out, moe_w_group, moe_b_group, moe_w_expert, moe_b_expert, moe_w1, moe_w3, moe_w2):
    bp, tp, d = x_prompt.shape
    bs, ts, _ = x_sample.shape
    depth = w_in.shape[0]
    w_a = lru_lambda.shape[1]
    w_b = cm_dw_b.shape[1]
    w_c = ret_norm_g.shape[1]
    n_p, n_s = bp * tp, bs * ts
    row = lambda v: v.reshape(1, -1)

    x_p, x_s = x_prompt.reshape(n_p, d), x_sample.reshape(n_s, d)
    outs = {k: [] for k in ("p_h", "p_ca", "p_cb", "p_s", "s_h", "s_ca", "s_cb")}
    s_states = None
    combine_p = None
    for l in range(depth):
        w_route, b_route = _route_weights(moe_w_group[l], moe_b_group[l], moe_w_expert[l], moe_b_expert[l])
        lw = dict(
            w_a=w_a, w_b=w_b, w_c=w_c, g_mix=row(norm_mix_g[l]), g_ffn=row(norm_ffn_g[l]),
            conv_a_w=lru_conv_w[l], conv_a_b=row(lru_conv_b[l]),
            wg=_blockdiag_gate_weights(lru_wa[l], lru_wx[l]),
            bg=row(jnp.concatenate([lru_ba[l], lru_bx[l]])), lam=row(lru_lambda[l]),
            conv_b_w=cm_dw_w[l], conv_b_b=row(cm_dw_b[l]), ln_g=row(cm_ln_g[l]), ln_b=row(cm_ln_b[l]),
            ret_g=row(ret_norm_g[l]), w_route=w_route, b_route=b_route,
        )
        parts = 1
        xmid_p, xn_p, rt_p, rg_p, seen, h_p, ca_p, cb_p, s_p = _prompt_layer(
            x_p, bp, tp, l, w_in, w_out, lw, parts, combine_p)
        outs["p_h"].append(h_p[:, 0, :])
        outs["p_ca"].append(ca_p[:, SUBLANES - (CONV_A - 1):, :])
        outs["p_cb"].append(cb_p[:, 4 * SUBLANES - (CONV_B - 1):, :])
        outs["p_s"].append(_unpair_state(s_p))
        proj_s = _in_proj(x_s, lw["g_mix"], w_in, l, parts)
        p_t = proj_s[:, :2 * w_a + 2 * w_b].reshape(bs, ts, -1).transpose(1, 0, 2)
        oab_t, h_s, ca_t, cb_t = _mix_sample_ab(
            p_t, state_lru_h, state_lru_conv[l].transpose(1, 0, 2), state_conv[l].transpose(1, 0, 2), l, lw,
            parts)
        oab = oab_t.transpose(1, 0, 2).reshape(n_s, w_a + w_b)
        mix_s, s_states = _mix_sample_c(proj_s, bs, ts, oab, state_ret, l, s_states, lw, parts)
        outs["s_h"].append(h_s)
        outs["s_ca"].append(ca_t.transpose(1, 0, 2))
        outs["s_cb"].append(cb_t.transpose(1, 0, 2))
        xmid_s, xn_s, rt_s, rg_s, counts = _out_proj(x_s, mix_s, w_out, l, seen, lw, parts)
        ys, pos1, pos2 = _moe_experts((xn_p, rt_p), (xn_s, rt_s), counts, moe_w1, moe_w3, moe_w2, l)
        last = l == depth - 1
        x_s = _combine(xmid_s, rg_s, pos1, pos2, n_p, ys, row(final_norm_g), last)
        if last:
            x_p = _combine(xmid_p, rg_p, pos1, pos2, 0, ys, row(final_norm_g), last)
        else:
            x_p, combine_p = xmid_p, (rg_p, pos1, pos2, ys)
    st = lambda k: jnp.stack(outs[k])
    return (x_p.reshape(bp, tp, d), x_s.reshape(bs, ts, d), st("p_h"), st("p_ca"), st("p_cb"), st("p_s"),
            st("s_h"), st("s_ca"), st("s_cb"), s_states)
```
